```python
import math
import jax
import jax.numpy as jnp
from jax import lax
import numpy as np

D_MODEL = 4096
BATCH = 1
SEQ = 8192
DEPTH = 2

HEAD_DIM = 128
A_HEADS = 8
A_VDIM = 2 * HEAD_DIM
B_HEADS = 16
B_CONFIGS = ((128, 1), (512, 4), (2048, 16))
C_HEADS = 32
C_KV_GROUPS = 2
C_HPG = C_HEADS // C_KV_GROUPS
C_CMP_LEN = 32
C_CMP_STRIDE = 16
C_CMP_HIDDEN = 256
C_SEL_LEN = 64
C_SEL_TOPK = 16
C_WINDOW = 512
REL_BUCKETS = 32
REL_MAX_DIST = 2048
REL_HEADS = 32
D_FF = 11008
CONV_WIDTH = 3

Q_BLOCK = 128
EPS = 1e-6
NEG = -1e30
FORCE = 1e9

A_QK_W = 2 * A_HEADS * HEAD_DIM
A_V_W = A_HEADS * A_VDIM
B_W = B_HEADS * HEAD_DIM
EV_IN = 2 * A_QK_W + A_V_W + 3 * B_W
C_Q_W = C_HEADS * HEAD_DIM
C_KV_W = C_KV_GROUPS * HEAD_DIM
OD_IN = C_Q_W + 6 * C_KV_W + 3 * C_HEADS

kernel_name = 'hybrid_diffattn_dilated_nsa_convffn'


def rms_norm(x, gain):
    xf = x.astype(jnp.float32)
    y = xf * lax.rsqrt(jnp.mean(xf * xf, axis=-1, keepdims=True) + EPS)
    return (y * gain.astype(jnp.float32)).astype(x.dtype)


def rel_bucket(dist):
    n = jnp.maximum(jnp.asarray(dist, jnp.int32), 0)
    max_exact = REL_BUCKETS // 2
    ratio = jnp.log(jnp.maximum(n, 1).astype(jnp.float32) / max_exact) / math.log(REL_MAX_DIST / max_exact)
    large = jnp.minimum(max_exact + (ratio * (REL_BUCKETS - max_exact)).astype(jnp.int32), REL_BUCKETS - 1)
    return jnp.where(n < max_exact, n, large)


def head_bias(table, dist):
    return jnp.moveaxis(table[rel_bucket(dist)], -1, 0).astype(jnp.float32)


def diff_attention(q, k, v, lam, table):
    B, S = q.shape[0], q.shape[1]
    scale = HEAD_DIM ** -0.5
    kpos = jnp.arange(S)

    def block(i):
        start = i * Q_BLOCK
        qb = lax.dynamic_slice_in_dim(q, start, Q_BLOCK, axis=1)
        dist = (start + jnp.arange(Q_BLOCK))[:, None] - kpos[None, :]
        s = jnp.einsum('bqchd,bkchd->bchqk', qb, k).astype(jnp.float32) * scale + head_bias(table, dist)
        p = jax.nn.softmax(jnp.where(dist >= 0, s, NEG), axis=-1)
        p = p[:, 0] - lam * p[:, 1]
        return jnp.einsum('bhqk,bkhe->bqhe', p.astype(v.dtype), v)

    out = lax.map(block, jnp.arange(S // Q_BLOCK))
    return out.transpose(1, 0, 2, 3, 4).reshape(B, S, A_HEADS, A_VDIM)


def dilated_group(q, k, v, table, window, dilation):
    B, S, H, hd = q.shape
    span = window // dilation
    L = S // dilation
    nb = -(-L // span)
    Lp = nb * span
    scale = HEAD_DIM ** -0.5

    def strided(x):
        x = x.reshape(B, L, dilation, H, hd).transpose(0, 2, 1, 3, 4)
        x = jnp.pad(x, ((0, 0), (0, 0), (0, Lp - L), (0, 0), (0, 0)))
        return x.reshape(B, dilation, nb, span, H, hd)

    def with_prev(x):
        prev = jnp.pad(x, ((0, 0), (0, 0), (1, 0), (0, 0), (0, 0), (0, 0)))[:, :, :-1]
        return jnp.concatenate([prev, x], axis=3)

    qb = strided(q)
    kc = with_prev(strided(k))
    vc = with_prev(strided(v))
    qi = np.arange(span)
    ki = np.arange(2 * span) - span
    j = qi[:, None] - ki[None, :]
    first_ok = (np.arange(nb)[:, None, None] * span + ki[None, None, :]) >= 0
    mask = (j >= 0)[None] & (j <= span)[None] & first_ok
    s = jnp.einsum('brnqhd,brnkhd->brnhqk', qb, kc).astype(jnp.float32) * scale
    s = s + head_bias(table, j * dilation)
    s = jnp.where(mask[:, None], s, NEG)
    lse = jax.nn.logsumexp(s, axis=-1, keepdims=True)
    p = jnp.exp(s - lse)
    o = jnp.einsum('brnhqk,brnkhd->brnqhd', p.astype(v.dtype), vc)
    o = o.reshape(B, dilation, Lp, H, hd)[:, :, :L].transpose(0, 2, 1, 3, 4).reshape(B, S, H, hd)
    lse = lse[..., 0].transpose(0, 1, 2, 4, 3).reshape(B, dilation, Lp, H)[:, :, :L]
    lse = lse.transpose(0, 2, 1, 3).reshape(B, S, H)
    return o, lse


def dilated_attention(q, k, v, table):
    outs, lses = [], []
    for window, dilation in B_CONFIGS:
        o, l = dilated_group(q, k, v, table, window, dilation)
        outs.append(o)
        lses.append(l)
    w = jax.nn.softmax(jnp.stack(lses, axis=0), axis=0)
    return jnp.einsum('gbsh,gbshd->bshd', w.astype(q.dtype), jnp.stack(outs, axis=0))


def even_mixer(h, w_in, a_qk_gain, a_lambda, a_subln_gain, b_qk_gain, w_out, rel_table, lam_init):
    B, S, _ = h.shape
    proj = h @ w_in
    cuts = [A_QK_W, 2 * A_QK_W, 2 * A_QK_W + A_V_W, 2 * A_QK_W + A_V_W + B_W, 2 * A_QK_W + A_V_W + 2 * B_W]
    aq, ak, av, bq, bk, bv = jnp.split(proj, cuts, axis=-1)
    aq = rms_norm(aq.reshape(B, S, 2, A_HEADS, HEAD_DIM), a_qk_gain[0])
    ak = rms_norm(ak.reshape(B, S, 2, A_HEADS, HEAD_DIM), a_qk_gain[1])
    av = av.reshape(B, S, A_HEADS, A_VDIM)
    lf = a_lambda.astype(jnp.float32)
    lam = jnp.exp(jnp.sum(lf[0] * lf[1])) - jnp.exp(jnp.sum(lf[2] * lf[3])) + lam_init
    ao = diff_attention(aq, ak, av, lam, rel_table[:, :A_HEADS])
    ao = rms_norm(ao, a_subln_gain) * (1.0 - lam_init)
    bq = rms_norm(bq.reshape(B, S, B_HEADS, HEAD_DIM), b_qk_gain[0])
    bk = rms_norm(bk.reshape(B, S, B_HEADS, HEAD_DIM), b_qk_gain[1])
    bv = bv.reshape(B, S, B_HEADS, HEAD_DIM)
    bo = dilated_attention(bq, bk, bv, rel_table[:, A_HEADS:A_HEADS + B_HEADS])
    o = jnp.concatenate([ao.reshape(B, S, A_V_W), bo.reshape(B, S, B_W)], axis=-1)
    return o @ w_out


def nsa_mixer(h, w_in, c_qk_gain, c_cmp_pe, c_cmp_w1, c_cmp_w2, w_out, rel_table):
    B, S, _ = h.shape
    G, HPG, hd = C_KV_GROUPS, C_HPG, HEAD_DIM
    scale = hd ** -0.5
    proj = h @ w_in
    q = rms_norm(proj[..., :C_Q_W].reshape(B, S, G, HPG, hd), c_qk_gain[0])
    kvs = proj[..., C_Q_W:C_Q_W + 6 * C_KV_W].reshape(B, S, 6, G, hd)
    kc, vc, ks, vs, kw, vw = [kvs[:, :, j] for j in range(6)]
    gates = jax.nn.sigmoid(proj[..., C_Q_W + 6 * C_KV_W:].astype(jnp.float32)).astype(h.dtype)
    gates = gates.reshape(B, S, 3, G, HPG)
    ks = rms_norm(ks, c_qk_gain[2])
    kw = rms_norm(kw, c_qk_gain[3])

    n_cmp = (S - C_CMP_LEN) // C_CMP_STRIDE + 1
    cidx = np.arange(n_cmp)[:, None] * C_CMP_STRIDE + np.arange(C_CMP_LEN)[None, :]

    def compress(x, pe, w1, w2):
        blk = x[:, cidx] + pe[:, None, :]
        flat = blk.transpose(0, 1, 3, 2, 4).reshape(B, n_cmp, G, C_CMP_LEN * hd)
        return jax.nn.gelu(flat @ w1) @ w2

    kcmp = rms_norm(compress(kc, c_cmp_pe[0], c_cmp_w1[0], c_cmp_w2[0]), c_qk_gain[1])
    vcmp = compress(vc, c_cmp_pe[1], c_cmp_w1[1], c_cmp_w2[1])
    cmp_end = cidx[:, -1]

    n_sel = S // C_SEL_LEN
    top_k = min(C_SEL_TOPK, n_sel)
    c_start = np.arange(n_cmp) * C_CMP_STRIDE
    s_start = np.arange(n_sel) * C_SEL_LEN
    overlap = ((c_start[:, None] < s_start[None, :] + C_SEL_LEN) & (c_start[:, None] + C_CMP_LEN > s_start[None, :]))
    overlap = jnp.asarray(overlap.astype(np.float32))
    ksb = ks.reshape(B, n_sel, C_SEL_LEN, G, hd).transpose(0, 3, 1, 2, 4)
    vsb = vs.reshape(B, n_sel, C_SEL_LEN, G, hd).transpose(0, 3, 1, 2, 4)
    kw_pad = jnp.pad(kw, ((0, 0), (C_WINDOW, 0), (0, 0), (0, 0)))
    vw_pad = jnp.pad(vw, ((0, 0), (C_WINDOW, 0), (0, 0), (0, 0)))

    table = rel_table[:, :C_HEADS].reshape(REL_BUCKETS, G, HPG)
    table_g = table.transpose(1, 0, 2)

    def grp_bias(dist):
        return jnp.moveaxis(table[rel_bucket(dist)], (-2, -1), (0, 1)).astype(jnp.float32)

    bidx = jnp.arange(B)[:, None, None, None]
    gidx = jnp.arange(G)[None, :, None, None]
    sel_blocks = jnp.arange(n_sel)

    def block(i):
        start = i * Q_BLOCK
        qb = lax.dynamic_slice_in_dim(q, start, Q_BLOCK, axis=1)
        gb = lax.dynamic_slice_in_dim(gates, start, Q_BLOCK, axis=1)
        t = start + jnp.arange(Q_BLOCK)
        dc = t[:, None] - cmp_end[None, :]
        mc = dc >= 0
        sc = jnp.einsum('bqghd,bkgd->bghqk', qb, kcmp).astype(jnp.float32) * scale + grp_bias(dc)
        pc = jax.nn.softmax(jnp.where(mc, sc, NEG), axis=-1) * mc
        oc = jnp.einsum('bghqk,bkgd->bqghd', pc.astype(vcmp.dtype), vcmp)
        imp = jnp.einsum('bghqk,kn->bgqn', pc, overlap)
        cur = t // C_SEL_LEN
        forced = (sel_blocks[None, :] == 0) | (sel_blocks[None, :] == cur[:, None]) | (sel_blocks[None, :] == cur[:, None] - 1)
        valid = sel_blocks[None, :] <= cur[:, None]
        score = jnp.where(valid, jnp.where(forced, FORCE, imp), NEG)
        top_score, top_idx = lax.top_k(score, top_k)
        top_ok = top_score > NEG / 2
        ksel = ksb[bidx, gidx, top_idx].reshape(B, G, Q_BLOCK, top_k * C_SEL_LEN, hd)
        vsel = vsb[bidx, gidx, top_idx].reshape(B, G, Q_BLOCK, top_k * C_SEL_LEN, hd)
        kpos = top_idx[..., None] * C_SEL_LEN + jnp.arange(C_SEL_LEN)
        ds = t[None, None, :, None, None] - kpos
        ms = ((ds >= 0) & top_ok[..., None]).reshape(B, G, 1, Q_BLOCK, top_k * C_SEL_LEN)
        bs = table_g[jnp.arange(G)[None, :, None, None, None], rel_bucket(ds)]
        bs = bs.reshape(B, G, Q_BLOCK, top_k * C_SEL_LEN, HPG).transpose(0, 1, 4, 2, 3).astype(jnp.float32)
        ss = jnp.einsum('bqghd,bgqnd->bghqn', qb, ksel).astype(jnp.float32) * scale + bs
        ps = jax.nn.softmax(jnp.where(ms, ss, NEG), axis=-1) * ms
        osel = jnp.einsum('bghqn,bgqnd->bqghd', ps.astype(vsel.dtype), vsel)
        kwb = lax.dynamic_slice_in_dim(kw_pad, start, Q_BLOCK + C_WINDOW, axis=1)
        vwb = lax.dynamic_slice_in_dim(vw_pad, start, Q_BLOCK + C_WINDOW, axis=1)
        kp = start - C_WINDOW + jnp.arange(Q_BLOCK + C_WINDOW)
        dw = t[:, None] - kp[None, :]
        mw = (dw >= 0) & (dw < C_WINDOW) & (kp[None, :] >= 0)
        sw = jnp.einsum('bqghd,bkgd->bghqk', qb, kwb).astype(jnp.float32) * scale + grp_bias(dw)
        pw = jax.nn.softmax(jnp.where(mw, sw, NEG), axis=-1)
        ow = jnp.einsum('bghqk,bkgd->bqghd', pw.astype(vwb.dtype), vwb)
        return gb[:, :, 0, :, :, None] * oc + gb[:, :, 1, :, :, None] * osel + gb[:, :, 2, :, :, None] * ow

    out = lax.map(block, jnp.arange(S // Q_BLOCK))
    out = out.transpose(1, 0, 2, 3, 4, 5).reshape(B, S, C_Q_W)
    return out @ w_out


def conv_ffn(h, w_gate, w_up, conv_w, conv_b, w_down):
    g = h @ w_gate
    u = h @ w_up
    g = lax.conv_general_dilated(g, conv_w[:, None, :], window_strides=(1,), padding=[(CONV_WIDTH - 1, 0)],
                                 dimension_numbers=('NWC', 'WIO', 'NWC'), feature_group_count=D_FF) + conv_b
    return (jax.nn.silu(g) * u) @ w_down


def setup_inputs(seed: int = 0) -> dict:
    key = jax.random.key(seed)
    k = jax.random.split(key, 22)
    NE = (DEPTH + 1) // 2
    NO = DEPTH // 2

    def nrm(kk, shape, scale):
        return jax.random.normal(kk, shape, jnp.float32) * scale

    def gain(kk, shape):
        return 1.0 + nrm(kk, shape, 0.05)

    return {
        'x': nrm(k[0], (BATCH, SEQ, D_MODEL), 1.0),
        'rel_table': nrm(k[1], (REL_BUCKETS, REL_HEADS), 0.5),
        'ev_norm': gain(k[2], (NE, D_MODEL)),
        'ev_w_in': nrm(k[3], (NE, D_MODEL, EV_IN), D_MODEL ** -0.5),
        'a_qk_gain': gain(k[4], (NE, 2, HEAD_DIM)),
        'a_lambda': nrm(k[5], (NE, 4, HEAD_DIM), 0.1),
        'a_subln_gain': gain(k[6], (NE, A_VDIM)),
        'b_qk_gain': gain(k[7], (NE, 2, HEAD_DIM)),
        'ev_w_out': nrm(k[8], (NE, A_V_W + B_W, D_MODEL), (A_V_W + B_W) ** -0.5),
        'od_norm': gain(k[9], (NO, D_MODEL)),
        'od_w_in': nrm(k[10], (NO, D_MODEL, OD_IN), D_MODEL ** -0.5),
        'c_qk_gain': gain(k[11], (NO, 4, HEAD_DIM)),
        'c_cmp_pe': nrm(k[12], (NO, 2, C_CMP_LEN, HEAD_DIM), 0.1),
        'c_cmp_w1': nrm(k[13], (NO, 2, C_CMP_LEN * HEAD_DIM, C_CMP_HIDDEN), (C_CMP_LEN * HEAD_DIM) ** -0.5),
        'c_cmp_w2': nrm(k[14], (NO, 2, C_CMP_HIDDEN, HEAD_DIM), C_CMP_HIDDEN ** -0.5),
        'od_w_out': nrm(k[15], (NO, C_Q_W, D_MODEL), C_Q_W ** -0.5),
        'ffn_norm': gain(k[16], (DEPTH, D_MODEL)),
        'ffn_w_gate': nrm(k[17], (DEPTH, D_MODEL, D_FF), D_MODEL ** -0.5),
        'ffn_w_up': nrm(k[18], (DEPTH, D_MODEL, D_FF), D_MODEL ** -0.5),
        'ffn_conv_w': nrm(k[19], (DEPTH, CONV_WIDTH, D_FF), CONV_WIDTH ** -0.5),
        'ffn_conv_b': nrm(k[20], (DEPTH, D_FF), 0.01),
        'ffn_w_down': nrm(k[21], (DEPTH, D_FF, D_MODEL), D_FF ** -0.5),
    }


def reference(x, rel_table, ev_norm, ev_w_in, a_qk_gain, a_lambda, a_subln_gain, b_qk_gain, ev_w_out,
              od_norm, od_w_in, c_qk_gain, c_cmp_pe, c_cmp_w1, c_cmp_w2, od_w_out,
              ffn_norm, ffn_w_gate, ffn_w_up, ffn_conv_w, ffn_conv_b, ffn_w_down):
    for i in range(DEPTH):
        if i % 2 == 0:
            e = i // 2
            lam_init = 0.8 - 0.6 * math.exp(-0.3 * i)
            x = x + even_mixer(rms_norm(x, ev_norm[e]), ev_w_in[e], a_qk_gain[e], a_lambda[e], a_subln_gain[e],
                               b_qk_gain[e], ev_w_out[e], rel_table, lam_init)
        else:
            o = i // 2
            x = x + nsa_mixer(rms_norm(x, od_norm[o]), od_w_in[o], c_qk_gain[o], c_cmp_pe[o], c_cmp_w1[o],
                              c_cmp_w2[o], od_w_out[o], rel_table)
        x = x + conv_ffn(rms_norm(x, ffn_norm[i]), ffn_w_gate[i], ffn_w_up[i], ffn_conv_w[i], ffn_conv_b[i],
                         ffn_w_down[i])
    return x
```

```python
import functools
import math

import numpy as np
import jax
import jax.numpy as jnp
from jax import lax
from jax.experimental import pallas as pl
from jax.experimental.pallas import tpu as pltpu

HEAD_DIM = 128
A_HEADS = 8
B_HEADS = 16
B_CONFIGS = ((128, 1), (512, 4), (2048, 16))
C_HEADS = 32
C_GROUPS = 2
C_HPG = C_HEADS // C_GROUPS
CMP_LEN = 32
CMP_STRIDE = 16
SEL_LEN = 64
SEL_TOPK = 16
C_WINDOW = 512
REL_BUCKETS = 32
REL_MAX_DIST = 2048
EPS = 1e-6
NEG = -1e30
FORCE = 1e9

LANES = 128
VMEM_LIMIT = 56 * 1024 * 1024

F32 = jnp.float32
BF16 = jnp.bfloat16


def _cparams(sem):
    return pltpu.CompilerParams(dimension_semantics=sem, vmem_limit_bytes=VMEM_LIMIT)


def _dot(a, b):
    return jnp.dot(a, b, preferred_element_type=F32)


def _dot_nt(a, b):
    return lax.dot_general(a, b, (((1,), (1,)), ((), ())), preferred_element_type=F32)


def _bucket_np(dist):
    n = np.maximum(dist, 0).astype(np.int32)
    max_exact = REL_BUCKETS // 2
    nf = np.maximum(n, 1).astype(np.float32)
    ratio = np.log(nf / np.float32(max_exact)) / np.float32(math.log(REL_MAX_DIST / max_exact))
    large = np.minimum(max_exact + (ratio * np.float32(REL_BUCKETS - max_exact)).astype(np.int32),
                       REL_BUCKETS - 1)
    return np.where(n < max_exact, n, large).astype(np.int32)


def _far_distance():
    b = _bucket_np(np.arange(4 * REL_MAX_DIST))
    return int(np.max(np.nonzero(b != REL_BUCKETS - 1)[0])) + 1


FAR = _far_distance()


def _bias_tiles(table_t, dist, valid):
    b = jnp.take(table_t, jnp.asarray(_bucket_np(dist)), axis=1)
    return jnp.where(jnp.asarray(valid), b, NEG).astype(F32)


def _toeplitz_dist(n_tiles, t):
    i = np.arange(t)[:, None]
    j = np.arange(t)[None, :]
    return np.stack([d * t + i - j for d in range(n_tiles)])


def _rmsnorm_kernel(x_ref, g_ref, o_ref):
    x = x_ref[...]
    ms = jnp.mean(x * x, axis=-1, keepdims=True)
    o_ref[...] = ((x * lax.rsqrt(ms + EPS)) * g_ref[...]).astype(o_ref.dtype)


def rmsnorm(x, gain):
    s, d = x.shape
    tr = min(256, s)
    return pl.pallas_call(
        _rmsnorm_kernel,
        out_shape=jax.ShapeDtypeStruct((s, d), BF16),
        grid=(s // tr,),
        in_specs=[pl.BlockSpec((tr, d), lambda i: (i, 0)), pl.BlockSpec((1, d), lambda i: (0, 0))],
        out_specs=pl.BlockSpec((tr, d), lambda i: (i, 0)),
        name="rmsnorm",
        compiler_params=_cparams(("parallel",)),
    )(x, gain.reshape(1, d).astype(F32))


def _proj_kernel(x_ref, w_ref, gain_ref, flag_ref, o_ref, *, nblk):
    acc = _dot(x_ref[...], w_ref[...])
    for c in range(nblk):
        sl = slice(c * LANES, (c + 1) * LANES)
        blk = acc[:, sl]
        ms = jnp.mean(blk * blk, axis=-1, keepdims=True)
        r = jnp.where(flag_ref[:, sl] > 0, lax.rsqrt(ms + EPS), 1.0)
        o_ref[c] = ((blk * r) * gain_ref[:, sl]).astype(o_ref.dtype)


def proj_headmajor(x, w, gain, flag, tn):
    s, k = x.shape
    n = w.shape[1]
    tm = min(1024, s)
    nblk = tn // LANES
    return pl.pallas_call(
        functools.partial(_proj_kernel, nblk=nblk),
        out_shape=jax.ShapeDtypeStruct((n // LANES, s, LANES), BF16),
        grid=(s // tm, n // tn),
        in_specs=[pl.BlockSpec((tm, k), lambda i, j: (i, 0)),
                  pl.BlockSpec((k, tn), lambda i, j: (0, j)),
                  pl.BlockSpec((1, tn), lambda i, j: (0, j)),
                  pl.BlockSpec((1, tn), lambda i, j: (0, j))],
        out_specs=pl.BlockSpec((nblk, tm, LANES), lambda i, j: (j, i, 0)),
        name="proj_headmajor",
        compiler_params=_cparams(("parallel", "parallel")),
    )(x, w, gain.reshape(1, n).astype(F32), flag.reshape(1, n).astype(F32))


def _mm_res_kernel(x_ref, w_ref, r_ref, o_ref):
    o_ref[...] = r_ref[...] + _dot(x_ref[...], w_ref[...])


def matmul_residual(x, w, res, tm, tn):
    s, k = x.shape
    n = w.shape[1]
    tm = min(tm, s)
    tn = min(tn, n)
    return pl.pallas_call(
        _mm_res_kernel,
        out_shape=jax.ShapeDtypeStruct((s, n), F32),
        grid=(s // tm, n // tn),
        in_specs=[pl.BlockSpec((tm, k), lambda i, j: (i, 0), pipeline_mode=pl.Buffered(1)),
                  pl.BlockSpec((k, tn), lambda i, j: (0, j)),
                  pl.BlockSpec((tm, tn), lambda i, j: (i, j))],
        out_specs=pl.BlockSpec((tm, tn), lambda i, j: (i, j)),
        name="matmul_residual",
        compiler_params=_cparams(("parallel", "arbitrary")),
    )(x, w, res)


def _mm2_res_kernel(x1_ref, x2_ref, w_ref, r_ref, o_ref, *, k1):
    acc = _dot(x1_ref[...], w_ref[0:k1, :]) + _dot(x2_ref[...], w_ref[k1:, :])
    o_ref[...] = r_ref[...] + acc


def matmul2_residual(x1, x2, w, res, tm, tn):
    s, k1 = x1.shape
    k2 = x2.shape[1]
    n = w.shape[1]
    tm = min(tm, s)
    tn = min(tn, n)
    return pl.pallas_call(
        functools.partial(_mm2_res_kernel, k1=k1),
        out_shape=jax.ShapeDtypeStruct((s, n), F32),
        grid=(s // tm, n // tn),
        in_specs=[pl.BlockSpec((tm, k1), lambda i, j: (i, 0)),
                  pl.BlockSpec((tm, k2), lambda i, j: (i, 0)),
                  pl.BlockSpec((k1 + k2, tn), lambda i, j: (0, j)),
                  pl.BlockSpec((tm, tn), lambda i, j: (i, j))],
        out_specs=pl.BlockSpec((tm, tn), lambda i, j: (i, j)),
        name="matmul2_residual",
        compiler_params=_cparams(("parallel", "arbitrary")),
    )(x1, x2, w, res)


def _ffn1_kernel(x_ref, wg_ref, wu_ref, cw_ref, cb_ref, o_ref, carry_ref):
    i = pl.program_id(0)
    j = pl.program_id(1)
    x = x_ref[...]
    g = _dot(x, wg_ref[...])
    u = _dot(x, wu_ref[...])
    tm = g.shape[0]

    @pl.when(i == 0)
    def _():
        carry_ref[j] = jnp.zeros(carry_ref.shape[1:], F32)

    prev = carry_ref[j]
    carry_ref[j] = g[tm - 8:tm, :]
    row = lax.broadcasted_iota(jnp.int32, g.shape, 0)
    p7 = prev[7:8, :]
    p6 = prev[6:7, :]
    g1 = jnp.where(row == 0, p7, pltpu.roll(g, 1, 0))
    g2 = jnp.where(row == 0, p6, jnp.where(row == 1, p7, pltpu.roll(g, 2, 0)))
    cw = cw_ref[...]
    gc = g2 * cw[0:1, :] + g1 * cw[1:2, :] + g * cw[2:3, :] + cb_ref[...]
    o_ref[...] = ((gc * jax.nn.sigmoid(gc)) * u).astype(o_ref.dtype)


def ffn_gate_up(h, wg, wu, conv_w, conv_b):
    s, d = h.shape
    f = wg.shape[1]
    tm = min(1024, s)
    tn = 256
    nj = f // tn
    return pl.pallas_call(
        _ffn1_kernel,
        out_shape=jax.ShapeDtypeStruct((s, f), BF16),
        grid=(s // tm, nj),
        in_specs=[pl.BlockSpec((tm, d), lambda i, j: (i, 0)),
                  pl.BlockSpec((d, tn), lambda i, j: (0, j)),
                  pl.BlockSpec((d, tn), lambda i, j: (0, j)),
                  pl.BlockSpec((3, tn), lambda i, j: (0, j)),
                  pl.BlockSpec((1, tn), lambda i, j: (0, j))],
        out_specs=pl.BlockSpec((tm, tn), lambda i, j: (i, j)),
        scratch_shapes=[pltpu.VMEM((nj, 8, tn), F32)],
        name="ffn_gate_up",
        compiler_params=_cparams(("arbitrary", "arbitrary")),
    )(h, wg, wu, conv_w.astype(F32), conv_b.reshape(1, f).astype(F32))


def _flash_kernel(qi_ref, kj_ref, bi_ref, fl_ref, *refs, hb, nv, use_sel):
    if use_sel:
        q_ref, k_ref, v_ref, b_ref, sel_ref, e_ref, o_ref, m_sc, l_sc, acc_sc = refs
    else:
        q_ref, k_ref, v_ref, b_ref, o_ref, m_sc, l_sc, acc_sc = refs
    p = pl.program_id(1)
    flags = fl_ref[p]
    dv = nv * LANES

    @pl.when((flags & 1) != 0)
    def _():
        m_sc[...] = jnp.full(m_sc.shape, -jnp.inf, F32)
        l_sc[...] = jnp.zeros(l_sc.shape, F32)
        acc_sc[...] = jnp.zeros(acc_sc.shape, F32)

    k = k_ref[0]
    if nv == 1:
        v = v_ref[0]
    else:
        v = jnp.concatenate([v_ref[c] for c in range(nv)], axis=-1)
    if use_sel:
        negm = (_dot(sel_ref[0], e_ref[0]) - 1.0) * (-NEG)

    def head(h, carry):
        s = _dot_nt(q_ref[h], k) + b_ref[h, 0]
        if use_sel:
            s = s + negm
        m_old = m_sc[h]
        m_new = jnp.maximum(m_old, jnp.max(s, axis=-1, keepdims=True))
        alpha = jnp.exp(m_old - m_new)
        e = jnp.exp(s - m_new)
        l_sc[h] = alpha * l_sc[h] + jnp.sum(e, axis=-1, keepdims=True)
        acc_sc[h] = alpha * acc_sc[h] + _dot(e.astype(BF16), v)
        m_sc[h] = m_new
        return carry

    if hb == 1:
        head(0, 0)
    else:
        lax.fori_loop(0, hb, head, 0)

    @pl.when((flags & 2) != 0)
    def _():
        for h in range(hb):
            o_ref[:, h * dv:(h + 1) * dv] = acc_sc[h] / l_sc[h]


def _pairs(nq, lookback, max_bias):
    qi, kj, bi, fl = [], [], [], []
    for q in range(nq):
        lo = 0 if lookback is None else max(0, q - lookback)
        for k in range(lo, q + 1):
            qi.append(q)
            kj.append(k)
            bi.append(min(q - k, max_bias))
            fl.append((1 if k == lo else 0) | (2 if k == q else 0))
    return [jnp.asarray(np.asarray(a, np.int32)) for a in (qi, kj, bi, fl)]


def flash_attention(q, k, v, bias, *, name, t, n_groups, hb, nv, q_map, k_map, v_map, b_map,
                    lookback=None, sel=None, expand=None):
    s = q.shape[1]
    nq = s // t
    qi, kj, bi, fl = _pairs(nq, lookback, bias.shape[1] - 1)
    n_pairs = int(qi.shape[0])
    dv = nv * LANES
    use_sel = sel is not None
    in_specs = [
        pl.BlockSpec((hb, t, LANES), lambda g, p, qi, kj, bi, fl: q_map(g, qi[p])),
        pl.BlockSpec((1, t, LANES), lambda g, p, qi, kj, bi, fl: k_map(g, kj[p])),
        pl.BlockSpec((nv, t, LANES), lambda g, p, qi, kj, bi, fl: v_map(g, kj[p])),
        pl.BlockSpec((hb, 1, t, t), lambda g, p, qi, kj, bi, fl: b_map(g, bi[p])),
    ]
    args = [q, k, v, bias]
    if use_sel:
        nselp = sel.shape[2]
        in_specs += [
            pl.BlockSpec((1, t, nselp), lambda g, p, qi, kj, bi, fl: (g, qi[p], 0)),
            pl.BlockSpec((1, nselp, t), lambda g, p, qi, kj, bi, fl: (kj[p], 0, 0)),
        ]
        args += [sel, expand]
    grid_spec = pltpu.PrefetchScalarGridSpec(
        num_scalar_prefetch=4,
        grid=(n_groups, n_pairs),
        in_specs=in_specs,
        out_specs=pl.BlockSpec((t, hb * dv), lambda g, p, qi, kj, bi, fl: (qi[p], g)),
        scratch_shapes=[pltpu.VMEM((hb, t, 1), F32), pltpu.VMEM((hb, t, 1), F32),
                        pltpu.VMEM((hb, t, dv), F32)],
    )
    return pl.pallas_call(
        functools.partial(_flash_kernel, hb=hb, nv=nv, use_sel=use_sel),
        out_shape=jax.ShapeDtypeStruct((s, n_groups * hb * dv), F32),
        grid_spec=grid_spec,
        name=name,
        compiler_params=_cparams(("parallel", "arbitrary")),
    )(qi, kj, bi, fl, *args)


def _apost_kernel(o_ref, lam_ref, g_ref, out_ref, *, lam_init):
    lf = lam_ref[...]
    s1 = jnp.sum(lf[0:1, :] * lf[1:2, :], axis=-1, keepdims=True)
    s2 = jnp.sum(lf[2:3, :] * lf[3:4, :], axis=-1, keepdims=True)
    lam = jnp.exp(s1) - jnp.exp(s2) + lam_init
    dv = 2 * HEAD_DIM
    for h in range(A_HEADS):
        o = o_ref[:, h * dv:(h + 1) * dv] - lam * o_ref[:, (A_HEADS + h) * dv:(A_HEADS + h + 1) * dv]
        ms = jnp.mean(o * o, axis=-1, keepdims=True)
        y = ((o * lax.rsqrt(ms + EPS)) * g_ref[...]) * (1.0 - lam_init)
        out_ref[:, h * dv:(h + 1) * dv] = y.astype(out_ref.dtype)


def diff_post(o, a_lambda, subln_gain, lam_init):
    s, w = o.shape
    ts = min(256, s)
    dv = 2 * HEAD_DIM
    return pl.pallas_call(
        functools.partial(_apost_kernel, lam_init=lam_init),
        out_shape=jax.ShapeDtypeStruct((s, w // 2), BF16),
        grid=(s // ts,),
        in_specs=[pl.BlockSpec((ts, w), lambda i: (i, 0)),
                  pl.BlockSpec((4, HEAD_DIM), lambda i: (0, 0)),
                  pl.BlockSpec((1, dv), lambda i: (0, 0))],
        out_specs=pl.BlockSpec((ts, w // 2), lambda i: (i, 0)),
        name="diff_post",
        compiler_params=_cparams(("parallel",)),
    )(o, a_lambda.astype(F32), subln_gain.reshape(1, dv).astype(F32))


def _dilated_kernel(q_ref, kp_ref, kc_ref, vp_ref, vc_ref, b_ref, o_ref, lse_ref):
    span = q_ref.shape[1]
    for h in range(B_HEADS):
        q = q_ref[h]
        sp = _dot_nt(q, kp_ref[h]) + b_ref[0, h, :, 0:span]
        sc = _dot_nt(q, kc_ref[h]) + b_ref[0, h, :, span:2 * span]
        m = jnp.maximum(jnp.max(sp, axis=-1, keepdims=True), jnp.max(sc, axis=-1, keepdims=True))
        ep = jnp.exp(sp - m)
        ec = jnp.exp(sc - m)
        l = jnp.sum(ep, axis=-1, keepdims=True) + jnp.sum(ec, axis=-1, keepdims=True)
        o = (_dot(ep.astype(BF16), vp_ref[h]) + _dot(ec.astype(BF16), vc_ref[h])) / l
        sl = slice(h * HEAD_DIM, (h + 1) * HEAD_DIM)
        o_ref[:, sl] = o
        lse_ref[:, sl] = jnp.broadcast_to(m + jnp.log(l), o.shape)


def dilated_group(proj_hm, bias, dilation, span, q_blk, k_blk, v_blk):
    nh, s, _ = proj_hm.shape
    l = s // dilation
    nb = l // span
    x = proj_hm.reshape(nh, l, dilation * LANES)
    w = B_HEADS * HEAD_DIM
    hspec = lambda blk, prev: pl.BlockSpec(
        (B_HEADS, span, LANES),
        (lambda r, n: (blk, jnp.maximum(n - 1, 0), r)) if prev else (lambda r, n: (blk, n, r)))
    o, lse = pl.pallas_call(
        _dilated_kernel,
        out_shape=[jax.ShapeDtypeStruct((l, dilation * w), F32)] * 2,
        grid=(dilation, nb),
        in_specs=[hspec(q_blk, False), hspec(k_blk, True), hspec(k_blk, False),
                  hspec(v_blk, True), hspec(v_blk, False),
                  pl.BlockSpec((1, B_HEADS, span, 2 * span), lambda r, n: (jnp.minimum(n, 1), 0, 0, 0))],
        out_specs=[pl.BlockSpec((span, w), lambda r, n: (n, r))] * 2,
        name="dilated_group",
        compiler_params=_cparams(("parallel", "parallel")),
    )(x, x, x, x, x, bias)
    return o.reshape(s, w), lse.reshape(s, w)


def _mix3_kernel(o1, o2, o3, l1, l2, l3, out_ref):
    a, b, c = l1[...], l2[...], l3[...]
    m = jnp.maximum(jnp.maximum(a, b), c)
    ea, eb, ec = jnp.exp(a - m), jnp.exp(b - m), jnp.exp(c - m)
    out_ref[...] = ((ea * o1[...] + eb * o2[...] + ec * o3[...]) / (ea + eb + ec)).astype(out_ref.dtype)


def mix_dilated(os, lses):
    s, w = os[0].shape
    ts = min(256, s)
    spec = pl.BlockSpec((ts, w), lambda i: (i, 0))
    return pl.pallas_call(
        _mix3_kernel,
        out_shape=jax.ShapeDtypeStruct((s, w), BF16),
        grid=(s // ts,),
        in_specs=[spec] * 6,
        out_specs=spec,
        name="mix_dilated",
        compiler_params=_cparams(("parallel",)),
    )(*os, *lses)


def _compress_kernel(x_ref, pe_ref, w1_ref, w2_ref, gain_ref, o_ref):
    kv = pl.program_id(0)
    x = x_ref[0, 0].astype(F32)
    nc = x.shape[0]
    a = _dot((x + pe_ref[0, 0]).astype(BF16), w1_ref[0, 0])
    b = _dot((x + pe_ref[0, 1]).astype(BF16), w1_ref[0, 1])
    hid = jax.nn.gelu(a + pltpu.roll(b, nc - 1, 0))
    c = _dot(hid.astype(BF16), w2_ref[0])
    ms = jnp.mean(c * c, axis=-1, keepdims=True)
    normed = (c * lax.rsqrt(ms + EPS)) * gain_ref[...]
    o_ref[0, 0] = jnp.where(kv == 0, normed, c).astype(o_ref.dtype)


def compress_kv(xc, pe, w1, w2, gain):
    _, g, nc, cw = xc.shape
    hid = w1.shape[-1]
    return pl.pallas_call(
        _compress_kernel,
        out_shape=jax.ShapeDtypeStruct((2, g, nc, HEAD_DIM), BF16),
        grid=(2, g),
        in_specs=[pl.BlockSpec((1, 1, nc, cw), lambda a, b: (a, b, 0, 0)),
                  pl.BlockSpec((1, 2, 1, cw), lambda a, b: (a, 0, 0, 0)),
                  pl.BlockSpec((1, 2, cw, hid), lambda a, b: (a, 0, 0, 0)),
                  pl.BlockSpec((1, hid, HEAD_DIM), lambda a, b: (a, 0, 0)),
                  pl.BlockSpec((1, HEAD_DIM), lambda a, b: (0, 0))],
        out_specs=pl.BlockSpec((1, 1, nc, HEAD_DIM), lambda a, b: (a, b, 0, 0)),
        name="compress_kv",
        compiler_params=_cparams(("parallel", "parallel")),
    )(xc, pe, w1, w2, gain.reshape(1, HEAD_DIM).astype(F32))


def _cmp_kernel(q_ref, k_ref, v_ref, *rest, nkt, n_top):
    bias_refs = rest[:nkt]
    ov_ref, oc_ref, sel_ref, imp_sc = rest[nkt:]
    qi = pl.program_id(1)
    tq = q_ref.shape[1]
    k = k_ref[0, 0]
    v = v_ref[0, 0]
    imp_sc[...] = jnp.zeros(imp_sc.shape, F32)
    for h in range(C_HPG):
        if nkt == 1:
            bias = bias_refs[0][h, 0]
        else:
            bias = jnp.concatenate([b[h, 0] for b in bias_refs], axis=-1)
        s = _dot_nt(q_ref[h], k) + bias
        mc = (bias > 0.5 * NEG).astype(F32)
        m = jnp.max(s, axis=-1, keepdims=True)
        e = jnp.exp(s - m)
        pc = (e / jnp.sum(e, axis=-1, keepdims=True)) * mc
        oc_ref[:, h * HEAD_DIM:(h + 1) * HEAD_DIM] = _dot(pc.astype(BF16), v)
        imp_sc[...] += pc
    imp = imp_sc[...]
    hi = imp.astype(BF16)
    lo = (imp - hi.astype(F32)).astype(BF16)
    ov = ov_ref[...]
    impn = _dot(hi, ov) + _dot(lo, ov)
    shape = impn.shape
    t = qi * tq + lax.broadcasted_iota(jnp.int32, shape, 0)
    n = lax.broadcasted_iota(jnp.int32, shape, 1)
    cur = lax.shift_right_logical(t, int(math.log2(SEL_LEN)))
    forced = (n == 0) | (n == cur) | (n == cur - 1)
    valid = n <= cur
    score = jnp.where(valid, jnp.where(forced, FORCE, impn), NEG)
    nf = n.astype(F32)
    sel = jnp.zeros(shape, F32)
    for _ in range(n_top):
        m = jnp.max(score, axis=-1, keepdims=True)
        first = jnp.min(jnp.where(score == m, nf, 1e9), axis=-1, keepdims=True)
        pick = nf == first
        sel = jnp.where(pick, 1.0, sel)
        score = jnp.where(pick, -jnp.inf, score)
    sel_ref[0] = jnp.where(valid, sel, 0.0).astype(sel_ref.dtype)


def cmp_attention(q_hm, kvcmp, bias, overlap, tq, n_top):
    s = q_hm.shape[1]
    ncp = kvcmp.shape[2]
    nkt = ncp // LANES
    nselp = overlap.shape[1]
    nd = bias.shape[1]
    per_tile = (LANES * CMP_STRIDE) // tq

    def bias_spec(kt):
        return pl.BlockSpec(
            (C_HPG, 1, tq, LANES),
            lambda g, i: (g, jnp.clip(i - per_tile * kt, -1, nd - 2) + 1, 0, 0))

    w = C_HEADS * HEAD_DIM
    return pl.pallas_call(
        functools.partial(_cmp_kernel, nkt=nkt, n_top=n_top),
        out_shape=[jax.ShapeDtypeStruct((s, w), F32),
                   jax.ShapeDtypeStruct((C_GROUPS, s, nselp), BF16)],
        grid=(C_GROUPS, s // tq),
        in_specs=[pl.BlockSpec((C_HPG, tq, LANES), lambda g, i: (g, i, 0)),
                  pl.BlockSpec((1, 1, ncp, LANES), lambda g, i: (0, g, 0, 0)),
                  pl.BlockSpec((1, 1, ncp, LANES), lambda g, i: (1, g, 0, 0))]
                 + [bias_spec(kt) for kt in range(nkt)]
                 + [pl.BlockSpec((ncp, nselp), lambda g, i: (0, 0))],
        out_specs=[pl.BlockSpec((tq, w // C_GROUPS), lambda g, i: (i, g)),
                   pl.BlockSpec((1, tq, nselp), lambda g, i: (g, i, 0))],
        scratch_shapes=[pltpu.VMEM((tq, ncp), F32)],
        name="cmp_attention",
        compiler_params=_cparams(("parallel", "parallel")),
    )(q_hm, kvcmp, kvcmp, *([bias] * nkt), overlap)


def _gate_kernel(oc_ref, os_ref, ow_ref, gt_ref, o_ref):
    gt = jax.nn.sigmoid(gt_ref[0].astype(F32))
    for h in range(C_HEADS):
        sl = slice(h * HEAD_DIM, (h + 1) * HEAD_DIM)
        g0 = gt[:, h:h + 1]
        g1 = gt[:, C_HEADS + h:C_HEADS + h + 1]
        g2 = gt[:, 2 * C_HEADS + h:2 * C_HEADS + h + 1]
        o_ref[:, sl] = (g0 * oc_ref[:, sl] + g1 * os_ref[:, sl] + g2 * ow_ref[:, sl]).astype(o_ref.dtype)


def gate_merge(oc, osel, ow, proj_hm, gate_blk):
    s, w = oc.shape
    ts = min(256, s)
    spec = pl.BlockSpec((ts, w), lambda i: (i, 0))
    return pl.pallas_call(
        _gate_kernel,
        out_shape=jax.ShapeDtypeStruct((s, w), BF16),
        grid=(s // ts,),
        in_specs=[spec, spec, spec, pl.BlockSpec((1, ts, LANES), lambda i: (gate_blk, i, 0))],
        out_specs=spec,
        name="gate_merge",
        compiler_params=_cparams(("parallel",)),
    )(oc, osel, ow, proj_hm)


def _tile_gain(g, reps, scale=1.0):
    return jnp.tile(g.astype(F32) * scale, reps)


def even_mixer(x, h, w_in, a_qk_gain, a_lambda, a_subln_gain, b_qk_gain, w_out, rel_table, lam_init):
    s = h.shape[0]
    scale = HEAD_DIM ** -0.5
    na = 2 * A_HEADS
    ones = lambda n: jnp.ones((n * HEAD_DIM,), F32)
    zeros = lambda n: jnp.zeros((n * HEAD_DIM,), F32)
    gain = jnp.concatenate([_tile_gain(a_qk_gain[0], na, scale), _tile_gain(a_qk_gain[1], na), ones(na),
                            _tile_gain(b_qk_gain[0], B_HEADS, scale), _tile_gain(b_qk_gain[1], B_HEADS),
                            ones(B_HEADS)])
    flag = jnp.concatenate([ones(2 * na), zeros(na), ones(2 * B_HEADS), zeros(B_HEADS)])
    proj = proj_headmajor(h, w_in.astype(BF16), gain, flag, tn=512)
    table_t = rel_table.astype(F32).T

    t = min(512, s)
    nda = min(-(-(FAR - 1) // t) + 2, s // t)
    dist = _toeplitz_dist(nda, t)
    bias_a = _bias_tiles(table_t[:A_HEADS], dist, dist >= 0)
    oa = flash_attention(
        proj, proj, proj, bias_a, name="diff_attention", t=t, n_groups=na, hb=1, nv=2,
        q_map=lambda g, qb: (g, qb, 0),
        k_map=lambda g, kb: (na + g, kb, 0),
        v_map=lambda g, kb: (na + g % A_HEADS, kb, 0),
        b_map=lambda g, bt: (g % A_HEADS, bt, 0, 0))
    ao = diff_post(oa, a_lambda, a_subln_gain, lam_init)

    hb0 = 3 * na // B_HEADS
    os, lses = [], []
    for window, dilation in B_CONFIGS:
        span = window // dilation
        qi = np.arange(span)[:, None]
        ki = np.arange(2 * span)[None, :] - span
        j = qi - ki
        ok = (j >= 0) & (j <= span)
        dist_b = np.stack([j * dilation, j * dilation])
        valid_b = np.stack([ok & (ki >= 0), ok])
        bias_b = _bias_tiles(table_t[A_HEADS:A_HEADS + B_HEADS], dist_b, valid_b)
        bias_b = bias_b.transpose(1, 0, 2, 3)
        o, lse = dilated_group(proj, bias_b, dilation, span, hb0, hb0 + 1, hb0 + 2)
        os.append(o)
        lses.append(lse)
    bo = mix_dilated(os, lses)
    return matmul2_residual(ao, bo, w_out.astype(BF16), x, tm=1024, tn=512)


def nsa_mixer(x, h, w_in, c_qk_gain, c_cmp_pe, c_cmp_w1, c_cmp_w2, w_out, rel_table):
    s, d = h.shape
    scale = HEAD_DIM ** -0.5
    g = C_GROUPS
    n_in = w_in.shape[1]
    n_pad = -(-n_in // 640) * 640
    ones = lambda n: jnp.ones((n,), F32)
    zeros = lambda n: jnp.zeros((n,), F32)
    kvw = g * HEAD_DIM
    gain = jnp.concatenate([_tile_gain(c_qk_gain[0], C_HEADS, scale), ones(2 * kvw),
                            _tile_gain(c_qk_gain[2], g), ones(kvw), _tile_gain(c_qk_gain[3], g), ones(kvw),
                            ones(n_pad - C_HEADS * HEAD_DIM - 6 * kvw)])
    flag = jnp.concatenate([ones(C_HEADS * HEAD_DIM), zeros(2 * kvw), ones(kvw), zeros(kvw), ones(kvw),
                            zeros(kvw), zeros(n_pad - C_HEADS * HEAD_DIM - 6 * kvw)])
    w_pad = jnp.pad(w_in.astype(BF16), ((0, 0), (0, n_pad - n_in)))
    proj = proj_headmajor(h, w_pad, gain, flag, tn=640)
    kv0 = C_HEADS
    table_t = rel_table.astype(F32).T[:C_HEADS]

    nc = s // CMP_STRIDE
    cw = CMP_STRIDE * HEAD_DIM
    xc = proj[kv0:kv0 + 2 * g].reshape(2, g, nc, cw)
    pe = c_cmp_pe.astype(F32).reshape(2, 2, 1, cw)
    w1 = c_cmp_w1.astype(BF16).reshape(2, 2, cw, c_cmp_w1.shape[-1])
    kvcmp = compress_kv(xc, pe, w1, c_cmp_w2.astype(BF16), c_qk_gain[1])
    if nc % LANES:
        kvcmp = jnp.pad(kvcmp, ((0, 0), (0, 0), (0, LANES - nc % LANES), (0, 0)))
    ncp = kvcmp.shape[2]

    tq = min(256, s)
    n_cmp = (s - CMP_LEN) // CMP_STRIDE + 1
    n_sel = s // SEL_LEN
    nselp = -(-n_sel // LANES) * LANES
    n_top = min(SEL_TOPK, n_sel)
    far_tile = -(-(FAR + CMP_STRIDE * (LANES - 1) + CMP_LEN - 1) // tq)
    deltas = np.arange(-1, far_tile + 1)
    i = np.arange(tq)[None, :, None]
    jj = np.arange(LANES)[None, None, :]
    dist_c = deltas[:, None, None] * tq + i - CMP_STRIDE * jj - (CMP_LEN - 1)
    bias_c = _bias_tiles(table_t, dist_c, dist_c >= 0)
    c_start = np.arange(ncp) * CMP_STRIDE
    s_start = np.arange(nselp) * SEL_LEN
    ov = ((c_start[:, None] < s_start[None, :] + SEL_LEN) & (c_start[:, None] + CMP_LEN > s_start[None, :])
          & (np.arange(ncp)[:, None] < n_cmp) & (np.arange(nselp)[None, :] < n_sel))
    oc, sel = cmp_attention(proj, kvcmp, bias_c, jnp.asarray(ov.astype(np.float32), BF16), tq, n_top)

    t = tq
    nds = min(-(-(FAR - 1) // t) + 2, s // t)
    dist = _toeplitz_dist(nds, t)
    bias_s = _bias_tiles(table_t, dist, dist >= 0)
    key_blk = np.arange(s) // SEL_LEN
    expand = (np.arange(nselp)[None, :, None] == key_blk.reshape(s // t, 1, t)).astype(np.float32)
    osel = flash_attention(
        proj, proj, proj, bias_s, name="selected_attention", t=t, n_groups=g, hb=C_HPG, nv=1,
        q_map=lambda gg, qb: (gg, qb, 0),
        k_map=lambda gg, kb: (kv0 + 2 * g + gg, kb, 0),
        v_map=lambda gg, kb: (kv0 + 3 * g + gg, kb, 0),
        b_map=lambda gg, bt: (gg, bt, 0, 0),
        sel=sel, expand=jnp.asarray(expand, BF16))

    look = -(-(C_WINDOW - 1) // t)
    ndw = min(look + 1, s // t)
    dist_w = _toeplitz_dist(ndw, t)
    bias_w = _bias_tiles(table_t, dist_w, (dist_w >= 0) & (dist_w < C_WINDOW))
    ow = flash_attention(
        proj, proj, proj, bias_w, name="window_attention", t=t, n_groups=g, hb=C_HPG, nv=1, lookback=look,
        q_map=lambda gg, qb: (gg, qb, 0),
        k_map=lambda gg, kb: (kv0 + 4 * g + gg, kb, 0),
        v_map=lambda gg, kb: (kv0 + 5 * g + gg, kb, 0),
        b_map=lambda gg, bt: (gg, bt, 0, 0))

    o = gate_merge(oc, osel, ow, proj, kv0 + 6 * g)
    return matmul_residual(o, w_out.astype(BF16), x, tm=1024, tn=512)


def conv_ffn(x, h, w_gate, w_up, conv_w, conv_b, w_down):
    act = ffn_gate_up(h, w_gate.astype(BF16), w_up.astype(BF16), conv_w, conv_b)
    return matmul_residual(act, w_down.astype(BF16), x, tm=1024, tn=256)


def kernel(x, rel_table, ev_norm, ev_w_in, a_qk_gain, a_lambda, a_subln_gain, b_qk_gain, ev_w_out,
           od_norm, od_w_in, c_qk_gain, c_cmp_pe, c_cmp_w1, c_cmp_w2, od_w_out,
           ffn_norm, ffn_w_gate, ffn_w_up, ffn_conv_w, ffn_conv_b, ffn_w_down):
    b, s, d = x.shape
    depth = ffn_norm.shape[0]
    outs = []
    for bi in range(b):
        y = x[bi].astype(F32)
        for i in range(depth):
            if i % 2 == 0:
                e = i // 2
                lam_init = 0.8 - 0.6 * math.exp(-0.3 * i)
                y = even_mixer(y, rmsnorm(y, ev_norm[e]), ev_w_in[e], a_qk_gain[e], a_lambda[e],
                               a_subln_gain[e], b_qk_gain[e], ev_w_out[e], rel_table, lam_init)
            else:
                o = i // 2
                y = nsa_mixer(y, rmsnorm(y, od_norm[o]), od_w_in[o], c_qk_gain[o], c_cmp_pe[o],
                              c_cmp_w1[o], c_cmp_w2[o], od_w_out[o], rel_table)
            y = conv_ffn(y, rmsnorm(y, ffn_norm[i]), ffn_w_gate[i], ffn_w_up[i], ffn_conv_w[i],
                         ffn_conv_b[i], ffn_w_down[i])
        outs.append(y)
    return jnp.stack(outs).astype(x.dtype)
```

```python
import functools
import math

import numpy as np
import jax
import jax.numpy as jnp
from jax import lax
from jax.experimental import pallas as pl
from jax.experimental.pallas import tpu as pltpu

HEAD_DIM = 128
A_HEADS = 8
B_HEADS = 16
B_CONFIGS = ((128, 1), (512, 4), (2048, 16))
C_HEADS = 32
C_GROUPS = 2
C_HPG = C_HEADS // C_GROUPS
CMP_LEN = 32
CMP_STRIDE = 16
SEL_LEN = 64
SEL_TOPK = 16
C_WINDOW = 512
REL_BUCKETS = 32
REL_MAX_DIST = 2048
EPS = 1e-6
NEG = -1e30
FORCE = 1e9

LANES = 128
VMEM_LIMIT = 56 * 1024 * 1024

F32 = jnp.float32
BF16 = jnp.bfloat16


def _cparams(sem):
    return pltpu.CompilerParams(dimension_semantics=sem, vmem_limit_bytes=VMEM_LIMIT)


def _dot(a, b):
    return jnp.dot(a, b, preferred_element_type=F32)


def _dot_nt(a, b):
    return lax.dot_general(a, b, (((1,), (1,)), ((), ())), preferred_element_type=F32)


def _bucket_np(dist):
    n = np.maximum(dist, 0).astype(np.int32)
    max_exact = REL_BUCKETS // 2
    nf = np.maximum(n, 1).astype(np.float32)
    ratio = np.log(nf / np.float32(max_exact)) / np.float32(math.log(REL_MAX_DIST / max_exact))
    large = np.minimum(max_exact + (ratio * np.float32(REL_BUCKETS - max_exact)).astype(np.int32),
                       REL_BUCKETS - 1)
    return np.where(n < max_exact, n, large).astype(np.int32)


def _far_distance():
    b = _bucket_np(np.arange(4 * REL_MAX_DIST))
    return int(np.max(np.nonzero(b != REL_BUCKETS - 1)[0])) + 1


FAR = _far_distance()


def _bias_by_distance(table_t, dist, valid):
    b = jnp.take(table_t, jnp.asarray(_bucket_np(dist)), axis=1)
    return jnp.where(jnp.asarray(valid), b, NEG).astype(F32)


def _toeplitz_kernel(g_ref, o_ref, *, rows, width, shift, stride, transpose):
    g = g_ref[0, 0]
    x = jnp.broadcast_to(g, (rows, g.shape[1]))
    y = pltpu.roll(x, shift, 1, stride=stride, stride_axis=0)[:, :width]
    o_ref[0, 0] = y.T if transpose else y


def toeplitz_tiles(g, rows, width, stride, offset, transpose=False):
    h, nd, wx = g.shape
    assert wx % LANES == 0 and 0 <= offset - stride * (rows - 1) and width - 1 + offset < wx
    shape = (width, rows) if transpose else (rows, width)
    return pl.pallas_call(
        functools.partial(_toeplitz_kernel, rows=rows, width=width, shift=(wx - offset) % wx, stride=stride,
                          transpose=transpose),
        out_shape=jax.ShapeDtypeStruct((h, nd) + shape, F32),
        grid=(h, nd),
        in_specs=[pl.BlockSpec((1, 1, 1, wx), lambda a, b: (a, b, 0, 0))],
        out_specs=pl.BlockSpec((1, 1) + shape, lambda a, b: (a, b, 0, 0)),
        name="toeplitz_tiles",
        compiler_params=_cparams(("parallel", "parallel")),
    )(g.reshape(h, nd, 1, wx))


def causal_bias_tiles(table_t, t, n_tiles, window=None):
    span = (n_tiles + 1) * t
    dist = (n_tiles * t) - np.arange(span)
    valid = dist >= 0 if window is None else (dist >= 0) & (dist < window)
    vrev = _bias_by_distance(table_t, dist, valid)
    g = jnp.stack([vrev[:, (n_tiles - d - 1) * t:(n_tiles - d + 1) * t] for d in range(n_tiles)], axis=1)
    return toeplitz_tiles(g, t, t, 1, t)


def _rmsnorm_kernel(x_ref, g_ref, o_ref):
    x = x_ref[...]
    ms = jnp.mean(x * x, axis=-1, keepdims=True)
    o_ref[...] = ((x * lax.rsqrt(ms + EPS)) * g_ref[...]).astype(o_ref.dtype)


def rmsnorm(x, gain):
    s, d = x.shape
    tr = min(256, s)
    return pl.pallas_call(
        _rmsnorm_kernel,
        out_shape=jax.ShapeDtypeStruct((s, d), BF16),
        grid=(s // tr,),
        in_specs=[pl.BlockSpec((tr, d), lambda i: (i, 0)), pl.BlockSpec((1, d), lambda i: (0, 0))],
        out_specs=pl.BlockSpec((tr, d), lambda i: (i, 0)),
        name="rmsnorm",
        compiler_params=_cparams(("parallel",)),
    )(x, gain.reshape(1, d).astype(F32))


def _proj_kernel(x_ref, w_ref, gain_ref, flag_ref, o_ref, *, nblk):
    acc = _dot(x_ref[...], w_ref[...])
    for c in range(nblk):
        sl = slice(c * LANES, (c + 1) * LANES)
        blk = acc[:, sl]
        ms = jnp.mean(blk * blk, axis=-1, keepdims=True)
        r = jnp.where(flag_ref[:, sl] > 0, lax.rsqrt(ms + EPS), 1.0)
        o_ref[c] = ((blk * r) * gain_ref[:, sl]).astype(o_ref.dtype)


def proj_headmajor(x, w, gain, flag, tn):
    s, k = x.shape
    n = w.shape[1]
    tm = min(1024, s)
    nblk = tn // LANES
    return pl.pallas_call(
        functools.partial(_proj_kernel, nblk=nblk),
        out_shape=jax.ShapeDtypeStruct((n // LANES, s, LANES), BF16),
        grid=(s // tm, n // tn),
        in_specs=[pl.BlockSpec((tm, k), lambda i, j: (i, 0)),
                  pl.BlockSpec((k, tn), lambda i, j: (0, j)),
                  pl.BlockSpec((1, tn), lambda i, j: (0, j)),
                  pl.BlockSpec((1, tn), lambda i, j: (0, j))],
        out_specs=pl.BlockSpec((nblk, tm, LANES), lambda i, j: (j, i, 0)),
        name="proj_headmajor",
        compiler_params=_cparams(("parallel", "parallel")),
    )(x, w, gain.reshape(1, n).astype(F32), flag.reshape(1, n).astype(F32))


def _mm_res_kernel(x_ref, w_ref, r_ref, o_ref):
    o_ref[...] = r_ref[...] + _dot(x_ref[...], w_ref[...])


def matmul_residual(x, w, res, tm, tn):
    s, k = x.shape
    n = w.shape[1]
    tm = min(tm, s)
    tn = min(tn, n)
    return pl.pallas_call(
        _mm_res_kernel,
        out_shape=jax.ShapeDtypeStruct((s, n), F32),
        grid=(s // tm, n // tn),
        in_specs=[pl.BlockSpec((tm, k), lambda i, j: (i, 0), pipeline_mode=pl.Buffered(1)),
                  pl.BlockSpec((k, tn), lambda i, j: (0, j)),
                  pl.BlockSpec((tm, tn), lambda i, j: (i, j))],
        out_specs=pl.BlockSpec((tm, tn), lambda i, j: (i, j)),
        name="matmul_residual",
        compiler_params=_cparams(("parallel", "arbitrary")),
    )(x, w, res)


def _mm2_res_kernel(x1_ref, x2_ref, w_ref, r_ref, o_ref, *, k1):
    acc = _dot(x1_ref[...], w_ref[0:k1, :]) + _dot(x2_ref[...], w_ref[k1:, :])
    o_ref[...] = r_ref[...] + acc


def matmul2_residual(x1, x2, w, res, tm, tn):
    s, k1 = x1.shape
    k2 = x2.shape[1]
    n = w.shape[1]
    tm = min(tm, s)
    tn = min(tn, n)
    return pl.pallas_call(
        functools.partial(_mm2_res_kernel, k1=k1),
        out_shape=jax.ShapeDtypeStruct((s, n), F32),
        grid=(s // tm, n // tn),
        in_specs=[pl.BlockSpec((tm, k1), lambda i, j: (i, 0)),
                  pl.BlockSpec((tm, k2), lambda i, j: (i, 0)),
                  pl.BlockSpec((k1 + k2, tn), lambda i, j: (0, j)),
                  pl.BlockSpec((tm, tn), lambda i, j: (i, j))],
        out_specs=pl.BlockSpec((tm, tn), lambda i, j: (i, j)),
        name="matmul2_residual",
        compiler_params=_cparams(("parallel", "arbitrary")),
    )(x1, x2, w, res)


def _ffn1_kernel(x_ref, wg_ref, wu_ref, cw_ref, cb_ref, o_ref, carry_ref):
    i = pl.program_id(0)
    j = pl.program_id(1)
    x = x_ref[...]
    g = _dot(x, wg_ref[...])
    u = _dot(x, wu_ref[...])
    tm = g.shape[0]

    @pl.when(i == 0)
    def _():
        carry_ref[j] = jnp.zeros(carry_ref.shape[1:], F32)

    prev = carry_ref[j]
    carry_ref[j] = g[tm - 8:tm, :]
    row = lax.broadcasted_iota(jnp.int32, g.shape, 0)
    p7 = prev[7:8, :]
    p6 = prev[6:7, :]
    g1 = jnp.where(row == 0, p7, pltpu.roll(g, 1, 0))
    g2 = jnp.where(row == 0, p6, jnp.where(row == 1, p7, pltpu.roll(g, 2, 0)))
    cw = cw_ref[...]
    gc = g2 * cw[0:1, :] + g1 * cw[1:2, :] + g * cw[2:3, :] + cb_ref[...]
    o_ref[...] = ((gc * jax.nn.sigmoid(gc)) * u).astype(o_ref.dtype)


def ffn_gate_up(h, wg, wu, conv_w, conv_b):
    s, d = h.shape
    f = wg.shape[1]
    tm = min(1024, s)
    tn = 256
    nj = f // tn
    return pl.pallas_call(
        _ffn1_kernel,
        out_shape=jax.ShapeDtypeStruct((s, f), BF16),
        grid=(s // tm, nj),
        in_specs=[pl.BlockSpec((tm, d), lambda i, j: (i, 0)),
                  pl.BlockSpec((d, tn), lambda i, j: (0, j)),
                  pl.BlockSpec((d, tn), lambda i, j: (0, j)),
                  pl.BlockSpec((3, tn), lambda i, j: (0, j)),
                  pl.BlockSpec((1, tn), lambda i, j: (0, j))],
        out_specs=pl.BlockSpec((tm, tn), lambda i, j: (i, j)),
        scratch_shapes=[pltpu.VMEM((nj, 8, tn), F32)],
        name="ffn_gate_up",
        compiler_params=_cparams(("arbitrary", "arbitrary")),
    )(h, wg, wu, conv_w.astype(F32), conv_b.reshape(1, f).astype(F32))


def _flash_kernel(qi_ref, kj_ref, bi_ref, fl_ref, *refs, hb, nv, use_sel, rc):
    if use_sel:
        (q_ref, k_ref, v_ref, b_ref, sel_ref, en_ref, o_ref,
         m_sc, acc_sc, s_sc, p_sc, al_sc, qa_sc) = refs
    else:
        q_ref, k_ref, v_ref, b_ref, o_ref, m_sc, acc_sc, s_sc, p_sc, al_sc = refs
    p = pl.program_id(1)
    flags = fl_ref[p]
    dv = nv * LANES
    t = q_ref.shape[1]
    tk = k_ref.shape[1]

    @pl.when((flags & 1) != 0)
    def _():
        m_sc[...] = jnp.full(m_sc.shape, -jnp.inf, F32)
        acc_sc[...] = jnp.zeros(acc_sc.shape, F32)
        if use_sel:
            notsel = (1.0 - sel_ref[0].astype(F32)).astype(BF16)
            for h in range(hb):
                qa_sc[h, :, 0:LANES] = q_ref[h]
                qa_sc[h, :, LANES:] = notsel

    k = k_ref[0]
    if use_sel:
        k = jnp.concatenate([k, en_ref[0]], axis=-1)
    v_aug = jnp.concatenate([v_ref[c] for c in range(nv)] + [jnp.ones((tk, LANES), BF16)], axis=-1)

    for h in range(hb):
        qh = qa_sc[h] if use_sel else q_ref[h]
        s_sc[h] = _dot_nt(qh, k)
        for r0 in range(0, t, rc):
            rows = slice(r0, r0 + rc)
            s = s_sc[h, rows, :] + b_ref[h, 0, rows, :]
            m_old = m_sc[h, rows, :]
            m_new = jnp.maximum(m_old, jnp.max(s, axis=-1, keepdims=True))
            al_sc[h, rows, :] = jnp.exp(m_old - m_new)
            m_sc[h, rows, :] = m_new
            p_sc[h, rows, :] = jnp.exp(s - jnp.tile(m_new, (1, tk // LANES))).astype(BF16)
        acc_sc[h] = jnp.tile(al_sc[h], (1, nv + 1)) * acc_sc[h] + _dot(p_sc[h], v_aug)

    @pl.when((flags & 2) != 0)
    def _():
        for h in range(hb):
            acc = acc_sc[h]
            o_ref[:, h * dv:(h + 1) * dv] = acc[:, 0:dv] / jnp.tile(acc[:, dv:], (1, nv))


def _pairs(nq, lookback, max_bias):
    qi, kj, bi, fl = [], [], [], []
    for q in range(nq):
        lo = 0 if lookback is None else max(0, q - lookback)
        for k in range(lo, q + 1):
            qi.append(q)
            kj.append(k)
            bi.append(min(q - k, max_bias))
            fl.append((1 if k == lo else 0) | (2 if k == q else 0))
    return [jnp.asarray(np.asarray(a, np.int32)) for a in (qi, kj, bi, fl)]


def flash_attention(q, k, v, bias, *, name, t, n_groups, hb, nv, q_map, k_map, v_map, b_map,
                    lookback=None, sel=None, key_neg=None):
    s = q.shape[1]
    nq = s // t
    qi, kj, bi, fl = _pairs(nq, lookback, bias.shape[1] - 1)
    n_pairs = int(qi.shape[0])
    dv = nv * LANES
    use_sel = sel is not None
    in_specs = [
        pl.BlockSpec((hb, t, LANES), lambda g, p, qi, kj, bi, fl: q_map(g, qi[p])),
        pl.BlockSpec((1, t, LANES), lambda g, p, qi, kj, bi, fl: k_map(g, kj[p])),
        pl.BlockSpec((nv, t, LANES), lambda g, p, qi, kj, bi, fl: v_map(g, kj[p])),
        pl.BlockSpec((hb, 1, t, t), lambda g, p, qi, kj, bi, fl: b_map(g, bi[p])),
    ]
    args = [q, k, v, bias]
    scratch = [pltpu.VMEM((hb, t, LANES), F32), pltpu.VMEM((hb, t, dv + LANES), F32),
               pltpu.VMEM((hb, t, t), F32), pltpu.VMEM((hb, t, t), BF16), pltpu.VMEM((hb, t, LANES), F32)]
    if use_sel:
        nselp = sel.shape[2]
        in_specs += [
            pl.BlockSpec((1, t, nselp), lambda g, p, qi, kj, bi, fl: (g, qi[p], 0)),
            pl.BlockSpec((1, t, nselp), lambda g, p, qi, kj, bi, fl: (kj[p], 0, 0)),
        ]
        args += [sel, key_neg]
        scratch.append(pltpu.VMEM((hb, t, LANES + nselp), BF16))
    grid_spec = pltpu.PrefetchScalarGridSpec(
        num_scalar_prefetch=4,
        grid=(n_groups, n_pairs),
        in_specs=in_specs,
        out_specs=pl.BlockSpec((t, hb * dv), lambda g, p, qi, kj, bi, fl: (qi[p], g)),
        scratch_shapes=scratch,
    )
    return pl.pallas_call(
        functools.partial(_flash_kernel, hb=hb, nv=nv, use_sel=use_sel, rc=min(64, t)),
        out_shape=jax.ShapeDtypeStruct((s, n_groups * hb * dv), F32),
        grid_spec=grid_spec,
        name=name,
        compiler_params=_cparams(("parallel", "arbitrary")),
    )(qi, kj, bi, fl, *args)


def _apost_kernel(o_ref, lam_ref, g_ref, out_ref, *, lam_init):
    lf = lam_ref[...]
    s1 = jnp.sum(lf[0:1, :] * lf[1:2, :], axis=-1, keepdims=True)
    s2 = jnp.sum(lf[2:3, :] * lf[3:4, :], axis=-1, keepdims=True)
    lam = jnp.exp(s1) - jnp.exp(s2) + lam_init
    dv = 2 * HEAD_DIM
    for h in range(A_HEADS):
        o = o_ref[:, h * dv:(h + 1) * dv] - lam * o_ref[:, (A_HEADS + h) * dv:(A_HEADS + h + 1) * dv]
        ms = jnp.mean(o * o, axis=-1, keepdims=True)
        y = ((o * lax.rsqrt(ms + EPS)) * g_ref[...]) * (1.0 - lam_init)
        out_ref[:, h * dv:(h + 1) * dv] = y.astype(out_ref.dtype)


def diff_post(o, a_lambda, subln_gain, lam_init):
    s, w = o.shape
    ts = min(256, s)
    dv = 2 * HEAD_DIM
    return pl.pallas_call(
        functools.partial(_apost_kernel, lam_init=lam_init),
        out_shape=jax.ShapeDtypeStruct((s, w // 2), BF16),
        grid=(s // ts,),
        in_specs=[pl.BlockSpec((ts, w), lambda i: (i, 0)),
                  pl.BlockSpec((4, HEAD_DIM), lambda i: (0, 0)),
                  pl.BlockSpec((1, dv), lambda i: (0, 0))],
        out_specs=pl.BlockSpec((ts, w // 2), lambda i: (i, 0)),
        name="diff_post",
        compiler_params=_cparams(("parallel",)),
    )(o, a_lambda.astype(F32), subln_gain.reshape(1, dv).astype(F32))


def _dilated_kernel(q_ref, kp_ref, kc_ref, vp_ref, vc_ref, b_ref, o_ref, lse_ref):
    span = q_ref.shape[1]
    no_prev = jnp.where(pl.program_id(1) == 0, NEG, 0.0)
    for h in range(B_HEADS):
        q = q_ref[h]
        sp = _dot_nt(q, kp_ref[h]) + (b_ref[h, 0, :, 0:span] + no_prev)
        sc = _dot_nt(q, kc_ref[h]) + b_ref[h, 0, :, span:2 * span]
        m = jnp.maximum(jnp.max(sp, axis=-1, keepdims=True), jnp.max(sc, axis=-1, keepdims=True))
        ep = jnp.exp(sp - m)
        ec = jnp.exp(sc - m)
        l = jnp.sum(ep, axis=-1, keepdims=True) + jnp.sum(ec, axis=-1, keepdims=True)
        o = (_dot(ep.astype(BF16), vp_ref[h]) + _dot(ec.astype(BF16), vc_ref[h])) / l
        sl = slice(h * HEAD_DIM, (h + 1) * HEAD_DIM)
        o_ref[:, sl] = o
        lse_ref[:, sl] = jnp.broadcast_to(m + jnp.log(l), o.shape)


def dilated_group(proj_hm, bias, dilation, span, q_blk, k_blk, v_blk):
    nh, s, _ = proj_hm.shape
    l = s // dilation
    nb = l // span
    x = proj_hm.reshape(nh, l, dilation * LANES)
    w = B_HEADS * HEAD_DIM
    hspec = lambda blk, prev: pl.BlockSpec(
        (B_HEADS, span, LANES),
        (lambda r, n: (blk, jnp.maximum(n - 1, 0), r)) if prev else (lambda r, n: (blk, n, r)))
    o, lse = pl.pallas_call(
        _dilated_kernel,
        out_shape=[jax.ShapeDtypeStruct((l, dilation * w), F32)] * 2,
        grid=(dilation, nb),
        in_specs=[hspec(q_blk, False), hspec(k_blk, True), hspec(k_blk, False),
                  hspec(v_blk, True), hspec(v_blk, False),
                  pl.BlockSpec((B_HEADS, 1, span, 2 * span), lambda r, n: (0, 0, 0, 0))],
        out_specs=[pl.BlockSpec((span, w), lambda r, n: (n, r))] * 2,
        name="dilated_group",
        compiler_params=_cparams(("parallel", "parallel")),
    )(x, x, x, x, x, bias)
    return o.reshape(s, w), lse.reshape(s, w)


def _mix3_kernel(o1, o2, o3, l1, l2, l3, out_ref):
    a, b, c = l1[...], l2[...], l3[...]
    m = jnp.maximum(jnp.maximum(a, b), c)
    ea, eb, ec = jnp.exp(a - m), jnp.exp(b - m), jnp.exp(c - m)
    out_ref[...] = ((ea * o1[...] + eb * o2[...] + ec * o3[...]) / (ea + eb + ec)).astype(out_ref.dtype)


def mix_dilated(os, lses):
    s, w = os[0].shape
    ts = min(256, s)
    spec = pl.BlockSpec((ts, w), lambda i: (i, 0))
    return pl.pallas_call(
        _mix3_kernel,
        out_shape=jax.ShapeDtypeStruct((s, w), BF16),
        grid=(s // ts,),
        in_specs=[spec] * 6,
        out_specs=spec,
        name="mix_dilated",
        compiler_params=_cparams(("parallel",)),
    )(*os, *lses)


def _compress_kernel(x_ref, pe_ref, w1_ref, w2_ref, gain_ref, o_ref):
    kv = pl.program_id(0)
    x = x_ref[0, 0].astype(F32)
    nc = x.shape[0]
    a = _dot((x + pe_ref[0, 0]).astype(BF16), w1_ref[0, 0])
    b = _dot((x + pe_ref[0, 1]).astype(BF16), w1_ref[0, 1])
    hid = jax.nn.gelu(a + pltpu.roll(b, nc - 1, 0))
    c = _dot(hid.astype(BF16), w2_ref[0])
    ms = jnp.mean(c * c, axis=-1, keepdims=True)
    normed = (c * lax.rsqrt(ms + EPS)) * gain_ref[...]
    o_ref[0, 0] = jnp.where(kv == 0, normed, c).astype(o_ref.dtype)


def compress_kv(xc, pe, w1, w2, gain):
    _, g, nc, cw = xc.shape
    hid = w1.shape[-1]
    return pl.pallas_call(
        _compress_kernel,
        out_shape=jax.ShapeDtypeStruct((2, g, nc, HEAD_DIM), BF16),
        grid=(2, g),
        in_specs=[pl.BlockSpec((1, 1, nc, cw), lambda a, b: (a, b, 0, 0)),
                  pl.BlockSpec((1, 2, 1, cw), lambda a, b: (a, 0, 0, 0)),
                  pl.BlockSpec((1, 2, cw, hid), lambda a, b: (a, 0, 0, 0)),
                  pl.BlockSpec((1, hid, HEAD_DIM), lambda a, b: (a, 0, 0)),
                  pl.BlockSpec((1, HEAD_DIM), lambda a, b: (0, 0))],
        out_specs=pl.BlockSpec((1, 1, nc, HEAD_DIM), lambda a, b: (a, b, 0, 0)),
        name="compress_kv",
        compiler_params=_cparams(("parallel", "parallel")),
    )(xc, pe, w1, w2, gain.reshape(1, HEAD_DIM).astype(F32))


def _cmp_kernel(q_ref, k_ref, v_ref, *rest, nkt, n_top):
    bias_refs = rest[:nkt]
    ov_ref, oc_ref, sel_ref, imp_sc = rest[nkt:]
    qi = pl.program_id(1)
    tq = q_ref.shape[1]
    k = k_ref[0, 0]
    v = v_ref[0, 0]
    imp_sc[...] = jnp.zeros(imp_sc.shape, F32)
    for h in range(C_HPG):
        if nkt == 1:
            bias = bias_refs[0][h, 0]
        else:
            bias = jnp.concatenate([b[h, 0] for b in bias_refs], axis=-1)
        s = _dot_nt(q_ref[h], k) + bias
        mc = (bias > 0.5 * NEG).astype(F32)
        m = jnp.max(s, axis=-1, keepdims=True)
        e = jnp.exp(s - m)
        pc = (e / jnp.sum(e, axis=-1, keepdims=True)) * mc
        oc_ref[:, h * HEAD_DIM:(h + 1) * HEAD_DIM] = _dot(pc.astype(BF16), v)
        imp_sc[...] += pc
    imp = imp_sc[...]
    hi = imp.astype(BF16)
    lo = (imp - hi.astype(F32)).astype(BF16)
    ov = ov_ref[...]
    impn = _dot(hi, ov) + _dot(lo, ov)
    shape = impn.shape
    t = qi * tq + lax.broadcasted_iota(jnp.int32, shape, 0)
    n = lax.broadcasted_iota(jnp.int32, shape, 1)
    cur = lax.shift_right_logical(t, int(math.log2(SEL_LEN)))
    forced = (n == 0) | (n == cur) | (n == cur - 1)
    valid = n <= cur
    score = jnp.where(valid, jnp.where(forced, FORCE, impn), NEG)
    nf = n.astype(F32)
    sel = jnp.zeros(shape, F32)
    for _ in range(n_top):
        m = jnp.max(score, axis=-1, keepdims=True)
        first = jnp.min(jnp.where(score == m, nf, 1e9), axis=-1, keepdims=True)
        pick = nf == first
        sel = jnp.where(pick, 1.0, sel)
        score = jnp.where(pick, -jnp.inf, score)
    sel_ref[0] = jnp.where(valid, sel, 0.0).astype(sel_ref.dtype)


def cmp_attention(q_hm, kvcmp, bias, overlap, tq, n_top):
    s = q_hm.shape[1]
    ncp = kvcmp.shape[2]
    nkt = ncp // LANES
    nselp = overlap.shape[1]
    nd = bias.shape[1]
    per_tile = (LANES * CMP_STRIDE) // tq

    def bias_spec(kt):
        return pl.BlockSpec(
            (C_HPG, 1, tq, LANES),
            lambda g, i: (g, jnp.clip(i - per_tile * kt, -1, nd - 2) + 1, 0, 0))

    w = C_HEADS * HEAD_DIM
    return pl.pallas_call(
        functools.partial(_cmp_kernel, nkt=nkt, n_top=n_top),
        out_shape=[jax.ShapeDtypeStruct((s, w), F32),
                   jax.ShapeDtypeStruct((C_GROUPS, s, nselp), BF16)],
        grid=(C_GROUPS, s // tq),
        in_specs=[pl.BlockSpec((C_HPG, tq, LANES), lambda g, i: (g, i, 0)),
                  pl.BlockSpec((1, 1, ncp, LANES), lambda g, i: (0, g, 0, 0)),
                  pl.BlockSpec((1, 1, ncp, LANES), lambda g, i: (1, g, 0, 0))]
                 + [bias_spec(kt) for kt in range(nkt)]
                 + [pl.BlockSpec((ncp, nselp), lambda g, i: (0, 0))],
        out_specs=[pl.BlockSpec((tq, w // C_GROUPS), lambda g, i: (i, g)),
                   pl.BlockSpec((1, tq, nselp), lambda g, i: (g, i, 0))],
        scratch_shapes=[pltpu.VMEM((tq, ncp), F32)],
        name="cmp_attention",
        compiler_params=_cparams(("parallel", "parallel")),
    )(q_hm, kvcmp, kvcmp, *([bias] * nkt), overlap)


def _gate_kernel(oc_ref, os_ref, ow_ref, gt_ref, o_ref):
    gt = jax.nn.sigmoid(gt_ref[0].astype(F32))
    for h in range(C_HEADS):
        sl = slice(h * HEAD_DIM, (h + 1) * HEAD_DIM)
        g0 = gt[:, h:h + 1]
        g1 = gt[:, C_HEADS + h:C_HEADS + h + 1]
        g2 = gt[:, 2 * C_HEADS + h:2 * C_HEADS + h + 1]
        o_ref[:, sl] = (g0 * oc_ref[:, sl] + g1 * os_ref[:, sl] + g2 * ow_ref[:, sl]).astype(o_ref.dtype)


def gate_merge(oc, osel, ow, proj_hm, gate_blk):
    s, w = oc.shape
    ts = min(256, s)
    spec = pl.BlockSpec((ts, w), lambda i: (i, 0))
    return pl.pallas_call(
        _gate_kernel,
        out_shape=jax.ShapeDtypeStruct((s, w), BF16),
        grid=(s // ts,),
        in_specs=[spec, spec, spec, pl.BlockSpec((1, ts, LANES), lambda i: (gate_blk, i, 0))],
        out_specs=spec,
        name="gate_merge",
        compiler_params=_cparams(("parallel",)),
    )(oc, osel, ow, proj_hm)


def _tile_gain(g, reps, scale=1.0):
    return jnp.tile(g.astype(F32) * scale, reps)


def even_mixer(x, h, w_in, a_qk_gain, a_lambda, a_subln_gain, b_qk_gain, w_out, rel_table, lam_init):
    s = h.shape[0]
    scale = HEAD_DIM ** -0.5
    na = 2 * A_HEADS
    ones = lambda n: jnp.ones((n * HEAD_DIM,), F32)
    zeros = lambda n: jnp.zeros((n * HEAD_DIM,), F32)
    gain = jnp.concatenate([_tile_gain(a_qk_gain[0], na, scale), _tile_gain(a_qk_gain[1], na), ones(na),
                            _tile_gain(b_qk_gain[0], B_HEADS, scale), _tile_gain(b_qk_gain[1], B_HEADS),
                            ones(B_HEADS)])
    flag = jnp.concatenate([ones(2 * na), zeros(na), ones(2 * B_HEADS), zeros(B_HEADS)])
    proj = proj_headmajor(h, w_in.astype(BF16), gain, flag, tn=512)
    table_t = rel_table.astype(F32).T

    t = min(512, s)
    nda = min(-(-(FAR - 1) // t) + 2, s // t)
    bias_a = causal_bias_tiles(table_t[:A_HEADS], t, nda)
    oa = flash_attention(
        proj, proj, proj, bias_a, name="diff_attention", t=t, n_groups=na, hb=1, nv=2,
        q_map=lambda g, qb: (g, qb, 0),
        k_map=lambda g, kb: (na + g, kb, 0),
        v_map=lambda g, kb: (na + g % A_HEADS, kb, 0),
        b_map=lambda g, bt: (g % A_HEADS, bt, 0, 0))
    ao = diff_post(oa, a_lambda, a_subln_gain, lam_init)

    hb0 = 3 * na // B_HEADS
    os, lses = [], []
    for window, dilation in B_CONFIGS:
        span = window // dilation
        j = 2 * span - np.arange(3 * span)
        g_b = _bias_by_distance(table_t[A_HEADS:A_HEADS + B_HEADS], j * dilation, (j >= 0) & (j <= span))
        bias_b = toeplitz_tiles(g_b[:, None, :], span, 2 * span, 1, span)
        o, lse = dilated_group(proj, bias_b, dilation, span, hb0, hb0 + 1, hb0 + 2)
        os.append(o)
        lses.append(lse)
    bo = mix_dilated(os, lses)
    return matmul2_residual(ao, bo, w_out.astype(BF16), x, tm=1024, tn=512)


def nsa_mixer(x, h, w_in, c_qk_gain, c_cmp_pe, c_cmp_w1, c_cmp_w2, w_out, rel_table):
    s, d = h.shape
    scale = HEAD_DIM ** -0.5
    g = C_GROUPS
    n_in = w_in.shape[1]
    n_pad = -(-n_in // 640) * 640
    ones = lambda n: jnp.ones((n,), F32)
    zeros = lambda n: jnp.zeros((n,), F32)
    kvw = g * HEAD_DIM
    gain = jnp.concatenate([_tile_gain(c_qk_gain[0], C_HEADS, scale), ones(2 * kvw),
                            _tile_gain(c_qk_gain[2], g), ones(kvw), _tile_gain(c_qk_gain[3], g), ones(kvw),
                            ones(n_pad - C_HEADS * HEAD_DIM - 6 * kvw)])
    flag = jnp.concatenate([ones(C_HEADS * HEAD_DIM), zeros(2 * kvw), ones(kvw), zeros(kvw), ones(kvw),
                            zeros(kvw), zeros(n_pad - C_HEADS * HEAD_DIM - 6 * kvw)])
    w_pad = jnp.pad(w_in.astype(BF16), ((0, 0), (0, n_pad - n_in)))
    proj = proj_headmajor(h, w_pad, gain, flag, tn=640)
    kv0 = C_HEADS
    table_t = rel_table.astype(F32).T[:C_HEADS]

    nc = s // CMP_STRIDE
    cw = CMP_STRIDE * HEAD_DIM
    xc = proj[kv0:kv0 + 2 * g].reshape(2, g, nc, cw)
    pe = c_cmp_pe.astype(F32).reshape(2, 2, 1, cw)
    w1 = c_cmp_w1.astype(BF16).reshape(2, 2, cw, c_cmp_w1.shape[-1])
    kvcmp = compress_kv(xc, pe, w1, c_cmp_w2.astype(BF16), c_qk_gain[1])
    if nc % LANES:
        kvcmp = jnp.pad(kvcmp, ((0, 0), (0, 0), (0, LANES - nc % LANES), (0, 0)))
    ncp = kvcmp.shape[2]

    tq = min(256, s)
    n_cmp = (s - CMP_LEN) // CMP_STRIDE + 1
    n_sel = s // SEL_LEN
    nselp = -(-n_sel // LANES) * LANES
    n_top = min(SEL_TOPK, n_sel)
    off = CMP_STRIDE * (LANES - 1)
    far_tile = -(-(FAR + off + CMP_LEN - 1) // tq)
    nd = far_tile + 2
    wx = -(-(tq + off) // LANES) * LANES
    dist_c = np.arange(tq * (nd - 1) + wx) - (tq + off + CMP_LEN - 1)
    vec_c = _bias_by_distance(table_t, dist_c, dist_c >= 0)
    g_c = jnp.stack([vec_c[:, tq * d:tq * d + wx] for d in range(nd)], axis=1)
    bias_c = toeplitz_tiles(g_c, LANES, tq, CMP_STRIDE, off, transpose=True)
    c_start = np.arange(ncp) * CMP_STRIDE
    s_start = np.arange(nselp) * SEL_LEN
    ov = ((c_start[:, None] < s_start[None, :] + SEL_LEN) & (c_start[:, None] + CMP_LEN > s_start[None, :])
          & (np.arange(ncp)[:, None] < n_cmp) & (np.arange(nselp)[None, :] < n_sel))
    oc, sel = cmp_attention(proj, kvcmp, bias_c, jnp.asarray(ov.astype(np.float32), BF16), tq, n_top)

    t = tq
    nds = min(-(-(FAR - 1) // t) + 2, s // t)
    bias_s = causal_bias_tiles(table_t, t, nds)
    key_blk = np.arange(s) // SEL_LEN
    key_neg = np.where(key_blk[:, None] == np.arange(nselp)[None, :], NEG, 0.0).reshape(s // t, t, nselp)
    osel = flash_attention(
        proj, proj, proj, bias_s, name="selected_attention", t=t, n_groups=g, hb=C_HPG, nv=1,
        q_map=lambda gg, qb: (gg, qb, 0),
        k_map=lambda gg, kb: (kv0 + 2 * g + gg, kb, 0),
        v_map=lambda gg, kb: (kv0 + 3 * g + gg, kb, 0),
        b_map=lambda gg, bt: (gg, bt, 0, 0),
        sel=sel, key_neg=jnp.asarray(key_neg.astype(np.float32), BF16))

    look = -(-(C_WINDOW - 1) // t)
    ndw = min(look + 1, s // t)
    bias_w = causal_bias_tiles(table_t, t, ndw, window=C_WINDOW)
    ow = flash_attention(
        proj, proj, proj, bias_w, name="window_attention", t=t, n_groups=g, hb=C_HPG, nv=1, lookback=look,
        q_map=lambda gg, qb: (gg, qb, 0),
        k_map=lambda gg, kb: (kv0 + 4 * g + gg, kb, 0),
        v_map=lambda gg, kb: (kv0 + 5 * g + gg, kb, 0),
        b_map=lambda gg, bt: (gg, bt, 0, 0))

    o = gate_merge(oc, osel, ow, proj, kv0 + 6 * g)
    return matmul_residual(o, w_out.astype(BF16), x, tm=1024, tn=512)


def conv_ffn(x, h, w_gate, w_up, conv_w, conv_b, w_down):
    act = ffn_gate_up(h, w_gate.astype(BF16), w_up.astype(BF16), conv_w, conv_b)
    return matmul_residual(act, w_down.astype(BF16), x, tm=1024, tn=256)


def kernel(x, rel_table, ev_norm, ev_w_in, a_qk_gain, a_lambda, a_subln_gain, b_qk_gain, ev_w_out,
           od_norm, od_w_in, c_qk_gain, c_cmp_pe, c_cmp_w1, c_cmp_w2, od_w_out,
           ffn_norm, ffn_w_gate, ffn_w_up, ffn_conv_w, ffn_conv_b, ffn_w_down):
    b, s, d = x.shape
    depth = ffn_norm.shape[0]
    outs = []
    for bi in range(b):
        y = x[bi].astype(F32)
        for i in range(depth):
            if i % 2 == 0:
                e = i // 2
                lam_init = 0.8 - 0.6 * math.exp(-0.3 * i)
                y = even_mixer(y, rmsnorm(y, ev_norm[e]), ev_w_in[e], a_qk_gain[e], a_lambda[e],
                               a_subln_gain[e], b_qk_gain[e], ev_w_out[e], rel_table, lam_init)
            else:
                o = i // 2
                y = nsa_mixer(y, rmsnorm(y, od_norm[o]), od_w_in[o], c_qk_gain[o], c_cmp_pe[o],
                              c_cmp_w1[o], c_cmp_w2[o], od_w_out[o], rel_table)
            y = conv_ffn(y, rmsnorm(y, ffn_norm[i]), ffn_w_gate[i], ffn_w_up[i], ffn_conv_w[i],
                         ffn_conv_b[i], ffn_w_down[i])
        outs.append(y)
    return jnp.stack(outs).astype(x.dtype)
```

```python
import functools
import math

import numpy as np
import jax
import jax.numpy as jnp
from jax import lax
from jax.experimental import pallas as pl
from jax.experimental.pallas import tpu as pltpu

HEAD_DIM = 128
A_HEADS = 8
B_HEADS = 16
B_CONFIGS = ((128, 1), (512, 4), (2048, 16))
C_HEADS = 32
C_GROUPS = 2
C_HPG = C_HEADS // C_GROUPS
CMP_LEN = 32
CMP_STRIDE = 16
SEL_LEN = 64
SEL_TOPK = 16
C_WINDOW = 512
REL_BUCKETS = 32
REL_MAX_DIST = 2048
EPS = 1e-6
NEG = -1e30
FORCE = 1e9

LANES = 128
VMEM_LIMIT = 56 * 1024 * 1024

F32 = jnp.float32
BF16 = jnp.bfloat16


def _cparams(sem):
    return pltpu.CompilerParams(dimension_semantics=sem, vmem_limit_bytes=VMEM_LIMIT)


def _dot(a, b):
    return jnp.dot(a, b, preferred_element_type=F32)


def _dot_nt(a, b):
    return lax.dot_general(a, b, (((1,), (1,)), ((), ())), preferred_element_type=F32)


def _bucket_np(dist):
    n = np.maximum(dist, 0).astype(np.int32)
    max_exact = REL_BUCKETS // 2
    nf = np.maximum(n, 1).astype(np.float32)
    ratio = np.log(nf / np.float32(max_exact)) / np.float32(math.log(REL_MAX_DIST / max_exact))
    large = np.minimum(max_exact + (ratio * np.float32(REL_BUCKETS - max_exact)).astype(np.int32),
                       REL_BUCKETS - 1)
    return np.where(n < max_exact, n, large).astype(np.int32)


def _far_distance():
    b = _bucket_np(np.arange(4 * REL_MAX_DIST))
    return int(np.max(np.nonzero(b != REL_BUCKETS - 1)[0])) + 1


FAR = _far_distance()


def _bias_by_distance(table_t, dist, valid):
    b = jnp.take(table_t, jnp.asarray(_bucket_np(dist)), axis=1)
    return jnp.where(jnp.asarray(valid), b, NEG).astype(F32)


def _toeplitz_kernel(g_ref, o_ref, *, rows, width, shift, stride, transpose):
    g = g_ref[0, 0]
    x = jnp.broadcast_to(g, (rows, g.shape[1]))
    y = pltpu.roll(x, shift, 1, stride=stride, stride_axis=0)[:, :width]
    o_ref[0, 0] = y.T if transpose else y


def toeplitz_tiles(g, rows, width, stride, offset, transpose=False):
    h, nd, wx = g.shape
    assert wx % LANES == 0 and 0 <= offset - stride * (rows - 1) and width - 1 + offset < wx
    shape = (width, rows) if transpose else (rows, width)
    return pl.pallas_call(
        functools.partial(_toeplitz_kernel, rows=rows, width=width, shift=(wx - offset) % wx, stride=stride,
                          transpose=transpose),
        out_shape=jax.ShapeDtypeStruct((h, nd) + shape, F32),
        grid=(h, nd),
        in_specs=[pl.BlockSpec((1, 1, 1, wx), lambda a, b: (a, b, 0, 0))],
        out_specs=pl.BlockSpec((1, 1) + shape, lambda a, b: (a, b, 0, 0)),
        name="toeplitz_tiles",
        compiler_params=_cparams(("parallel", "parallel")),
    )(g.reshape(h, nd, 1, wx))


def causal_bias_tiles(table_t, t, n_tiles, window=None):
    span = (n_tiles + 1) * t
    dist = (n_tiles * t) - np.arange(span)
    valid = dist >= 0 if window is None else (dist >= 0) & (dist < window)
    vrev = _bias_by_distance(table_t, dist, valid)
    g = jnp.stack([vrev[:, (n_tiles - d - 1) * t:(n_tiles - d + 1) * t] for d in range(n_tiles)], axis=1)
    return toeplitz_tiles(g, t, t, 1, t)


def _rmsnorm_kernel(x_ref, g_ref, o_ref):
    x = x_ref[...]
    ms = jnp.mean(x * x, axis=-1, keepdims=True)
    o_ref[...] = ((x * lax.rsqrt(ms + EPS)) * g_ref[...]).astype(o_ref.dtype)


def rmsnorm(x, gain):
    s, d = x.shape
    tr = min(256, s)
    return pl.pallas_call(
        _rmsnorm_kernel,
        out_shape=jax.ShapeDtypeStruct((s, d), BF16),
        grid=(s // tr,),
        in_specs=[pl.BlockSpec((tr, d), lambda i: (i, 0)), pl.BlockSpec((1, d), lambda i: (0, 0))],
        out_specs=pl.BlockSpec((tr, d), lambda i: (i, 0)),
        name="rmsnorm",
        compiler_params=_cparams(("parallel",)),
    )(x, gain.reshape(1, d).astype(F32))


def _proj_kernel(x_ref, w_ref, gain_ref, flag_ref, o_ref, *, nblk):
    acc = _dot(x_ref[...], w_ref[...])
    for c in range(nblk):
        sl = slice(c * LANES, (c + 1) * LANES)
        blk = acc[:, sl]
        ms = jnp.mean(blk * blk, axis=-1, keepdims=True)
        r = jnp.where(flag_ref[:, sl] > 0, lax.rsqrt(ms + EPS), 1.0)
        o_ref[c] = ((blk * r) * gain_ref[:, sl]).astype(o_ref.dtype)


def proj_headmajor(x, w, gain, flag, tn):
    s, k = x.shape
    n = w.shape[1]
    tm = min(1024, s)
    nblk = tn // LANES
    return pl.pallas_call(
        functools.partial(_proj_kernel, nblk=nblk),
        out_shape=jax.ShapeDtypeStruct((n // LANES, s, LANES), BF16),
        grid=(s // tm, n // tn),
        in_specs=[pl.BlockSpec((tm, k), lambda i, j: (i, 0)),
                  pl.BlockSpec((k, tn), lambda i, j: (0, j)),
                  pl.BlockSpec((1, tn), lambda i, j: (0, j)),
                  pl.BlockSpec((1, tn), lambda i, j: (0, j))],
        out_specs=pl.BlockSpec((nblk, tm, LANES), lambda i, j: (j, i, 0)),
        name="proj_headmajor",
        compiler_params=_cparams(("parallel", "parallel")),
    )(x, w, gain.reshape(1, n).astype(F32), flag.reshape(1, n).astype(F32))


def _mm_res_kernel(x_ref, w_ref, r_ref, o_ref):
    o_ref[...] = r_ref[...] + _dot(x_ref[...], w_ref[...])


def matmul_residual(x, w, res, tm, tn):
    s, k = x.shape
    n = w.shape[1]
    tm = min(tm, s)
    tn = min(tn, n)
    return pl.pallas_call(
        _mm_res_kernel,
        out_shape=jax.ShapeDtypeStruct((s, n), F32),
        grid=(s // tm, n // tn),
        in_specs=[pl.BlockSpec((tm, k), lambda i, j: (i, 0), pipeline_mode=pl.Buffered(1)),
                  pl.BlockSpec((k, tn), lambda i, j: (0, j)),
                  pl.BlockSpec((tm, tn), lambda i, j: (i, j))],
        out_specs=pl.BlockSpec((tm, tn), lambda i, j: (i, j)),
        name="matmul_residual",
        compiler_params=_cparams(("parallel", "arbitrary")),
    )(x, w, res)


def _mm2_res_kernel(x1_ref, x2_ref, w_ref, r_ref, o_ref, *, k1):
    acc = _dot(x1_ref[...], w_ref[0:k1, :]) + _dot(x2_ref[...], w_ref[k1:, :])
    o_ref[...] = r_ref[...] + acc


def matmul2_residual(x1, x2, w, res, tm, tn):
    s, k1 = x1.shape
    k2 = x2.shape[1]
    n = w.shape[1]
    tm = min(tm, s)
    tn = min(tn, n)
    return pl.pallas_call(
        functools.partial(_mm2_res_kernel, k1=k1),
        out_shape=jax.ShapeDtypeStruct((s, n), F32),
        grid=(s // tm, n // tn),
        in_specs=[pl.BlockSpec((tm, k1), lambda i, j: (i, 0)),
                  pl.BlockSpec((tm, k2), lambda i, j: (i, 0)),
                  pl.BlockSpec((k1 + k2, tn), lambda i, j: (0, j)),
                  pl.BlockSpec((tm, tn), lambda i, j: (i, j))],
        out_specs=pl.BlockSpec((tm, tn), lambda i, j: (i, j)),
        name="matmul2_residual",
        compiler_params=_cparams(("parallel", "arbitrary")),
    )(x1, x2, w, res)


def _ffn1_kernel(x_ref, wg_ref, wu_ref, cw_ref, cb_ref, o_ref, carry_ref):
    i = pl.program_id(0)
    j = pl.program_id(1)
    x = x_ref[...]
    g = _dot(x, wg_ref[...])
    u = _dot(x, wu_ref[...])
    tm = g.shape[0]

    @pl.when(i == 0)
    def _():
        carry_ref[j] = jnp.zeros(carry_ref.shape[1:], F32)

    prev = carry_ref[j]
    carry_ref[j] = g[tm - 8:tm, :]
    row = lax.broadcasted_iota(jnp.int32, g.shape, 0)
    p7 = prev[7:8, :]
    p6 = prev[6:7, :]
    g1 = jnp.where(row == 0, p7, pltpu.roll(g, 1, 0))
    g2 = jnp.where(row == 0, p6, jnp.where(row == 1, p7, pltpu.roll(g, 2, 0)))
    cw = cw_ref[...]
    gc = g2 * cw[0:1, :] + g1 * cw[1:2, :] + g * cw[2:3, :] + cb_ref[...]
    o_ref[...] = ((gc * jax.nn.sigmoid(gc)) * u).astype(o_ref.dtype)


def ffn_gate_up(h, wg, wu, conv_w, conv_b):
    s, d = h.shape
    f = wg.shape[1]
    tm = min(1024, s)
    tn = 256
    nj = f // tn
    return pl.pallas_call(
        _ffn1_kernel,
        out_shape=jax.ShapeDtypeStruct((s, f), BF16),
        grid=(s // tm, nj),
        in_specs=[pl.BlockSpec((tm, d), lambda i, j: (i, 0)),
                  pl.BlockSpec((d, tn), lambda i, j: (0, j)),
                  pl.BlockSpec((d, tn), lambda i, j: (0, j)),
                  pl.BlockSpec((3, tn), lambda i, j: (0, j)),
                  pl.BlockSpec((1, tn), lambda i, j: (0, j))],
        out_specs=pl.BlockSpec((tm, tn), lambda i, j: (i, j)),
        scratch_shapes=[pltpu.VMEM((nj, 8, tn), F32)],
        name="ffn_gate_up",
        compiler_params=_cparams(("arbitrary", "arbitrary")),
    )(h, wg, wu, conv_w.astype(F32), conv_b.reshape(1, f).astype(F32))


def _flash_kernel(qi_ref, kj_ref, bi_ref, fl_ref, *refs, hb, nv, use_sel, rc, diff_lam_init):
    q_ref, k_ref, v_ref, b_ref = refs[:4]
    refs = refs[4:]
    if use_sel:
        sel_ref, en_ref = refs[:2]
        refs = refs[2:]
    if diff_lam_init is not None:
        lam_ref, gain_ref = refs[:2]
        refs = refs[2:]
    o_ref, m_sc, acc_sc, s_sc, p_sc, al_sc = refs[:6]
    if use_sel:
        qa_sc = refs[6]
    per_head_k = k_ref.shape[0] == hb and hb > 1
    per_head_bias = b_ref.shape[0] == hb
    p = pl.program_id(1)
    flags = fl_ref[p]
    dv = nv * LANES
    t = q_ref.shape[1]
    tk = k_ref.shape[1]

    @pl.when((flags & 1) != 0)
    def _():
        m_sc[...] = jnp.full(m_sc.shape, -jnp.inf, F32)
        acc_sc[...] = jnp.zeros(acc_sc.shape, F32)
        if use_sel:
            notsel = (1.0 - sel_ref[0].astype(F32)).astype(BF16)
            for h in range(hb):
                qa_sc[h, :, 0:LANES] = q_ref[h]
                qa_sc[h, :, LANES:] = notsel

    v_aug = jnp.concatenate([v_ref[c] for c in range(nv)] + [jnp.ones((tk, LANES), BF16)], axis=-1)

    for h in range(hb):
        k = k_ref[h if per_head_k else 0]
        if use_sel:
            k = jnp.concatenate([k, en_ref[0]], axis=-1)
        qh = qa_sc[h] if use_sel else q_ref[h]
        s_sc[h] = _dot_nt(qh, k)
        hbias = h if per_head_bias else 0
        for r0 in range(0, t, rc):
            rows = slice(r0, r0 + rc)
            s = s_sc[h, rows, :] + b_ref[hbias, 0, rows, :]
            m_old = m_sc[h, rows, :]
            m_new = jnp.maximum(m_old, jnp.max(s, axis=-1, keepdims=True))
            al_sc[h, rows, :] = jnp.exp(m_old - m_new)
            m_sc[h, rows, :] = m_new
            p_sc[h, rows, :] = jnp.exp(s - jnp.tile(m_new, (1, tk // LANES))).astype(BF16)
        acc_sc[h] = jnp.tile(al_sc[h], (1, nv + 1)) * acc_sc[h] + _dot(p_sc[h], v_aug)

    def normalised(h):
        acc = acc_sc[h]
        return acc[:, 0:dv] / jnp.tile(acc[:, dv:], (1, nv))

    @pl.when((flags & 2) != 0)
    def _():
        if diff_lam_init is None:
            for h in range(hb):
                o_ref[:, h * dv:(h + 1) * dv] = normalised(h).astype(o_ref.dtype)
        else:
            lf = lam_ref[...]
            s1 = jnp.sum(lf[0:1, :] * lf[1:2, :], axis=-1, keepdims=True)
            s2 = jnp.sum(lf[2:3, :] * lf[3:4, :], axis=-1, keepdims=True)
            lam = jnp.exp(s1) - jnp.exp(s2) + diff_lam_init
            o = normalised(0) - lam * normalised(1)
            ms = jnp.mean(o * o, axis=-1, keepdims=True)
            y = ((o * lax.rsqrt(ms + EPS)) * gain_ref[...]) * (1.0 - diff_lam_init)
            o_ref[...] = y.astype(o_ref.dtype)


def _pairs(nq, lookback, max_bias):
    qi, kj, bi, fl = [], [], [], []
    for q in range(nq):
        lo = 0 if lookback is None else max(0, q - lookback)
        for k in range(lo, q + 1):
            qi.append(q)
            kj.append(k)
            bi.append(min(q - k, max_bias))
            fl.append((1 if k == lo else 0) | (2 if k == q else 0))
    return [jnp.asarray(np.asarray(a, np.int32)) for a in (qi, kj, bi, fl)]


def flash_attention(q, k, v, bias, *, name, t, n_groups, hb, nv, q_map, k_map, v_map, b_map,
                    lookback=None, sel=None, key_neg=None, diff_params=None):
    s = q.shape[1]
    nq = s // t
    qi, kj, bi, fl = _pairs(nq, lookback, bias.shape[1] - 1)
    n_pairs = int(qi.shape[0])
    dv = nv * LANES
    use_sel = sel is not None
    diff = diff_params is not None
    in_specs = [
        pl.BlockSpec((hb, t, LANES), lambda g, p, qi, kj, bi, fl: q_map(g, qi[p])),
        pl.BlockSpec((hb if diff else 1, t, LANES), lambda g, p, qi, kj, bi, fl: k_map(g, kj[p])),
        pl.BlockSpec((nv, t, LANES), lambda g, p, qi, kj, bi, fl: v_map(g, kj[p])),
        pl.BlockSpec((1 if diff else hb, 1, t, t), lambda g, p, qi, kj, bi, fl: b_map(g, bi[p])),
    ]
    args = [q, k, v, bias]
    scratch = [pltpu.VMEM((hb, t, LANES), F32), pltpu.VMEM((hb, t, dv + LANES), F32),
               pltpu.VMEM((hb, t, t), F32), pltpu.VMEM((hb, t, t), BF16), pltpu.VMEM((hb, t, LANES), F32)]
    if use_sel:
        nselp = sel.shape[2]
        in_specs += [
            pl.BlockSpec((1, t, nselp), lambda g, p, qi, kj, bi, fl: (g, qi[p], 0)),
            pl.BlockSpec((1, t, nselp), lambda g, p, qi, kj, bi, fl: (kj[p], 0, 0)),
        ]
        args += [sel, key_neg]
        scratch.append(pltpu.VMEM((hb, t, LANES + nselp), BF16))
    lam_init = None
    if diff:
        a_lambda, subln_gain, lam_init = diff_params
        in_specs += [
            pl.BlockSpec((4, HEAD_DIM), lambda g, p, qi, kj, bi, fl: (0, 0)),
            pl.BlockSpec((1, dv), lambda g, p, qi, kj, bi, fl: (0, 0)),
        ]
        args += [a_lambda.astype(F32), subln_gain.reshape(1, dv).astype(F32)]
    out_w = dv if diff else hb * dv
    grid_spec = pltpu.PrefetchScalarGridSpec(
        num_scalar_prefetch=4,
        grid=(n_groups, n_pairs),
        in_specs=in_specs,
        out_specs=pl.BlockSpec((t, out_w), lambda g, p, qi, kj, bi, fl: (qi[p], g)),
        scratch_shapes=scratch,
    )
    rc = max(8, min(t, (16 * 8 * LANES) // t))
    return pl.pallas_call(
        functools.partial(_flash_kernel, hb=hb, nv=nv, use_sel=use_sel, rc=rc, diff_lam_init=lam_init),
        out_shape=jax.ShapeDtypeStruct((s, n_groups * out_w), BF16 if diff else F32),
        grid_spec=grid_spec,
        name=name,
        compiler_params=_cparams(("parallel", "arbitrary")),
    )(qi, kj, bi, fl, *args)


def _dilated_kernel(q_ref, kp_ref, kc_ref, vp_ref, vc_ref, b_ref, o_ref, lse_ref, s_sc, p_sc, m_sc):
    span = q_ref.shape[1]
    rc = min(64, span)
    no_prev = jnp.where(pl.program_id(1) == 0, NEG, 0.0)
    col = lax.broadcasted_iota(jnp.int32, (1, 2 * span), 1)
    prev_mask = jnp.where(col < span, no_prev, 0.0)
    ones = jnp.ones((2 * span, LANES), BF16)
    for h in range(B_HEADS):
        keys = jnp.concatenate([kp_ref[h], kc_ref[h]], axis=0)
        s_sc[h] = _dot_nt(q_ref[h], keys)
        for r0 in range(0, span, rc):
            rows = slice(r0, r0 + rc)
            s = s_sc[h, rows, :] + (b_ref[h, 0, rows, :] + prev_mask)
            m = jnp.max(s, axis=-1, keepdims=True)
            m_sc[h, rows, :] = jnp.broadcast_to(m, (rc, LANES))
            p_sc[h, rows, :] = jnp.exp(s - m).astype(BF16)
        vals = jnp.concatenate([jnp.concatenate([vp_ref[h], vc_ref[h]], axis=0), ones], axis=-1)
        pv = _dot(p_sc[h], vals)
        l = pv[:, HEAD_DIM:]
        sl = slice(h * HEAD_DIM, (h + 1) * HEAD_DIM)
        o_ref[:, sl] = pv[:, 0:HEAD_DIM] / l
        lse_ref[:, sl] = m_sc[h] + jnp.log(l)


def dilated_group(proj_hm, bias, dilation, span, q_blk, k_blk, v_blk):
    nh, s, _ = proj_hm.shape
    l = s // dilation
    nb = l // span
    x = proj_hm.reshape(nh, l, dilation * LANES)
    w = B_HEADS * HEAD_DIM
    hspec = lambda blk, prev: pl.BlockSpec(
        (B_HEADS, span, LANES),
        (lambda r, n: (blk, jnp.maximum(n - 1, 0), r)) if prev else (lambda r, n: (blk, n, r)))
    o, lse = pl.pallas_call(
        _dilated_kernel,
        out_shape=[jax.ShapeDtypeStruct((l, dilation * w), F32)] * 2,
        grid=(dilation, nb),
        in_specs=[hspec(q_blk, False), hspec(k_blk, True), hspec(k_blk, False),
                  hspec(v_blk, True), hspec(v_blk, False),
                  pl.BlockSpec((B_HEADS, 1, span, 2 * span), lambda r, n: (0, 0, 0, 0))],
        out_specs=[pl.BlockSpec((span, w), lambda r, n: (n, r))] * 2,
        scratch_shapes=[pltpu.VMEM((B_HEADS, span, 2 * span), F32),
                        pltpu.VMEM((B_HEADS, span, 2 * span), BF16),
                        pltpu.VMEM((B_HEADS, span, LANES), F32)],
        name="dilated_group",
        compiler_params=_cparams(("parallel", "parallel")),
    )(x, x, x, x, x, bias)
    return o.reshape(s, w), lse.reshape(s, w)


def _mix3_kernel(o1, o2, o3, l1, l2, l3, out_ref):
    a, b, c = l1[...], l2[...], l3[...]
    m = jnp.maximum(jnp.maximum(a, b), c)
    ea, eb, ec = jnp.exp(a - m), jnp.exp(b - m), jnp.exp(c - m)
    out_ref[...] = ((ea * o1[...] + eb * o2[...] + ec * o3[...]) / (ea + eb + ec)).astype(out_ref.dtype)


def mix_dilated(os, lses):
    s, w = os[0].shape
    ts = min(256, s)
    spec = pl.BlockSpec((ts, w), lambda i: (i, 0))
    return pl.pallas_call(
        _mix3_kernel,
        out_shape=jax.ShapeDtypeStruct((s, w), BF16),
        grid=(s // ts,),
        in_specs=[spec] * 6,
        out_specs=spec,
        name="mix_dilated",
        compiler_params=_cparams(("parallel",)),
    )(*os, *lses)


def _compress_kernel(x_ref, pe_ref, w1_ref, w2_ref, gain_ref, o_ref):
    kv = pl.program_id(0)
    x = x_ref[0, 0].astype(F32)
    nc = x.shape[0]
    a = _dot((x + pe_ref[0, 0]).astype(BF16), w1_ref[0, 0])
    b = _dot((x + pe_ref[0, 1]).astype(BF16), w1_ref[0, 1])
    hid = jax.nn.gelu(a + pltpu.roll(b, nc - 1, 0))
    c = _dot(hid.astype(BF16), w2_ref[0])
    ms = jnp.mean(c * c, axis=-1, keepdims=True)
    normed = (c * lax.rsqrt(ms + EPS)) * gain_ref[...]
    o_ref[0, 0] = jnp.where(kv == 0, normed, c).astype(o_ref.dtype)


def compress_kv(xc, pe, w1, w2, gain):
    _, g, nc, cw = xc.shape
    hid = w1.shape[-1]
    return pl.pallas_call(
        _compress_kernel,
        out_shape=jax.ShapeDtypeStruct((2, g, nc, HEAD_DIM), BF16),
        grid=(2, g),
        in_specs=[pl.BlockSpec((1, 1, nc, cw), lambda a, b: (a, b, 0, 0)),
                  pl.BlockSpec((1, 2, 1, cw), lambda a, b: (a, 0, 0, 0)),
                  pl.BlockSpec((1, 2, cw, hid), lambda a, b: (a, 0, 0, 0)),
                  pl.BlockSpec((1, hid, HEAD_DIM), lambda a, b: (a, 0, 0)),
                  pl.BlockSpec((1, HEAD_DIM), lambda a, b: (0, 0))],
        out_specs=pl.BlockSpec((1, 1, nc, HEAD_DIM), lambda a, b: (a, b, 0, 0)),
        name="compress_kv",
        compiler_params=_cparams(("parallel", "parallel")),
    )(xc, pe, w1, w2, gain.reshape(1, HEAD_DIM).astype(F32))


def _cmp_kernel(q_ref, k_ref, v_ref, *rest, nkt, n_top):
    bias_refs = rest[:nkt]
    ov_ref, oc_ref, sel_ref, imp_sc = rest[nkt:]
    qi = pl.program_id(1)
    tq = q_ref.shape[1]
    k = k_ref[0, 0]
    v = v_ref[0, 0]
    imp_sc[...] = jnp.zeros(imp_sc.shape, F32)
    for h in range(C_HPG):
        if nkt == 1:
            bias = bias_refs[0][h, 0]
        else:
            bias = jnp.concatenate([b[h, 0] for b in bias_refs], axis=-1)
        s = _dot_nt(q_ref[h], k) + bias
        mc = (bias > 0.5 * NEG).astype(F32)
        m = jnp.max(s, axis=-1, keepdims=True)
        e = jnp.exp(s - m)
        pc = (e / jnp.sum(e, axis=-1, keepdims=True)) * mc
        oc_ref[:, h * HEAD_DIM:(h + 1) * HEAD_DIM] = _dot(pc.astype(BF16), v)
        imp_sc[...] += pc
    imp = imp_sc[...]
    hi = imp.astype(BF16)
    lo = (imp - hi.astype(F32)).astype(BF16)
    ov = ov_ref[...]
    impn = _dot(hi, ov) + _dot(lo, ov)
    shape = impn.shape
    t = qi * tq + lax.broadcasted_iota(jnp.int32, shape, 0)
    n = lax.broadcasted_iota(jnp.int32, shape, 1)
    cur = lax.shift_right_logical(t, int(math.log2(SEL_LEN)))
    forced = (n == 0) | (n == cur) | (n == cur - 1)
    valid = n <= cur
    score = jnp.where(valid, jnp.where(forced, FORCE, impn), NEG)
    nf = n.astype(F32)
    sel = jnp.zeros(shape, F32)
    for _ in range(n_top):
        m = jnp.max(score, axis=-1, keepdims=True)
        first = jnp.min(jnp.where(score == m, nf, 1e9), axis=-1, keepdims=True)
        pick = nf == first
        sel = jnp.where(pick, 1.0, sel)
        score = jnp.where(pick, -jnp.inf, score)
    sel_ref[0] = jnp.where(valid, sel, 0.0).astype(sel_ref.dtype)


def cmp_attention(q_hm, kvcmp, bias, overlap, tq, n_top):
    s = q_hm.shape[1]
    ncp = kvcmp.shape[2]
    nkt = ncp // LANES
    nselp = overlap.shape[1]
    nd = bias.shape[1]
    per_tile = (LANES * CMP_STRIDE) // tq

    def bias_spec(kt):
        return pl.BlockSpec(
            (C_HPG, 1, tq, LANES),
            lambda g, i: (g, jnp.clip(i - per_tile * kt, -1, nd - 2) + 1, 0, 0))

    w = C_HEADS * HEAD_DIM
    return pl.pallas_call(
        functools.partial(_cmp_kernel, nkt=nkt, n_top=n_top),
        out_shape=[jax.ShapeDtypeStruct((s, w), F32),
                   jax.ShapeDtypeStruct((C_GROUPS, s, nselp), BF16)],
        grid=(C_GROUPS, s // tq),
        in_specs=[pl.BlockSpec((C_HPG, tq, LANES), lambda g, i: (g, i, 0)),
                  pl.BlockSpec((1, 1, ncp, LANES), lambda g, i: (0, g, 0, 0)),
                  pl.BlockSpec((1, 1, ncp, LANES), lambda g, i: (1, g, 0, 0))]
                 + [bias_spec(kt) for kt in range(nkt)]
                 + [pl.BlockSpec((ncp, nselp), lambda g, i: (0, 0))],
        out_specs=[pl.BlockSpec((tq, w // C_GROUPS), lambda g, i: (i, g)),
                   pl.BlockSpec((1, tq, nselp), lambda g, i: (g, i, 0))],
        scratch_shapes=[pltpu.VMEM((tq, ncp), F32)],
        name="cmp_attention",
        compiler_params=_cparams(("parallel", "parallel")),
    )(q_hm, kvcmp, kvcmp, *([bias] * nkt), overlap)


def _gate_kernel(oc_ref, os_ref, ow_ref, gt_ref, o_ref):
    gt = jax.nn.sigmoid(gt_ref[0].astype(F32))
    for h in range(C_HEADS):
        sl = slice(h * HEAD_DIM, (h + 1) * HEAD_DIM)
        g0 = gt[:, h:h + 1]
        g1 = gt[:, C_HEADS + h:C_HEADS + h + 1]
        g2 = gt[:, 2 * C_HEADS + h:2 * C_HEADS + h + 1]
        o_ref[:, sl] = (g0 * oc_ref[:, sl] + g1 * os_ref[:, sl] + g2 * ow_ref[:, sl]).astype(o_ref.dtype)


def gate_merge(oc, osel, ow, proj_hm, gate_blk):
    s, w = oc.shape
    ts = min(256, s)
    spec = pl.BlockSpec((ts, w), lambda i: (i, 0))
    return pl.pallas_call(
        _gate_kernel,
        out_shape=jax.ShapeDtypeStruct((s, w), BF16),
        grid=(s // ts,),
        in_specs=[spec, spec, spec, pl.BlockSpec((1, ts, LANES), lambda i: (gate_blk, i, 0))],
        out_specs=spec,
        name="gate_merge",
        compiler_params=_cparams(("parallel",)),
    )(oc, osel, ow, proj_hm)


def _tile_gain(g, reps, scale=1.0):
    return jnp.tile(g.astype(F32) * scale, reps)


def even_mixer(x, h, w_in, a_qk_gain, a_lambda, a_subln_gain, b_qk_gain, w_out, rel_table, lam_init):
    s = h.shape[0]
    scale = HEAD_DIM ** -0.5
    na = 2 * A_HEADS
    ones = lambda n: jnp.ones((n * HEAD_DIM,), F32)
    zeros = lambda n: jnp.zeros((n * HEAD_DIM,), F32)
    d = h.shape[1]
    wa = na * HEAD_DIM
    table_t = rel_table.astype(F32).T

    def pair_cols(w):
        return w.reshape(d, 2, A_HEADS, HEAD_DIM).transpose(0, 2, 1, 3).reshape(d, wa)

    w_a = jnp.concatenate([pair_cols(w_in[:, 0:wa]), pair_cols(w_in[:, wa:2 * wa]), w_in[:, 2 * wa:3 * wa]],
                          axis=1).astype(BF16)
    gain_a = jnp.concatenate([_tile_gain(a_qk_gain[0], na, scale), _tile_gain(a_qk_gain[1], na), ones(na)])
    flag_a = jnp.concatenate([ones(2 * na), zeros(na)])
    proj_a = proj_headmajor(h, w_a, gain_a, flag_a, tn=512)
    t = min(1024, s)
    nda = min(-(-(FAR - 1) // t) + 2, s // t)
    bias_a = causal_bias_tiles(table_t[:A_HEADS], t, nda)
    ao = flash_attention(
        proj_a, proj_a, proj_a, bias_a, name="diff_attention", t=t, n_groups=A_HEADS, hb=2, nv=2,
        q_map=lambda g, qb: (g, qb, 0),
        k_map=lambda g, kb: (A_HEADS + g, kb, 0),
        v_map=lambda g, kb: (2 * A_HEADS + g, kb, 0),
        b_map=lambda g, bt: (g, bt, 0, 0),
        diff_params=(a_lambda, a_subln_gain, lam_init))

    gain_b = jnp.concatenate([_tile_gain(b_qk_gain[0], B_HEADS, scale), _tile_gain(b_qk_gain[1], B_HEADS),
                              ones(B_HEADS)])
    flag_b = jnp.concatenate([ones(2 * B_HEADS), zeros(B_HEADS)])
    proj = proj_headmajor(h, w_in[:, 3 * wa:].astype(BF16), gain_b, flag_b, tn=512)
    hb0 = 0
    os, lses = [], []
    for window, dilation in B_CONFIGS:
        span = window // dilation
        j = 2 * span - np.arange(3 * span)
        g_b = _bias_by_distance(table_t[A_HEADS:A_HEADS + B_HEADS], j * dilation, (j >= 0) & (j <= span))
        bias_b = toeplitz_tiles(g_b[:, None, :], span, 2 * span, 1, span)
        o, lse = dilated_group(proj, bias_b, dilation, span, hb0, hb0 + 1, hb0 + 2)
        os.append(o)
        lses.append(lse)
    bo = mix_dilated(os, lses)
    return matmul2_residual(ao, bo, w_out.astype(BF16), x, tm=1024, tn=512)


def nsa_mixer(x, h, w_in, c_qk_gain, c_cmp_pe, c_cmp_w1, c_cmp_w2, w_out, rel_table):
    s, d = h.shape
    scale = HEAD_DIM ** -0.5
    g = C_GROUPS
    n_in = w_in.shape[1]
    n_pad = -(-n_in // 640) * 640
    ones = lambda n: jnp.ones((n,), F32)
    zeros = lambda n: jnp.zeros((n,), F32)
    kvw = g * HEAD_DIM
    gain = jnp.concatenate([_tile_gain(c_qk_gain[0], C_HEADS, scale), ones(2 * kvw),
                            _tile_gain(c_qk_gain[2], g), ones(kvw), _tile_gain(c_qk_gain[3], g), ones(kvw),
                            ones(n_pad - C_HEADS * HEAD_DIM - 6 * kvw)])
    flag = jnp.concatenate([ones(C_HEADS * HEAD_DIM), zeros(2 * kvw), ones(kvw), zeros(kvw), ones(kvw),
                            zeros(kvw), zeros(n_pad - C_HEADS * HEAD_DIM - 6 * kvw)])
    w_pad = jnp.pad(w_in.astype(BF16), ((0, 0), (0, n_pad - n_in)))
    proj = proj_headmajor(h, w_pad, gain, flag, tn=640)
    kv0 = C_HEADS
    table_t = rel_table.astype(F32).T[:C_HEADS]

    nc = s // CMP_STRIDE
    cw = CMP_STRIDE * HEAD_DIM
    xc = proj[kv0:kv0 + 2 * g].reshape(2, g, nc, cw)
    pe = c_cmp_pe.astype(F32).reshape(2, 2, 1, cw)
    w1 = c_cmp_w1.astype(BF16).reshape(2, 2, cw, c_cmp_w1.shape[-1])
    kvcmp = compress_kv(xc, pe, w1, c_cmp_w2.astype(BF16), c_qk_gain[1])
    if nc % LANES:
        kvcmp = jnp.pad(kvcmp, ((0, 0), (0, 0), (0, LANES - nc % LANES), (0, 0)))
    ncp = kvcmp.shape[2]

    tq = min(256, s)
    n_cmp = (s - CMP_LEN) // CMP_STRIDE + 1
    n_sel = s // SEL_LEN
    nselp = -(-n_sel // LANES) * LANES
    n_top = min(SEL_TOPK, n_sel)
    off = CMP_STRIDE * (LANES - 1)
    far_tile = -(-(FAR + off + CMP_LEN - 1) // tq)
    nd = far_tile + 2
    wx = -(-(tq + off) // LANES) * LANES
    dist_c = np.arange(tq * (nd - 1) + wx) - (tq + off + CMP_LEN - 1)
    vec_c = _bias_by_distance(table_t, dist_c, dist_c >= 0)
    g_c = jnp.stack([vec_c[:, tq * d:tq * d + wx] for d in range(nd)], axis=1)
    bias_c = toeplitz_tiles(g_c, LANES, tq, CMP_STRIDE, off, transpose=True)
    c_start = np.arange(ncp) * CMP_STRIDE
    s_start = np.arange(nselp) * SEL_LEN
    ov = ((c_start[:, None] < s_start[None, :] + SEL_LEN) & (c_start[:, None] + CMP_LEN > s_start[None, :])
          & (np.arange(ncp)[:, None] < n_cmp) & (np.arange(nselp)[None, :] < n_sel))
    oc, sel = cmp_attention(proj, kvcmp, bias_c, jnp.asarray(ov.astype(np.float32), BF16), tq, n_top)

    t = tq
    nds = min(-(-(FAR - 1) // t) + 2, s // t)
    bias_s = causal_bias_tiles(table_t, t, nds)
    key_blk = np.arange(s) // SEL_LEN
    key_neg = np.where(key_blk[:, None] == np.arange(nselp)[None, :], NEG, 0.0).reshape(s // t, t, nselp)
    osel = flash_attention(
        proj, proj, proj, bias_s, name="selected_attention", t=t, n_groups=g, hb=C_HPG, nv=1,
        q_map=lambda gg, qb: (gg, qb, 0),
        k_map=lambda gg, kb: (kv0 + 2 * g + gg, kb, 0),
        v_map=lambda gg, kb: (kv0 + 3 * g + gg, kb, 0),
        b_map=lambda gg, bt: (gg, bt, 0, 0),
        sel=sel, key_neg=jnp.asarray(key_neg.astype(np.float32), BF16))

    look = -(-(C_WINDOW - 1) // t)
    ndw = min(look + 1, s // t)
    bias_w = causal_bias_tiles(table_t, t, ndw, window=C_WINDOW)
    ow = flash_attention(
        proj, proj, proj, bias_w, name="window_attention", t=t, n_groups=g, hb=C_HPG, nv=1, lookback=look,
        q_map=lambda gg, qb: (gg, qb, 0),
        k_map=lambda gg, kb: (kv0 + 4 * g + gg, kb, 0),
        v_map=lambda gg, kb: (kv0 + 5 * g + gg, kb, 0),
        b_map=lambda gg, bt: (gg, bt, 0, 0))

    o = gate_merge(oc, osel, ow, proj, kv0 + 6 * g)
    return matmul_residual(o, w_out.astype(BF16), x, tm=1024, tn=512)


def conv_ffn(x, h, w_gate, w_up, conv_w, conv_b, w_down):
    act = ffn_gate_up(h, w_gate.astype(BF16), w_up.astype(BF16), conv_w, conv_b)
    return matmul_residual(act, w_down.astype(BF16), x, tm=1024, tn=256)


def kernel(x, rel_table, ev_norm, ev_w_in, a_qk_gain, a_lambda, a_subln_gain, b_qk_gain, ev_w_out,
           od_norm, od_w_in, c_qk_gain, c_cmp_pe, c_cmp_w1, c_cmp_w2, od_w_out,
           ffn_norm, ffn_w_gate, ffn_w_up, ffn_conv_w, ffn_conv_b, ffn_w_down):
    b, s, d = x.shape
    depth = ffn_norm.shape[0]
    outs = []
    for bi in range(b):
        y = x[bi].astype(F32)
        for i in range(depth):
            if i % 2 == 0:
                e = i // 2
                lam_init = 0.8 - 0.6 * math.exp(-0.3 * i)
                y = even_mixer(y, rmsnorm(y, ev_norm[e]), ev_w_in[e], a_qk_gain[e], a_lambda[e],
                               a_subln_gain[e], b_qk_gain[e], ev_w_out[e], rel_table, lam_init)
            else:
                o = i // 2
                y = nsa_mixer(y, rmsnorm(y, od_norm[o]), od_w_in[o], c_qk_gain[o], c_cmp_pe[o],
                              c_cmp_w1[o], c_cmp_w2[o], od_w_out[o], rel_table)
            y = conv_ffn(y, rmsnorm(y, ffn_norm[i]), ffn_w_gate[i], ffn_w_up[i], ffn_conv_w[i],
                         ffn_conv_b[i], ffn_w_down[i])
        outs.append(y)
    return jnp.stack(outs).astype(x.dtype)
```

```python
import functools
import math

import numpy as np
import jax
import jax.numpy as jnp
from jax import lax
from jax.experimental import pallas as pl
from jax.experimental.pallas import tpu as pltpu

HEAD_DIM = 128
A_HEADS = 8
B_HEADS = 16
B_CONFIGS = ((128, 1), (512, 4), (2048, 16))
C_HEADS = 32
C_GROUPS = 2
C_HPG = C_HEADS // C_GROUPS
CMP_LEN = 32
CMP_STRIDE = 16
SEL_LEN = 64
SEL_TOPK = 16
C_WINDOW = 512
REL_BUCKETS = 32
REL_MAX_DIST = 2048
EPS = 1e-6
NEG = -1e30
FORCE = 1e9

LANES = 128
VMEM_LIMIT = 56 * 1024 * 1024

F32 = jnp.float32
BF16 = jnp.bfloat16


def _cparams(sem):
    return pltpu.CompilerParams(dimension_semantics=sem, vmem_limit_bytes=VMEM_LIMIT)


def _dot(a, b):
    return jnp.dot(a, b, preferred_element_type=F32)


def _dot_nt(a, b):
    return lax.dot_general(a, b, (((1,), (1,)), ((), ())), preferred_element_type=F32)


def _bucket_np(dist):
    n = np.maximum(dist, 0).astype(np.int32)
    max_exact = REL_BUCKETS // 2
    nf = np.maximum(n, 1).astype(np.float32)
    ratio = np.log(nf / np.float32(max_exact)) / np.float32(math.log(REL_MAX_DIST / max_exact))
    large = np.minimum(max_exact + (ratio * np.float32(REL_BUCKETS - max_exact)).astype(np.int32),
                       REL_BUCKETS - 1)
    return np.where(n < max_exact, n, large).astype(np.int32)


def _far_distance():
    b = _bucket_np(np.arange(4 * REL_MAX_DIST))
    return int(np.max(np.nonzero(b != REL_BUCKETS - 1)[0])) + 1


FAR = _far_distance()


def _bias_by_distance(table_t, dist, valid):
    b = jnp.take(table_t, jnp.asarray(_bucket_np(dist)), axis=1)
    return jnp.where(jnp.asarray(valid), b, NEG).astype(F32)


def _toeplitz_kernel(g_ref, o_ref, *, rows, width, shift, stride, transpose):
    g = g_ref[0, 0]
    x = jnp.broadcast_to(g, (rows, g.shape[1]))
    y = pltpu.roll(x, shift, 1, stride=stride, stride_axis=0)[:, :width]
    o_ref[0, 0] = y.T if transpose else y


def toeplitz_tiles(g, rows, width, stride, offset, transpose=False):
    h, nd, wx = g.shape
    assert wx % LANES == 0 and 0 <= offset - stride * (rows - 1) and width - 1 + offset < wx
    shape = (width, rows) if transpose else (rows, width)
    return pl.pallas_call(
        functools.partial(_toeplitz_kernel, rows=rows, width=width, shift=(wx - offset) % wx, stride=stride,
                          transpose=transpose),
        out_shape=jax.ShapeDtypeStruct((h, nd) + shape, F32),
        grid=(h, nd),
        in_specs=[pl.BlockSpec((1, 1, 1, wx), lambda a, b: (a, b, 0, 0))],
        out_specs=pl.BlockSpec((1, 1) + shape, lambda a, b: (a, b, 0, 0)),
        name="toeplitz_tiles",
        compiler_params=_cparams(("parallel", "parallel")),
    )(g.reshape(h, nd, 1, wx))


def causal_bias_tiles(table_t, t, n_tiles, window=None):
    span = (n_tiles + 1) * t
    dist = (n_tiles * t) - np.arange(span)
    valid = dist >= 0 if window is None else (dist >= 0) & (dist < window)
    vrev = _bias_by_distance(table_t, dist, valid)
    g = jnp.stack([vrev[:, (n_tiles - d - 1) * t:(n_tiles - d + 1) * t] for d in range(n_tiles)], axis=1)
    return toeplitz_tiles(g, t, t, 1, t)


def _rmsnorm_kernel(x_ref, g_ref, o_ref):
    x = x_ref[...]
    ms = jnp.mean(x * x, axis=-1, keepdims=True)
    o_ref[...] = ((x * lax.rsqrt(ms + EPS)) * g_ref[...]).astype(o_ref.dtype)


def rmsnorm(x, gain):
    s, d = x.shape
    tr = min(256, s)
    return pl.pallas_call(
        _rmsnorm_kernel,
        out_shape=jax.ShapeDtypeStruct((s, d), BF16),
        grid=(s // tr,),
        in_specs=[pl.BlockSpec((tr, d), lambda i: (i, 0)), pl.BlockSpec((1, d), lambda i: (0, 0))],
        out_specs=pl.BlockSpec((tr, d), lambda i: (i, 0)),
        name="rmsnorm",
        compiler_params=_cparams(("parallel",)),
    )(x, gain.reshape(1, d).astype(F32))


ROW_SUB = 256


def col_blocks(w, tn):
    k, n = w.shape
    return w.reshape(k, n // tn, tn).transpose(1, 0, 2).astype(BF16)


def _proj_kernel(x_ref, w_ref, gain_ref, flag_ref, o_ref, *, nblk, ts):
    w = w_ref[0]
    for r0 in range(0, x_ref.shape[0], ts):
        acc = _dot(x_ref[r0:r0 + ts, :], w)
        for c in range(nblk):
            sl = slice(c * LANES, (c + 1) * LANES)
            blk = acc[:, sl]
            ms = jnp.mean(blk * blk, axis=-1, keepdims=True)
            r = jnp.where(flag_ref[:, sl] > 0, lax.rsqrt(ms + EPS), 1.0)
            o_ref[c, r0:r0 + ts, :] = ((blk * r) * gain_ref[:, sl]).astype(o_ref.dtype)


def proj_headmajor(x, w, gain, flag, tn):
    s, k = x.shape
    n = w.shape[1]
    tm = min(1024, s)
    nblk = tn // LANES
    return pl.pallas_call(
        functools.partial(_proj_kernel, nblk=nblk, ts=min(ROW_SUB, tm)),
        out_shape=jax.ShapeDtypeStruct((n // LANES, s, LANES), BF16),
        grid=(s // tm, n // tn),
        in_specs=[pl.BlockSpec((tm, k), lambda i, j: (i, 0)),
                  pl.BlockSpec((1, k, tn), lambda i, j: (j, 0, 0)),
                  pl.BlockSpec((1, tn), lambda i, j: (0, j)),
                  pl.BlockSpec((1, tn), lambda i, j: (0, j))],
        out_specs=pl.BlockSpec((nblk, tm, LANES), lambda i, j: (j, i, 0)),
        name="proj_headmajor",
        compiler_params=_cparams(("parallel", "parallel")),
    )(x, col_blocks(w, tn), gain.reshape(1, n).astype(F32), flag.reshape(1, n).astype(F32))


def _mm_res_kernel(x_ref, w_ref, r_ref, o_ref):
    o_ref[...] = r_ref[...] + _dot(x_ref[...], w_ref[0])


def matmul_residual(x, w, res, tm, tn):
    s, k = x.shape
    n = w.shape[1]
    tm = min(tm, s)
    tn = min(tn, n)
    return pl.pallas_call(
        _mm_res_kernel,
        out_shape=jax.ShapeDtypeStruct((s, n), F32),
        grid=(s // tm, n // tn),
        in_specs=[pl.BlockSpec((tm, k), lambda i, j: (i, 0), pipeline_mode=pl.Buffered(1)),
                  pl.BlockSpec((1, k, tn), lambda i, j: (j, 0, 0)),
                  pl.BlockSpec((tm, tn), lambda i, j: (i, j))],
        out_specs=pl.BlockSpec((tm, tn), lambda i, j: (i, j)),
        name="matmul_residual",
        compiler_params=_cparams(("parallel", "arbitrary")),
    )(x, col_blocks(w, tn), res)


def _mm2_res_kernel(x1_ref, x2_ref, w_ref, r_ref, o_ref, *, k1):
    acc = _dot(x1_ref[...], w_ref[0, 0:k1, :]) + _dot(x2_ref[...], w_ref[0, k1:, :])
    o_ref[...] = r_ref[...] + acc


def matmul2_residual(x1, x2, w, res, tm, tn):
    s, k1 = x1.shape
    k2 = x2.shape[1]
    n = w.shape[1]
    tm = min(tm, s)
    tn = min(tn, n)
    return pl.pallas_call(
        functools.partial(_mm2_res_kernel, k1=k1),
        out_shape=jax.ShapeDtypeStruct((s, n), F32),
        grid=(s // tm, n // tn),
        in_specs=[pl.BlockSpec((tm, k1), lambda i, j: (i, 0)),
                  pl.BlockSpec((tm, k2), lambda i, j: (i, 0)),
                  pl.BlockSpec((1, k1 + k2, tn), lambda i, j: (j, 0, 0)),
                  pl.BlockSpec((tm, tn), lambda i, j: (i, j))],
        out_specs=pl.BlockSpec((tm, tn), lambda i, j: (i, j)),
        name="matmul2_residual",
        compiler_params=_cparams(("parallel", "arbitrary")),
    )(x1, x2, col_blocks(w, tn), res)


def _ffn1_kernel(x_ref, wg_ref, wu_ref, cw_ref, cb_ref, o_ref, carry_ref, *, ts):
    i = pl.program_id(0)
    j = pl.program_id(1)

    @pl.when(i == 0)
    def _():
        carry_ref[j] = jnp.zeros(carry_ref.shape[1:], F32)

    prev = carry_ref[j]
    wg = wg_ref[0]
    wu = wu_ref[0]
    cw = cw_ref[...]
    cb = cb_ref[...]
    row = lax.broadcasted_iota(jnp.int32, (ts, wg.shape[1]), 0)
    for r0 in range(0, x_ref.shape[0], ts):
        x = x_ref[r0:r0 + ts, :]
        g = _dot(x, wg)
        u = _dot(x, wu)
        p7 = prev[7:8, :]
        p6 = prev[6:7, :]
        g1 = jnp.where(row == 0, p7, pltpu.roll(g, 1, 0))
        g2 = jnp.where(row == 0, p6, jnp.where(row == 1, p7, pltpu.roll(g, 2, 0)))
        gc = g2 * cw[0:1, :] + g1 * cw[1:2, :] + g * cw[2:3, :] + cb
        o_ref[r0:r0 + ts, :] = ((gc * jax.nn.sigmoid(gc)) * u).astype(o_ref.dtype)
        prev = g[ts - 8:ts, :]
    carry_ref[j] = prev


def ffn_gate_up(h, wg, wu, conv_w, conv_b):
    s, d = h.shape
    f = wg.shape[1]
    tm = min(1024, s)
    tn = 256
    nj = f // tn
    return pl.pallas_call(
        functools.partial(_ffn1_kernel, ts=min(ROW_SUB, tm)),
        out_shape=jax.ShapeDtypeStruct((s, f), BF16),
        grid=(s // tm, nj),
        in_specs=[pl.BlockSpec((tm, d), lambda i, j: (i, 0)),
                  pl.BlockSpec((1, d, tn), lambda i, j: (j, 0, 0)),
                  pl.BlockSpec((1, d, tn), lambda i, j: (j, 0, 0)),
                  pl.BlockSpec((3, tn), lambda i, j: (0, j)),
                  pl.BlockSpec((1, tn), lambda i, j: (0, j))],
        out_specs=pl.BlockSpec((tm, tn), lambda i, j: (i, j)),
        scratch_shapes=[pltpu.VMEM((nj, 8, tn), F32)],
        name="ffn_gate_up",
        compiler_params=_cparams(("arbitrary", "arbitrary")),
    )(h, col_blocks(wg, tn), col_blocks(wu, tn), conv_w.astype(F32), conv_b.reshape(1, f).astype(F32))


def _flash_kernel(qi_ref, kj_ref, bi_ref, fl_ref, *refs, hb, nv, use_sel, rc, diff_lam_init):
    q_ref, k_ref, v_ref, b_ref = refs[:4]
    refs = refs[4:]
    if use_sel:
        sel_ref, en_ref = refs[:2]
        refs = refs[2:]
    if diff_lam_init is not None:
        lam_ref, gain_ref = refs[:2]
        refs = refs[2:]
    o_ref, m_sc, acc_sc, s_sc, p_sc, al_sc = refs[:6]
    if use_sel:
        qa_sc = refs[6]
    per_head_k = k_ref.shape[0] == hb and hb > 1
    per_head_bias = b_ref.shape[0] == hb
    p = pl.program_id(1)
    flags = fl_ref[p]
    dv = nv * LANES
    t = q_ref.shape[1]
    tk = k_ref.shape[1]

    @pl.when((flags & 1) != 0)
    def _():
        m_sc[...] = jnp.full(m_sc.shape, -jnp.inf, F32)
        acc_sc[...] = jnp.zeros(acc_sc.shape, F32)
        if use_sel:
            notsel = (1.0 - sel_ref[0].astype(F32)).astype(BF16)
            for h in range(hb):
                qa_sc[h, :, 0:LANES] = q_ref[h]
                qa_sc[h, :, LANES:] = notsel

    v_aug = jnp.concatenate([v_ref[c] for c in range(nv)] + [jnp.ones((tk, LANES), BF16)], axis=-1)

    for h in range(hb):
        k = k_ref[h if per_head_k else 0]
        if use_sel:
            k = jnp.concatenate([k, en_ref[0]], axis=-1)
        qh = qa_sc[h] if use_sel else q_ref[h]
        s_sc[h] = _dot_nt(qh, k)
        hbias = h if per_head_bias else 0
        for r0 in range(0, t, rc):
            rows = slice(r0, r0 + rc)
            s = s_sc[h, rows, :] + b_ref[hbias, 0, rows, :]
            m_old = m_sc[h, rows, :]
            m_new = jnp.maximum(m_old, jnp.max(s, axis=-1, keepdims=True))
            al_sc[h, rows, :] = jnp.exp(m_old - m_new)
            m_sc[h, rows, :] = m_new
            p_sc[h, rows, :] = jnp.exp(s - jnp.tile(m_new, (1, tk // LANES))).astype(BF16)
        acc_sc[h] = jnp.tile(al_sc[h], (1, nv + 1)) * acc_sc[h] + _dot(p_sc[h], v_aug)

    def normalised(h):
        acc = acc_sc[h]
        return acc[:, 0:dv] / jnp.tile(acc[:, dv:], (1, nv))

    @pl.when((flags & 2) != 0)
    def _():
        if diff_lam_init is None:
            for h in range(hb):
                o_ref[:, h * dv:(h + 1) * dv] = normalised(h).astype(o_ref.dtype)
        else:
            lf = lam_ref[...]
            s1 = jnp.sum(lf[0:1, :] * lf[1:2, :], axis=-1, keepdims=True)
            s2 = jnp.sum(lf[2:3, :] * lf[3:4, :], axis=-1, keepdims=True)
            lam = jnp.exp(s1) - jnp.exp(s2) + diff_lam_init
            o = normalised(0) - lam * normalised(1)
            ms = jnp.mean(o * o, axis=-1, keepdims=True)
            y = ((o * lax.rsqrt(ms + EPS)) * gain_ref[...]) * (1.0 - diff_lam_init)
            o_ref[...] = y.astype(o_ref.dtype)


def _pairs(nq, lookback, max_bias):
    qi, kj, bi, fl = [], [], [], []
    for q in range(nq):
        lo = 0 if lookback is None else max(0, q - lookback)
        for k in range(lo, q + 1):
            qi.append(q)
            kj.append(k)
            bi.append(min(q - k, max_bias))
            fl.append((1 if k == lo else 0) | (2 if k == q else 0))
    return [jnp.asarray(np.asarray(a, np.int32)) for a in (qi, kj, bi, fl)]


def flash_attention(q, k, v, bias, *, name, t, n_groups, hb, nv, q_map, k_map, v_map, b_map,
                    lookback=None, sel=None, key_neg=None, diff_params=None):
    s = q.shape[1]
    nq = s // t
    qi, kj, bi, fl = _pairs(nq, lookback, bias.shape[1] - 1)
    n_pairs = int(qi.shape[0])
    dv = nv * LANES
    use_sel = sel is not None
    diff = diff_params is not None
    in_specs = [
        pl.BlockSpec((hb, t, LANES), lambda g, p, qi, kj, bi, fl: q_map(g, qi[p])),
        pl.BlockSpec((hb if diff else 1, t, LANES), lambda g, p, qi, kj, bi, fl: k_map(g, kj[p])),
        pl.BlockSpec((nv, t, LANES), lambda g, p, qi, kj, bi, fl: v_map(g, kj[p])),
        pl.BlockSpec((1 if diff else hb, 1, t, t), lambda g, p, qi, kj, bi, fl: b_map(g, bi[p])),
    ]
    args = [q, k, v, bias]
    scratch = [pltpu.VMEM((hb, t, LANES), F32), pltpu.VMEM((hb, t, dv + LANES), F32),
               pltpu.VMEM((hb, t, t), F32), pltpu.VMEM((hb, t, t), BF16), pltpu.VMEM((hb, t, LANES), F32)]
    if use_sel:
        nselp = sel.shape[2]
        in_specs += [
            pl.BlockSpec((1, t, nselp), lambda g, p, qi, kj, bi, fl: (g, qi[p], 0)),
            pl.BlockSpec((1, t, nselp), lambda g, p, qi, kj, bi, fl: (kj[p], 0, 0)),
        ]
        args += [sel, key_neg]
        scratch.append(pltpu.VMEM((hb, t, LANES + nselp), BF16))
    lam_init = None
    if diff:
        a_lambda, subln_gain, lam_init = diff_params
        in_specs += [
            pl.BlockSpec((4, HEAD_DIM), lambda g, p, qi, kj, bi, fl: (0, 0)),
            pl.BlockSpec((1, dv), lambda g, p, qi, kj, bi, fl: (0, 0)),
        ]
        args += [a_lambda.astype(F32), subln_gain.reshape(1, dv).astype(F32)]
    out_w = dv if diff else hb * dv
    grid_spec = pltpu.PrefetchScalarGridSpec(
        num_scalar_prefetch=4,
        grid=(n_groups, n_pairs),
        in_specs=in_specs,
        out_specs=pl.BlockSpec((t, out_w), lambda g, p, qi, kj, bi, fl: (qi[p], g)),
        scratch_shapes=scratch,
    )
    rc = max(8, min(t, (16 * 8 * LANES) // t))
    return pl.pallas_call(
        functools.partial(_flash_kernel, hb=hb, nv=nv, use_sel=use_sel, rc=rc, diff_lam_init=lam_init),
        out_shape=jax.ShapeDtypeStruct((s, n_groups * out_w), BF16 if diff else F32),
        grid_spec=grid_spec,
        name=name,
        compiler_params=_cparams(("parallel", "arbitrary")),
    )(qi, kj, bi, fl, *args)


def _dilated_kernel(q_ref, kp_ref, kc_ref, vp_ref, vc_ref, b_ref, o_ref, lse_ref, s_sc, p_sc, m_sc):
    span = q_ref.shape[1]
    rc = min(64, span)
    no_prev = jnp.where(pl.program_id(1) == 0, NEG, 0.0)
    col = lax.broadcasted_iota(jnp.int32, (1, 2 * span), 1)
    prev_mask = jnp.where(col < span, no_prev, 0.0)
    ones = jnp.ones((2 * span, LANES), BF16)
    for h in range(B_HEADS):
        keys = jnp.concatenate([kp_ref[h], kc_ref[h]], axis=0)
        s_sc[h] = _dot_nt(q_ref[h], keys)
        for r0 in range(0, span, rc):
            rows = slice(r0, r0 + rc)
            s = s_sc[h, rows, :] + (b_ref[h, 0, rows, :] + prev_mask)
            m = jnp.max(s, axis=-1, keepdims=True)
            m_sc[h, rows, :] = jnp.broadcast_to(m, (rc, LANES))
            p_sc[h, rows, :] = jnp.exp(s - m).astype(BF16)
        vals = jnp.concatenate([jnp.concatenate([vp_ref[h], vc_ref[h]], axis=0), ones], axis=-1)
        pv = _dot(p_sc[h], vals)
        l = pv[:, HEAD_DIM:]
        sl = slice(h * HEAD_DIM, (h + 1) * HEAD_DIM)
        o_ref[:, sl] = pv[:, 0:HEAD_DIM] / l
        lse_ref[:, sl] = m_sc[h] + jnp.log(l)


def dilated_group(proj_hm, bias, dilation, span, q_blk, k_blk, v_blk):
    nh, s, _ = proj_hm.shape
    l = s // dilation
    nb = l // span
    x = proj_hm.reshape(nh, l, dilation * LANES)
    w = B_HEADS * HEAD_DIM
    hspec = lambda blk, prev: pl.BlockSpec(
        (B_HEADS, span, LANES),
        (lambda r, n: (blk, jnp.maximum(n - 1, 0), r)) if prev else (lambda r, n: (blk, n, r)))
    o, lse = pl.pallas_call(
        _dilated_kernel,
        out_shape=[jax.ShapeDtypeStruct((l, dilation * w), F32)] * 2,
        grid=(dilation, nb),
        in_specs=[hspec(q_blk, False), hspec(k_blk, True), hspec(k_blk, False),
                  hspec(v_blk, True), hspec(v_blk, False),
                  pl.BlockSpec((B_HEADS, 1, span, 2 * span), lambda r, n: (0, 0, 0, 0))],
        out_specs=[pl.BlockSpec((span, w), lambda r, n: (n, r))] * 2,
        scratch_shapes=[pltpu.VMEM((B_HEADS, span, 2 * span), F32),
                        pltpu.VMEM((B_HEADS, span, 2 * span), BF16),
                        pltpu.VMEM((B_HEADS, span, LANES), F32)],
        name="dilated_group",
        compiler_params=_cparams(("parallel", "parallel")),
    )(x, x, x, x, x, bias)
    return o.reshape(s, w), lse.reshape(s, w)


def _mix3_kernel(o1, o2, o3, l1, l2, l3, out_ref):
    a, b, c = l1[...], l2[...], l3[...]
    m = jnp.maximum(jnp.maximum(a, b), c)
    ea, eb, ec = jnp.exp(a - m), jnp.exp(b - m), jnp.exp(c - m)
    out_ref[...] = ((ea * o1[...] + eb * o2[...] + ec * o3[...]) / (ea + eb + ec)).astype(out_ref.dtype)


def mix_dilated(os, lses):
    s, w = os[0].shape
    ts = min(256, s)
    spec = pl.BlockSpec((ts, w), lambda i: (i, 0))
    return pl.pallas_call(
        _mix3_kernel,
        out_shape=jax.ShapeDtypeStruct((s, w), BF16),
        grid=(s // ts,),
        in_specs=[spec] * 6,
        out_specs=spec,
        name="mix_dilated",
        compiler_params=_cparams(("parallel",)),
    )(*os, *lses)


def _compress_kernel(x_ref, pe_ref, w1_ref, w2_ref, gain_ref, o_ref):
    kv = pl.program_id(0)
    x = x_ref[0, 0].astype(F32)
    nc = x.shape[0]
    a = _dot((x + pe_ref[0, 0]).astype(BF16), w1_ref[0, 0])
    b = _dot((x + pe_ref[0, 1]).astype(BF16), w1_ref[0, 1])
    hid = jax.nn.gelu(a + pltpu.roll(b, nc - 1, 0))
    c = _dot(hid.astype(BF16), w2_ref[0])
    ms = jnp.mean(c * c, axis=-1, keepdims=True)
    normed = (c * lax.rsqrt(ms + EPS)) * gain_ref[...]
    o_ref[0, 0] = jnp.where(kv == 0, normed, c).astype(o_ref.dtype)


def compress_kv(xc, pe, w1, w2, gain):
    _, g, nc, cw = xc.shape
    hid = w1.shape[-1]
    return pl.pallas_call(
        _compress_kernel,
        out_shape=jax.ShapeDtypeStruct((2, g, nc, HEAD_DIM), BF16),
        grid=(2, g),
        in_specs=[pl.BlockSpec((1, 1, nc, cw), lambda a, b: (a, b, 0, 0)),
                  pl.BlockSpec((1, 2, 1, cw), lambda a, b: (a, 0, 0, 0)),
                  pl.BlockSpec((1, 2, cw, hid), lambda a, b: (a, 0, 0, 0)),
                  pl.BlockSpec((1, hid, HEAD_DIM), lambda a, b: (a, 0, 0)),
                  pl.BlockSpec((1, HEAD_DIM), lambda a, b: (0, 0))],
        out_specs=pl.BlockSpec((1, 1, nc, HEAD_DIM), lambda a, b: (a, b, 0, 0)),
        name="compress_kv",
        compiler_params=_cparams(("parallel", "parallel")),
    )(xc, pe, w1, w2, gain.reshape(1, HEAD_DIM).astype(F32))


def _cmp_kernel(q_ref, k_ref, v_ref, *rest, nkt, n_top):
    bias_refs = rest[:nkt]
    ov_ref, oc_ref, sel_ref, imp_sc = rest[nkt:]
    qi = pl.program_id(1)
    tq = q_ref.shape[1]
    k = k_ref[0, 0]
    v = v_ref[0, 0]
    imp_sc[...] = jnp.zeros(imp_sc.shape, F32)
    for h in range(C_HPG):
        if nkt == 1:
            bias = bias_refs[0][h, 0]
        else:
            bias = jnp.concatenate([b[h, 0] for b in bias_refs], axis=-1)
        s = _dot_nt(q_ref[h], k) + bias
        mc = (bias > 0.5 * NEG).astype(F32)
        m = jnp.max(s, axis=-1, keepdims=True)
        e = jnp.exp(s - m)
        pc = (e / jnp.sum(e, axis=-1, keepdims=True)) * mc
        oc_ref[:, h * HEAD_DIM:(h + 1) * HEAD_DIM] = _dot(pc.astype(BF16), v)
        imp_sc[...] += pc
    imp = imp_sc[...]
    hi = imp.astype(BF16)
    lo = (imp - hi.astype(F32)).astype(BF16)
    ov = ov_ref[...]
    impn = _dot(hi, ov) + _dot(lo, ov)
    shape = impn.shape
    t = qi * tq + lax.broadcasted_iota(jnp.int32, shape, 0)
    n = lax.broadcasted_iota(jnp.int32, shape, 1)
    cur = lax.shift_right_logical(t, int(math.log2(SEL_LEN)))
    forced = (n == 0) | (n == cur) | (n == cur - 1)
    valid = n <= cur
    score = jnp.where(valid, jnp.where(forced, FORCE, impn), NEG)
    nf = n.astype(F32)
    sel = jnp.zeros(shape, F32)
    for _ in range(n_top):
        m = jnp.max(score, axis=-1, keepdims=True)
        first = jnp.min(jnp.where(score == m, nf, 1e9), axis=-1, keepdims=True)
        pick = nf == first
        sel = jnp.where(pick, 1.0, sel)
        score = jnp.where(pick, -jnp.inf, score)
    sel_ref[0] = jnp.where(valid, sel, 0.0).astype(sel_ref.dtype)


def cmp_attention(q_hm, kvcmp, bias, overlap, tq, n_top):
    s = q_hm.shape[1]
    ncp = kvcmp.shape[2]
    nkt = ncp // LANES
    nselp = overlap.shape[1]
    nd = bias.shape[1]
    per_tile = (LANES * CMP_STRIDE) // tq

    def bias_spec(kt):
        return pl.BlockSpec(
            (C_HPG, 1, tq, LANES),
            lambda g, i: (g, jnp.clip(i - per_tile * kt, -1, nd - 2) + 1, 0, 0))

    w = C_HEADS * HEAD_DIM
    return pl.pallas_call(
        functools.partial(_cmp_kernel, nkt=nkt, n_top=n_top),
        out_shape=[jax.ShapeDtypeStruct((s, w), F32),
                   jax.ShapeDtypeStruct((C_GROUPS, s, nselp), BF16)],
        grid=(C_GROUPS, s // tq),
        in_specs=[pl.BlockSpec((C_HPG, tq, LANES), lambda g, i: (g, i, 0)),
                  pl.BlockSpec((1, 1, ncp, LANES), lambda g, i: (0, g, 0, 0)),
                  pl.BlockSpec((1, 1, ncp, LANES), lambda g, i: (1, g, 0, 0))]
                 + [bias_spec(kt) for kt in range(nkt)]
                 + [pl.BlockSpec((ncp, nselp), lambda g, i: (0, 0))],
        out_specs=[pl.BlockSpec((tq, w // C_GROUPS), lambda g, i: (i, g)),
                   pl.BlockSpec((1, tq, nselp), lambda g, i: (g, i, 0))],
        scratch_shapes=[pltpu.VMEM((tq, ncp), F32)],
        name="cmp_attention",
        compiler_params=_cparams(("parallel", "parallel")),
    )(q_hm, kvcmp, kvcmp, *([bias] * nkt), overlap)


def _gate_kernel(oc_ref, os_ref, ow_ref, gt_ref, o_ref):
    gt = jax.nn.sigmoid(gt_ref[0].astype(F32))
    for h in range(C_HEADS):
        sl = slice(h * HEAD_DIM, (h + 1) * HEAD_DIM)
        g0 = gt[:, h:h + 1]
        g1 = gt[:, C_HEADS + h:C_HEADS + h + 1]
        g2 = gt[:, 2 * C_HEADS + h:2 * C_HEADS + h + 1]
        o_ref[:, sl] = (g0 * oc_ref[:, sl] + g1 * os_ref[:, sl] + g2 * ow_ref[:, sl]).astype(o_ref.dtype)


def gate_merge(oc, osel, ow, proj_hm, gate_blk):
    s, w = oc.shape
    ts = min(256, s)
    spec = pl.BlockSpec((ts, w), lambda i: (i, 0))
    return pl.pallas_call(
        _gate_kernel,
        out_shape=jax.ShapeDtypeStruct((s, w), BF16),
        grid=(s // ts,),
        in_specs=[spec, spec, spec, pl.BlockSpec((1, ts, LANES), lambda i: (gate_blk, i, 0))],
        out_specs=spec,
        name="gate_merge",
        compiler_params=_cparams(("parallel",)),
    )(oc, osel, ow, proj_hm)


def _tile_gain(g, reps, scale=1.0):
    return jnp.tile(g.astype(F32) * scale, reps)


def even_mixer(x, h, w_in, a_qk_gain, a_lambda, a_subln_gain, b_qk_gain, w_out, rel_table, lam_init):
    s = h.shape[0]
    scale = HEAD_DIM ** -0.5
    na = 2 * A_HEADS
    ones = lambda n: jnp.ones((n * HEAD_DIM,), F32)
    zeros = lambda n: jnp.zeros((n * HEAD_DIM,), F32)
    d = h.shape[1]
    wa = na * HEAD_DIM
    table_t = rel_table.astype(F32).T

    def pair_cols(w):
        return w.reshape(d, 2, A_HEADS, HEAD_DIM).transpose(0, 2, 1, 3).reshape(d, wa)

    w_a = jnp.concatenate([pair_cols(w_in[:, 0:wa]), pair_cols(w_in[:, wa:2 * wa]), w_in[:, 2 * wa:3 * wa]],
                          axis=1).astype(BF16)
    gain_a = jnp.concatenate([_tile_gain(a_qk_gain[0], na, scale), _tile_gain(a_qk_gain[1], na), ones(na)])
    flag_a = jnp.concatenate([ones(2 * na), zeros(na)])
    proj_a = proj_headmajor(h, w_a, gain_a, flag_a, tn=512)
    t = min(1024, s)
    nda = min(-(-(FAR - 1) // t) + 2, s // t)
    bias_a = causal_bias_tiles(table_t[:A_HEADS], t, nda)
    ao = flash_attention(
        proj_a, proj_a, proj_a, bias_a, name="diff_attention", t=t, n_groups=A_HEADS, hb=2, nv=2,
        q_map=lambda g, qb: (g, qb, 0),
        k_map=lambda g, kb: (A_HEADS + g, kb, 0),
        v_map=lambda g, kb: (2 * A_HEADS + g, kb, 0),
        b_map=lambda g, bt: (g, bt, 0, 0),
        diff_params=(a_lambda, a_subln_gain, lam_init))

    gain_b = jnp.concatenate([_tile_gain(b_qk_gain[0], B_HEADS, scale), _tile_gain(b_qk_gain[1], B_HEADS),
                              ones(B_HEADS)])
    flag_b = jnp.concatenate([ones(2 * B_HEADS), zeros(B_HEADS)])
    proj = proj_headmajor(h, w_in[:, 3 * wa:], gain_b, flag_b, tn=512)
    hb0 = 0
    os, lses = [], []
    for window, dilation in B_CONFIGS:
        span = window // dilation
        j = 2 * span - np.arange(3 * span)
        g_b = _bias_by_distance(table_t[A_HEADS:A_HEADS + B_HEADS], j * dilation, (j >= 0) & (j <= span))
        bias_b = toeplitz_tiles(g_b[:, None, :], span, 2 * span, 1, span)
        o, lse = dilated_group(proj, bias_b, dilation, span, hb0, hb0 + 1, hb0 + 2)
        os.append(o)
        lses.append(lse)
    bo = mix_dilated(os, lses)
    return matmul2_residual(ao, bo, w_out, x, tm=1024, tn=512)


def nsa_mixer(x, h, w_in, c_qk_gain, c_cmp_pe, c_cmp_w1, c_cmp_w2, w_out, rel_table):
    s, d = h.shape
    scale = HEAD_DIM ** -0.5
    g = C_GROUPS
    n_in = w_in.shape[1]
    n_pad = -(-n_in // 640) * 640
    ones = lambda n: jnp.ones((n,), F32)
    zeros = lambda n: jnp.zeros((n,), F32)
    kvw = g * HEAD_DIM
    gain = jnp.concatenate([_tile_gain(c_qk_gain[0], C_HEADS, scale), ones(2 * kvw),
                            _tile_gain(c_qk_gain[2], g), ones(kvw), _tile_gain(c_qk_gain[3], g), ones(kvw),
                            ones(n_pad - C_HEADS * HEAD_DIM - 6 * kvw)])
    flag = jnp.concatenate([ones(C_HEADS * HEAD_DIM), zeros(2 * kvw), ones(kvw), zeros(kvw), ones(kvw),
                            zeros(kvw), zeros(n_pad - C_HEADS * HEAD_DIM - 6 * kvw)])
    w_pad = jnp.pad(w_in.astype(BF16), ((0, 0), (0, n_pad - n_in)))
    proj = proj_headmajor(h, w_pad, gain, flag, tn=640)
    kv0 = C_HEADS
    table_t = rel_table.astype(F32).T[:C_HEADS]

    nc = s // CMP_STRIDE
    cw = CMP_STRIDE * HEAD_DIM
    xc = proj[kv0:kv0 + 2 * g].reshape(2, g, nc, cw)
    pe = c_cmp_pe.astype(F32).reshape(2, 2, 1, cw)
    w1 = c_cmp_w1.astype(BF16).reshape(2, 2, cw, c_cmp_w1.shape[-1])
    kvcmp = compress_kv(xc, pe, w1, c_cmp_w2.astype(BF16), c_qk_gain[1])
    if nc % LANES:
        kvcmp = jnp.pad(kvcmp, ((0, 0), (0, 0), (0, LANES - nc % LANES), (0, 0)))
    ncp = kvcmp.shape[2]

    tq = min(256, s)
    n_cmp = (s - CMP_LEN) // CMP_STRIDE + 1
    n_sel = s // SEL_LEN
    nselp = -(-n_sel // LANES) * LANES
    n_top = min(SEL_TOPK, n_sel)
    off = CMP_STRIDE * (LANES - 1)
    far_tile = -(-(FAR + off + CMP_LEN - 1) // tq)
    nd = far_tile + 2
    wx = -(-(tq + off) // LANES) * LANES
    dist_c = np.arange(tq * (nd - 1) + wx) - (tq + off + CMP_LEN - 1)
    vec_c = _bias_by_distance(table_t, dist_c, dist_c >= 0)
    g_c = jnp.stack([vec_c[:, tq * d:tq * d + wx] for d in range(nd)], axis=1)
    bias_c = toeplitz_tiles(g_c, LANES, tq, CMP_STRIDE, off, transpose=True)
    c_start = np.arange(ncp) * CMP_STRIDE
    s_start = np.arange(nselp) * SEL_LEN
    ov = ((c_start[:, None] < s_start[None, :] + SEL_LEN) & (c_start[:, None] + CMP_LEN > s_start[None, :])
          & (np.arange(ncp)[:, None] < n_cmp) & (np.arange(nselp)[None, :] < n_sel))
    oc, sel = cmp_attention(proj, kvcmp, bias_c, jnp.asarray(ov.astype(np.float32), BF16), tq, n_top)

    t = tq
    nds = min(-(-(FAR - 1) // t) + 2, s // t)
    bias_s = causal_bias_tiles(table_t, t, nds)
    key_blk = np.arange(s) // SEL_LEN
    key_neg = np.where(key_blk[:, None] == np.arange(nselp)[None, :], NEG, 0.0).reshape(s // t, t, nselp)
    osel = flash_attention(
        proj, proj, proj, bias_s, name="selected_attention", t=t, n_groups=g, hb=C_HPG, nv=1,
        q_map=lambda gg, qb: (gg, qb, 0),
        k_map=lambda gg, kb: (kv0 + 2 * g + gg, kb, 0),
        v_map=lambda gg, kb: (kv0 + 3 * g + gg, kb, 0),
        b_map=lambda gg, bt: (gg, bt, 0, 0),
        sel=sel, key_neg=jnp.asarray(key_neg.astype(np.float32), BF16))

    look = -(-(C_WINDOW - 1) // t)
    ndw = min(look + 1, s // t)
    bias_w = causal_bias_tiles(table_t, t, ndw, window=C_WINDOW)
    ow = flash_attention(
        proj, proj, proj, bias_w, name="window_attention", t=t, n_groups=g, hb=C_HPG, nv=1, lookback=look,
        q_map=lambda gg, qb: (gg, qb, 0),
        k_map=lambda gg, kb: (kv0 + 4 * g + gg, kb, 0),
        v_map=lambda gg, kb: (kv0 + 5 * g + gg, kb, 0),
        b_map=lambda gg, bt: (gg, bt, 0, 0))

    o = gate_merge(oc, osel, ow, proj, kv0 + 6 * g)
    return matmul_residual(o, w_out, x, tm=1024, tn=512)


def conv_ffn(x, h, w_gate, w_up, conv_w, conv_b, w_down):
    act = ffn_gate_up(h, w_gate, w_up, conv_w, conv_b)
    return matmul_residual(act, w_down, x, tm=1024, tn=256)


def kernel(x, rel_table, ev_norm, ev_w_in, a_qk_gain, a_lambda, a_subln_gain, b_qk_gain, ev_w_out,
           od_norm, od_w_in, c_qk_gain, c_cmp_pe, c_cmp_w1, c_cmp_w2, od_w_out,
           ffn_norm, ffn_w_gate, ffn_w_up, ffn_conv_w, ffn_conv_b, ffn_w_down):
    b, s, d = x.shape
    depth = ffn_norm.shape[0]
    outs = []
    for bi in range(b):
        y = x[bi].astype(F32)
        for i in range(depth):
            if i % 2 == 0:
                e = i // 2
                lam_init = 0.8 - 0.6 * math.exp(-0.3 * i)
                y = even_mixer(y, rmsnorm(y, ev_norm[e]), ev_w_in[e], a_qk_gain[e], a_lambda[e],
                               a_subln_gain[e], b_qk_gain[e], ev_w_out[e], rel_table, lam_init)
            else:
                o = i // 2
                y = nsa_mixer(y, rmsnorm(y, od_norm[o]), od_w_in[o], c_qk_gain[o], c_cmp_pe[o],
                              c_cmp_w1[o], c_cmp_w2[o], od_w_out[o], rel_table)
            y = conv_ffn(y, rmsnorm(y, ffn_norm[i]), ffn_w_gate[i], ffn_w_up[i], ffn_conv_w[i],
                         ffn_conv_b[i], ffn_w_down[i])
        outs.append(y)
    return jnp.stack(outs).astype(x.dtype)
```

```python
import functools
import math

import numpy as np
import jax
import jax.numpy as jnp
from jax import lax
from jax.experimental import pallas as pl
from jax.experimental.pallas import tpu as pltpu

HEAD_DIM = 128
A_HEADS = 8
B_HEADS = 16
B_CONFIGS = ((128, 1), (512, 4), (2048, 16))
C_HEADS = 32
C_GROUPS = 2
C_HPG = C_HEADS // C_GROUPS
CMP_LEN = 32
CMP_STRIDE = 16
SEL_LEN = 64
SEL_TOPK = 16
C_WINDOW = 512
REL_BUCKETS = 32
REL_MAX_DIST = 2048
EPS = 1e-6
NEG = -1e30
FORCE = 1e9

LANES = 128
VMEM_LIMIT = 56 * 1024 * 1024

F32 = jnp.float32
BF16 = jnp.bfloat16


def _cparams(sem):
    return pltpu.CompilerParams(dimension_semantics=sem, vmem_limit_bytes=VMEM_LIMIT)


def _dot(a, b):
    return jnp.dot(a, b, preferred_element_type=F32)


def _dot_nt(a, b):
    return lax.dot_general(a, b, (((1,), (1,)), ((), ())), preferred_element_type=F32)


def _bucket_np(dist):
    n = np.maximum(dist, 0).astype(np.int32)
    max_exact = REL_BUCKETS // 2
    nf = np.maximum(n, 1).astype(np.float32)
    ratio = np.log(nf / np.float32(max_exact)) / np.float32(math.log(REL_MAX_DIST / max_exact))
    large = np.minimum(max_exact + (ratio * np.float32(REL_BUCKETS - max_exact)).astype(np.int32),
                       REL_BUCKETS - 1)
    return np.where(n < max_exact, n, large).astype(np.int32)


def _far_distance():
    b = _bucket_np(np.arange(4 * REL_MAX_DIST))
    return int(np.max(np.nonzero(b != REL_BUCKETS - 1)[0])) + 1


FAR = _far_distance()


def _bias_by_distance(table_t, dist, valid):
    b = jnp.take(table_t, jnp.asarray(_bucket_np(dist)), axis=1)
    return jnp.where(jnp.asarray(valid), b, NEG).astype(F32)


def _toeplitz_kernel(g_ref, o_ref, *, rows, width, shift, stride, transpose):
    g = g_ref[0, 0]
    x = jnp.broadcast_to(g, (rows, g.shape[1]))
    y = pltpu.roll(x, shift, 1, stride=stride, stride_axis=0)[:, :width]
    o_ref[0, 0] = y.T if transpose else y


def toeplitz_tiles(g, rows, width, stride, offset, transpose=False):
    h, nd, wx = g.shape
    assert wx % LANES == 0 and 0 <= offset - stride * (rows - 1) and width - 1 + offset < wx
    shape = (width, rows) if transpose else (rows, width)
    return pl.pallas_call(
        functools.partial(_toeplitz_kernel, rows=rows, width=width, shift=(wx - offset) % wx, stride=stride,
                          transpose=transpose),
        out_shape=jax.ShapeDtypeStruct((h, nd) + shape, F32),
        grid=(h, nd),
        in_specs=[pl.BlockSpec((1, 1, 1, wx), lambda a, b: (a, b, 0, 0))],
        out_specs=pl.BlockSpec((1, 1) + shape, lambda a, b: (a, b, 0, 0)),
        name="toeplitz_tiles",
        compiler_params=_cparams(("parallel", "parallel")),
    )(g.reshape(h, nd, 1, wx))


def causal_bias_tiles(table_t, t, n_tiles, window=None):
    span = (n_tiles + 1) * t
    dist = (n_tiles * t) - np.arange(span)
    valid = dist >= 0 if window is None else (dist >= 0) & (dist < window)
    vrev = _bias_by_distance(table_t, dist, valid)
    g = jnp.stack([vrev[:, (n_tiles - d - 1) * t:(n_tiles - d + 1) * t] for d in range(n_tiles)], axis=1)
    return toeplitz_tiles(g, t, t, 1, t)


def _rmsnorm_kernel(x_ref, g_ref, o_ref):
    x = x_ref[...]
    ms = jnp.mean(x * x, axis=-1, keepdims=True)
    o_ref[...] = ((x * lax.rsqrt(ms + EPS)) * g_ref[...]).astype(o_ref.dtype)


def rmsnorm(x, gain):
    s, d = x.shape
    tr = min(256, s)
    return pl.pallas_call(
        _rmsnorm_kernel,
        out_shape=jax.ShapeDtypeStruct((s, d), BF16),
        grid=(s // tr,),
        in_specs=[pl.BlockSpec((tr, d), lambda i: (i, 0)), pl.BlockSpec((1, d), lambda i: (0, 0))],
        out_specs=pl.BlockSpec((tr, d), lambda i: (i, 0)),
        name="rmsnorm",
        compiler_params=_cparams(("parallel",)),
    )(x, gain.reshape(1, d).astype(F32))


ROW_SUB = 256


def col_blocks(w, tn):
    return w.astype(BF16)[None]


def _proj_kernel(x_ref, w_ref, gain_ref, flag_ref, o_ref, *, nblk, ts):
    w = w_ref[0]
    for r0 in range(0, x_ref.shape[0], ts):
        acc = _dot(x_ref[r0:r0 + ts, :], w)
        for c in range(nblk):
            sl = slice(c * LANES, (c + 1) * LANES)
            blk = acc[:, sl]
            ms = jnp.mean(blk * blk, axis=-1, keepdims=True)
            r = jnp.where(flag_ref[:, sl] > 0, lax.rsqrt(ms + EPS), 1.0)
            o_ref[c, r0:r0 + ts, :] = ((blk * r) * gain_ref[:, sl]).astype(o_ref.dtype)


def proj_headmajor(x, w, gain, flag, tn):
    s, k = x.shape
    n = w.shape[1]
    tm = min(2048, s)
    nblk = tn // LANES
    return pl.pallas_call(
        functools.partial(_proj_kernel, nblk=nblk, ts=min(ROW_SUB, tm)),
        out_shape=jax.ShapeDtypeStruct((n // LANES, s, LANES), BF16),
        grid=(s // tm, n // tn),
        in_specs=[pl.BlockSpec((tm, k), lambda i, j: (i, 0)),
                  pl.BlockSpec((1, k, tn), lambda i, j: (0, 0, j)),
                  pl.BlockSpec((1, tn), lambda i, j: (0, j)),
                  pl.BlockSpec((1, tn), lambda i, j: (0, j))],
        out_specs=pl.BlockSpec((nblk, tm, LANES), lambda i, j: (j, i, 0)),
        name="proj_headmajor",
        compiler_params=_cparams(("parallel", "parallel")),
    )(x, col_blocks(w, tn), gain.reshape(1, n).astype(F32), flag.reshape(1, n).astype(F32))


def _mm_res_kernel(x_ref, w_ref, r_ref, o_ref):
    o_ref[...] = r_ref[...] + _dot(x_ref[...], w_ref[0])


def matmul_residual(x, w, res, tm, tn):
    s, k = x.shape
    n = w.shape[1]
    tm = min(tm, s)
    tn = min(tn, n)
    return pl.pallas_call(
        _mm_res_kernel,
        out_shape=jax.ShapeDtypeStruct((s, n), F32),
        grid=(s // tm, n // tn),
        in_specs=[pl.BlockSpec((tm, k), lambda i, j: (i, 0), pipeline_mode=pl.Buffered(1)),
                  pl.BlockSpec((1, k, tn), lambda i, j: (0, 0, j)),
                  pl.BlockSpec((tm, tn), lambda i, j: (i, j))],
        out_specs=pl.BlockSpec((tm, tn), lambda i, j: (i, j)),
        name="matmul_residual",
        compiler_params=_cparams(("parallel", "arbitrary")),
    )(x, col_blocks(w, tn), res)


def _mm2_res_kernel(x1_ref, x2_ref, w_ref, r_ref, o_ref, *, k1):
    acc = _dot(x1_ref[...], w_ref[0, 0:k1, :]) + _dot(x2_ref[...], w_ref[0, k1:, :])
    o_ref[...] = r_ref[...] + acc


def matmul2_residual(x1, x2, w, res, tm, tn):
    s, k1 = x1.shape
    k2 = x2.shape[1]
    n = w.shape[1]
    tm = min(tm, s)
    tn = min(tn, n)
    return pl.pallas_call(
        functools.partial(_mm2_res_kernel, k1=k1),
        out_shape=jax.ShapeDtypeStruct((s, n), F32),
        grid=(s // tm, n // tn),
        in_specs=[pl.BlockSpec((tm, k1), lambda i, j: (i, 0)),
                  pl.BlockSpec((tm, k2), lambda i, j: (i, 0)),
                  pl.BlockSpec((1, k1 + k2, tn), lambda i, j: (0, 0, j)),
                  pl.BlockSpec((tm, tn), lambda i, j: (i, j))],
        out_specs=pl.BlockSpec((tm, tn), lambda i, j: (i, j)),
        name="matmul2_residual",
        compiler_params=_cparams(("parallel", "arbitrary")),
    )(x1, x2, col_blocks(w, tn), res)


def _ffn1_kernel(x_ref, wg_ref, wu_ref, cw_ref, cb_ref, o_ref, carry_ref, *, ts):
    i = pl.program_id(0)
    j = pl.program_id(1)

    @pl.when(i == 0)
    def _():
        carry_ref[j] = jnp.zeros(carry_ref.shape[1:], F32)

    prev = carry_ref[j]
    wg = wg_ref[0]
    wu = wu_ref[0]
    cw = cw_ref[...]
    cb = cb_ref[...]
    row = lax.broadcasted_iota(jnp.int32, (ts, wg.shape[1]), 0)
    for r0 in range(0, x_ref.shape[0], ts):
        x = x_ref[r0:r0 + ts, :]
        g = _dot(x, wg)
        u = _dot(x, wu)
        p7 = prev[7:8, :]
        p6 = prev[6:7, :]
        g1 = jnp.where(row == 0, p7, pltpu.roll(g, 1, 0))
        g2 = jnp.where(row == 0, p6, jnp.where(row == 1, p7, pltpu.roll(g, 2, 0)))
        gc = g2 * cw[0:1, :] + g1 * cw[1:2, :] + g * cw[2:3, :] + cb
        o_ref[r0:r0 + ts, :] = ((gc * jax.nn.sigmoid(gc)) * u).astype(o_ref.dtype)
        prev = g[ts - 8:ts, :]
    carry_ref[j] = prev


def ffn_gate_up(h, wg, wu, conv_w, conv_b):
    s, d = h.shape
    f = wg.shape[1]
    tm = min(2048, s)
    tn = 256
    nj = f // tn
    return pl.pallas_call(
        functools.partial(_ffn1_kernel, ts=min(ROW_SUB, tm)),
        out_shape=jax.ShapeDtypeStruct((s, f), BF16),
        grid=(s // tm, nj),
        in_specs=[pl.BlockSpec((tm, d), lambda i, j: (i, 0)),
                  pl.BlockSpec((1, d, tn), lambda i, j: (0, 0, j)),
                  pl.BlockSpec((1, d, tn), lambda i, j: (0, 0, j)),
                  pl.BlockSpec((3, tn), lambda i, j: (0, j)),
                  pl.BlockSpec((1, tn), lambda i, j: (0, j))],
        out_specs=pl.BlockSpec((tm, tn), lambda i, j: (i, j)),
        scratch_shapes=[pltpu.VMEM((nj, 8, tn), F32)],
        name="ffn_gate_up",
        compiler_params=_cparams(("arbitrary", "arbitrary")),
    )(h, col_blocks(wg, tn), col_blocks(wu, tn), conv_w.astype(F32), conv_b.reshape(1, f).astype(F32))


def _flash_kernel(qi_ref, kj_ref, bi_ref, fl_ref, *refs, hb, nv, use_sel, rc, diff_lam_init):
    q_ref, k_ref, v_ref, b_ref = refs[:4]
    refs = refs[4:]
    if use_sel:
        sel_ref, en_ref = refs[:2]
        refs = refs[2:]
    if diff_lam_init is not None:
        lam_ref, gain_ref = refs[:2]
        refs = refs[2:]
    o_ref, m_sc, acc_sc, s_sc, p_sc, al_sc = refs[:6]
    if use_sel:
        qa_sc = refs[6]
    per_head_k = k_ref.shape[0] == hb and hb > 1
    per_head_bias = b_ref.shape[0] == hb
    p = pl.program_id(1)
    flags = fl_ref[p]
    dv = nv * LANES
    t = q_ref.shape[1]
    tk = k_ref.shape[1]

    @pl.when((flags & 1) != 0)
    def _():
        m_sc[...] = jnp.full(m_sc.shape, -jnp.inf, F32)
        acc_sc[...] = jnp.zeros(acc_sc.shape, F32)
        if use_sel:
            notsel = (1.0 - sel_ref[0].astype(F32)).astype(BF16)
            for h in range(hb):
                qa_sc[h, :, 0:LANES] = q_ref[h]
                qa_sc[h, :, LANES:] = notsel

    v_aug = jnp.concatenate([v_ref[c] for c in range(nv)] + [jnp.ones((tk, LANES), BF16)], axis=-1)

    for h in range(hb):
        k = k_ref[h if per_head_k else 0]
        if use_sel:
            k = jnp.concatenate([k, en_ref[0]], axis=-1)
        qh = qa_sc[h] if use_sel else q_ref[h]
        s_sc[h] = _dot_nt(qh, k)
        hbias = h if per_head_bias else 0
        for r0 in range(0, t, rc):
            rows = slice(r0, r0 + rc)
            s = s_sc[h, rows, :] + b_ref[hbias, 0, rows, :]
            m_old = m_sc[h, rows, :]
            m_new = jnp.maximum(m_old, jnp.max(s, axis=-1, keepdims=True))
            al_sc[h, rows, :] = jnp.exp(m_old - m_new)
            m_sc[h, rows, :] = m_new
            p_sc[h, rows, :] = jnp.exp(s - jnp.tile(m_new, (1, tk // LANES))).astype(BF16)
        acc_sc[h] = jnp.tile(al_sc[h], (1, nv + 1)) * acc_sc[h] + _dot(p_sc[h], v_aug)

    def normalised(h):
        acc = acc_sc[h]
        return acc[:, 0:dv] / jnp.tile(acc[:, dv:], (1, nv))

    @pl.when((flags & 2) != 0)
    def _():
        if diff_lam_init is None:
            for h in range(hb):
                o_ref[:, h * dv:(h + 1) * dv] = normalised(h).astype(o_ref.dtype)
        else:
            lf = lam_ref[...]
            s1 = jnp.sum(lf[0:1, :] * lf[1:2, :], axis=-1, keepdims=True)
            s2 = jnp.sum(lf[2:3, :] * lf[3:4, :], axis=-1, keepdims=True)
            lam = jnp.exp(s1) - jnp.exp(s2) + diff_lam_init
            o = normalised(0) - lam * normalised(1)
            ms = jnp.mean(o * o, axis=-1, keepdims=True)
            y = ((o * lax.rsqrt(ms + EPS)) * gain_ref[...]) * (1.0 - diff_lam_init)
            o_ref[...] = y.astype(o_ref.dtype)


def _pairs(nq, lookback, max_bias):
    qi, kj, bi, fl = [], [], [], []
    for q in range(nq):
        lo = 0 if lookback is None else max(0, q - lookback)
        for k in range(lo, q + 1):
            qi.append(q)
            kj.append(k)
            bi.append(min(q - k, max_bias))
            fl.append((1 if k == lo else 0) | (2 if k == q else 0))
    return [jnp.asarray(np.asarray(a, np.int32)) for a in (qi, kj, bi, fl)]


def flash_attention(q, k, v, bias, *, name, t, n_groups, hb, nv, q_map, k_map, v_map, b_map,
                    lookback=None, sel=None, key_neg=None, diff_params=None):
    s = q.shape[1]
    nq = s // t
    qi, kj, bi, fl = _pairs(nq, lookback, bias.shape[1] - 1)
    n_pairs = int(qi.shape[0])
    dv = nv * LANES
    use_sel = sel is not None
    diff = diff_params is not None
    in_specs = [
        pl.BlockSpec((hb, t, LANES), lambda g, p, qi, kj, bi, fl: q_map(g, qi[p])),
        pl.BlockSpec((hb if diff else 1, t, LANES), lambda g, p, qi, kj, bi, fl: k_map(g, kj[p])),
        pl.BlockSpec((nv, t, LANES), lambda g, p, qi, kj, bi, fl: v_map(g, kj[p])),
        pl.BlockSpec((1 if diff else hb, 1, t, t), lambda g, p, qi, kj, bi, fl: b_map(g, bi[p])),
    ]
    args = [q, k, v, bias]
    scratch = [pltpu.VMEM((hb, t, LANES), F32), pltpu.VMEM((hb, t, dv + LANES), F32),
               pltpu.VMEM((hb, t, t), F32), pltpu.VMEM((hb, t, t), BF16), pltpu.VMEM((hb, t, LANES), F32)]
    if use_sel:
        nselp = sel.shape[2]
        in_specs += [
            pl.BlockSpec((1, t, nselp), lambda g, p, qi, kj, bi, fl: (g, qi[p], 0)),
            pl.BlockSpec((1, t, nselp), lambda g, p, qi, kj, bi, fl: (kj[p], 0, 0)),
        ]
        args += [sel, key_neg]
        scratch.append(pltpu.VMEM((hb, t, LANES + nselp), BF16))
    lam_init = None
    if diff:
        a_lambda, subln_gain, lam_init = diff_params
        in_specs += [
            pl.BlockSpec((4, HEAD_DIM), lambda g, p, qi, kj, bi, fl: (0, 0)),
            pl.BlockSpec((1, dv), lambda g, p, qi, kj, bi, fl: (0, 0)),
        ]
        args += [a_lambda.astype(F32), subln_gain.reshape(1, dv).astype(F32)]
    out_w = dv if diff else hb * dv
    grid_spec = pltpu.PrefetchScalarGridSpec(
        num_scalar_prefetch=4,
        grid=(n_groups, n_pairs),
        in_specs=in_specs,
        out_specs=pl.BlockSpec((t, out_w), lambda g, p, qi, kj, bi, fl: (qi[p], g)),
        scratch_shapes=scratch,
    )
    rc = max(8, min(t, (16 * 8 * LANES) // t))
    return pl.pallas_call(
        functools.partial(_flash_kernel, hb=hb, nv=nv, use_sel=use_sel, rc=rc, diff_lam_init=lam_init),
        out_shape=jax.ShapeDtypeStruct((s, n_groups * out_w), BF16 if diff else F32),
        grid_spec=grid_spec,
        name=name,
        compiler_params=_cparams(("parallel", "arbitrary")),
    )(qi, kj, bi, fl, *args)


def _dilated_kernel(q_ref, kp_ref, kc_ref, vp_ref, vc_ref, b_ref, o_ref, lse_ref, s_sc, p_sc, m_sc):
    span = q_ref.shape[1]
    rc = min(64, span)
    no_prev = jnp.where(pl.program_id(1) == 0, NEG, 0.0)
    col = lax.broadcasted_iota(jnp.int32, (1, 2 * span), 1)
    prev_mask = jnp.where(col < span, no_prev, 0.0)
    ones = jnp.ones((2 * span, LANES), BF16)
    for h in range(B_HEADS):
        keys = jnp.concatenate([kp_ref[h], kc_ref[h]], axis=0)
        s_sc[h] = _dot_nt(q_ref[h], keys)
        for r0 in range(0, span, rc):
            rows = slice(r0, r0 + rc)
            s = s_sc[h, rows, :] + (b_ref[h, 0, rows, :] + prev_mask)
            m = jnp.max(s, axis=-1, keepdims=True)
            m_sc[h, rows, :] = jnp.broadcast_to(m, (rc, LANES))
            p_sc[h, rows, :] = jnp.exp(s - m).astype(BF16)
        vals = jnp.concatenate([jnp.concatenate([vp_ref[h], vc_ref[h]], axis=0), ones], axis=-1)
        pv = _dot(p_sc[h], vals)
        l = pv[:, HEAD_DIM:]
        sl = slice(h * HEAD_DIM, (h + 1) * HEAD_DIM)
        o_ref[:, sl] = pv[:, 0:HEAD_DIM] / l
        lse_ref[:, sl] = m_sc[h] + jnp.log(l)


def dilated_group(proj_hm, bias, dilation, span, q_blk, k_blk, v_blk):
    nh, s, _ = proj_hm.shape
    l = s // dilation
    nb = l // span
    x = proj_hm.reshape(nh, l, dilation * LANES)
    w = B_HEADS * HEAD_DIM
    hspec = lambda blk, prev: pl.BlockSpec(
        (B_HEADS, span, LANES),
        (lambda r, n: (blk, jnp.maximum(n - 1, 0), r)) if prev else (lambda r, n: (blk, n, r)))
    o, lse = pl.pallas_call(
        _dilated_kernel,
        out_shape=[jax.ShapeDtypeStruct((l, dilation * w), F32)] * 2,
        grid=(dilation, nb),
        in_specs=[hspec(q_blk, False), hspec(k_blk, True), hspec(k_blk, False),
                  hspec(v_blk, True), hspec(v_blk, False),
                  pl.BlockSpec((B_HEADS, 1, span, 2 * span), lambda r, n: (0, 0, 0, 0))],
        out_specs=[pl.BlockSpec((span, w), lambda r, n: (n, r))] * 2,
        scratch_shapes=[pltpu.VMEM((B_HEADS, span, 2 * span), F32),
                        pltpu.VMEM((B_HEADS, span, 2 * span), BF16),
                        pltpu.VMEM((B_HEADS, span, LANES), F32)],
        name="dilated_group",
        compiler_params=_cparams(("parallel", "parallel")),
    )(x, x, x, x, x, bias)
    return o.reshape(s, w), lse.reshape(s, w)


def _mix3_kernel(o1, o2, o3, l1, l2, l3, out_ref):
    a, b, c = l1[...], l2[...], l3[...]
    m = jnp.maximum(jnp.maximum(a, b), c)
    ea, eb, ec = jnp.exp(a - m), jnp.exp(b - m), jnp.exp(c - m)
    out_ref[...] = ((ea * o1[...] + eb * o2[...] + ec * o3[...]) / (ea + eb + ec)).astype(out_ref.dtype)


def mix_dilated(os, lses):
    s, w = os[0].shape
    ts = min(256, s)
    spec = pl.BlockSpec((ts, w), lambda i: (i, 0))
    return pl.pallas_call(
        _mix3_kernel,
        out_shape=jax.ShapeDtypeStruct((s, w), BF16),
        grid=(s // ts,),
        in_specs=[spec] * 6,
        out_specs=spec,
        name="mix_dilated",
        compiler_params=_cparams(("parallel",)),
    )(*os, *lses)


def _compress_kernel(x_ref, pe_ref, w1_ref, w2_ref, gain_ref, o_ref):
    kv = pl.program_id(0)
    x = x_ref[0, 0].astype(F32)
    nc = x.shape[0]
    a = _dot((x + pe_ref[0, 0]).astype(BF16), w1_ref[0, 0])
    b = _dot((x + pe_ref[0, 1]).astype(BF16), w1_ref[0, 1])
    hid = jax.nn.gelu(a + pltpu.roll(b, nc - 1, 0))
    c = _dot(hid.astype(BF16), w2_ref[0])
    ms = jnp.mean(c * c, axis=-1, keepdims=True)
    normed = (c * lax.rsqrt(ms + EPS)) * gain_ref[...]
    o_ref[0, 0] = jnp.where(kv == 0, normed, c).astype(o_ref.dtype)


def compress_kv(xc, pe, w1, w2, gain):
    _, g, nc, cw = xc.shape
    hid = w1.shape[-1]
    return pl.pallas_call(
        _compress_kernel,
        out_shape=jax.ShapeDtypeStruct((2, g, nc, HEAD_DIM), BF16),
        grid=(2, g),
        in_specs=[pl.BlockSpec((1, 1, nc, cw), lambda a, b: (a, b, 0, 0)),
                  pl.BlockSpec((1, 2, 1, cw), lambda a, b: (a, 0, 0, 0)),
                  pl.BlockSpec((1, 2, cw, hid), lambda a, b: (a, 0, 0, 0)),
                  pl.BlockSpec((1, hid, HEAD_DIM), lambda a, b: (a, 0, 0)),
                  pl.BlockSpec((1, HEAD_DIM), lambda a, b: (0, 0))],
        out_specs=pl.BlockSpec((1, 1, nc, HEAD_DIM), lambda a, b: (a, b, 0, 0)),
        name="compress_kv",
        compiler_params=_cparams(("parallel", "parallel")),
    )(xc, pe, w1, w2, gain.reshape(1, HEAD_DIM).astype(F32))


def _cmp_kernel(q_ref, k_ref, v_ref, *rest, nkt, n_top):
    bias_refs = rest[:nkt]
    ov_ref, oc_ref, sel_ref, imp_sc, s_sc, p_sc = rest[nkt:]
    qi = pl.program_id(1)
    tq = q_ref.shape[1]
    ncp = k_ref.shape[2]
    rc = max(8, min(tq, (16 * 8 * LANES) // ncp))
    k = k_ref[0, 0]
    v = v_ref[0, 0]
    for h in range(C_HPG):
        s_sc[h] = _dot_nt(q_ref[h], k)
        for r0 in range(0, tq, rc):
            rows = slice(r0, r0 + rc)
            if nkt == 1:
                bias = bias_refs[0][h, 0, rows, :]
            else:
                bias = jnp.concatenate([b[h, 0, rows, :] for b in bias_refs], axis=-1)
            s = s_sc[h, rows, :] + bias
            m = jnp.max(s, axis=-1, keepdims=True)
            e = jnp.exp(s - m)
            pc = e * jnp.where(m > 0.5 * NEG, 1.0 / jnp.sum(e, axis=-1, keepdims=True), 0.0)
            p_sc[h, rows, :] = pc.astype(BF16)
            if h == 0:
                imp_sc[rows, :] = pc
            else:
                imp_sc[rows, :] += pc
        oc_ref[:, h * HEAD_DIM:(h + 1) * HEAD_DIM] = _dot(p_sc[h], v)
    imp = imp_sc[...]
    hi = imp.astype(BF16)
    lo = (imp - hi.astype(F32)).astype(BF16)
    ov = ov_ref[...]
    impn = _dot(hi, ov) + _dot(lo, ov)
    score_in = impn.T
    shape = score_in.shape
    t = qi * tq + lax.broadcasted_iota(jnp.int32, shape, 1)
    n = lax.broadcasted_iota(jnp.int32, shape, 0)
    cur = lax.shift_right_logical(t, int(math.log2(SEL_LEN)))
    forced = (n == 0) | (n == cur) | (n == cur - 1)
    valid = n <= cur
    score = jnp.where(valid, jnp.where(forced, FORCE, score_in), NEG)
    nf = n.astype(F32)
    sel = jnp.zeros(shape, F32)
    for _ in range(n_top):
        m = jnp.max(score, axis=0, keepdims=True)
        first = jnp.min(jnp.where(score == m, nf, 1e9), axis=0, keepdims=True)
        pick = nf == first
        sel = jnp.where(pick, 1.0, sel)
        score = jnp.where(pick, -jnp.inf, score)
    sel_ref[0] = jnp.where(valid, sel, 0.0).T.astype(sel_ref.dtype)


def cmp_attention(q_hm, kvcmp, bias, overlap, tq, n_top):
    s = q_hm.shape[1]
    ncp = kvcmp.shape[2]
    nkt = ncp // LANES
    nselp = overlap.shape[1]
    nd = bias.shape[1]
    per_tile = (LANES * CMP_STRIDE) // tq

    def bias_spec(kt):
        return pl.BlockSpec(
            (C_HPG, 1, tq, LANES),
            lambda g, i: (g, jnp.clip(i - per_tile * kt, -1, nd - 2) + 1, 0, 0))

    w = C_HEADS * HEAD_DIM
    return pl.pallas_call(
        functools.partial(_cmp_kernel, nkt=nkt, n_top=n_top),
        out_shape=[jax.ShapeDtypeStruct((s, w), F32),
                   jax.ShapeDtypeStruct((C_GROUPS, s, nselp), BF16)],
        grid=(C_GROUPS, s // tq),
        in_specs=[pl.BlockSpec((C_HPG, tq, LANES), lambda g, i: (g, i, 0)),
                  pl.BlockSpec((1, 1, ncp, LANES), lambda g, i: (0, g, 0, 0)),
                  pl.BlockSpec((1, 1, ncp, LANES), lambda g, i: (1, g, 0, 0))]
                 + [bias_spec(kt) for kt in range(nkt)]
                 + [pl.BlockSpec((ncp, nselp), lambda g, i: (0, 0))],
        out_specs=[pl.BlockSpec((tq, w // C_GROUPS), lambda g, i: (i, g)),
                   pl.BlockSpec((1, tq, nselp), lambda g, i: (g, i, 0))],
        scratch_shapes=[pltpu.VMEM((tq, ncp), F32), pltpu.VMEM((C_HPG, tq, ncp), F32),
                        pltpu.VMEM((C_HPG, tq, ncp), BF16)],
        name="cmp_attention",
        compiler_params=_cparams(("parallel", "parallel")),
    )(q_hm, kvcmp, kvcmp, *([bias] * nkt), overlap)


def _gate_kernel(oc_ref, os_ref, ow_ref, gt_ref, o_ref):
    gt = jax.nn.sigmoid(gt_ref[0].astype(F32))
    for h in range(C_HEADS):
        sl = slice(h * HEAD_DIM, (h + 1) * HEAD_DIM)
        g0 = gt[:, h:h + 1]
        g1 = gt[:, C_HEADS + h:C_HEADS + h + 1]
        g2 = gt[:, 2 * C_HEADS + h:2 * C_HEADS + h + 1]
        o_ref[:, sl] = (g0 * oc_ref[:, sl] + g1 * os_ref[:, sl] + g2 * ow_ref[:, sl]).astype(o_ref.dtype)


def gate_merge(oc, osel, ow, proj_hm, gate_blk):
    s, w = oc.shape
    ts = min(256, s)
    spec = pl.BlockSpec((ts, w), lambda i: (i, 0))
    return pl.pallas_call(
        _gate_kernel,
        out_shape=jax.ShapeDtypeStruct((s, w), BF16),
        grid=(s // ts,),
        in_specs=[spec, spec, spec, pl.BlockSpec((1, ts, LANES), lambda i: (gate_blk, i, 0))],
        out_specs=spec,
        name="gate_merge",
        compiler_params=_cparams(("parallel",)),
    )(oc, osel, ow, proj_hm)


def _tile_gain(g, reps, scale=1.0):
    return jnp.tile(g.astype(F32) * scale, reps)


def even_mixer(x, h, w_in, a_qk_gain, a_lambda, a_subln_gain, b_qk_gain, w_out, rel_table, lam_init):
    s = h.shape[0]
    scale = HEAD_DIM ** -0.5
    na = 2 * A_HEADS
    ones = lambda n: jnp.ones((n * HEAD_DIM,), F32)
    zeros = lambda n: jnp.zeros((n * HEAD_DIM,), F32)
    d = h.shape[1]
    wa = na * HEAD_DIM
    table_t = rel_table.astype(F32).T

    def pair_cols(w):
        return w.reshape(d, 2, A_HEADS, HEAD_DIM).transpose(0, 2, 1, 3).reshape(d, wa)

    w_a = jnp.concatenate([pair_cols(w_in[:, 0:wa]), pair_cols(w_in[:, wa:2 * wa]), w_in[:, 2 * wa:3 * wa]],
                          axis=1).astype(BF16)
    gain_a = jnp.concatenate([_tile_gain(a_qk_gain[0], na, scale), _tile_gain(a_qk_gain[1], na), ones(na)])
    flag_a = jnp.concatenate([ones(2 * na), zeros(na)])
    proj_a = proj_headmajor(h, w_a, gain_a, flag_a, tn=512)
    t = min(1024, s)
    nda = min(-(-(FAR - 1) // t) + 2, s // t)
    bias_a = causal_bias_tiles(table_t[:A_HEADS], t, nda)
    ao = flash_attention(
        proj_a, proj_a, proj_a, bias_a, name="diff_attention", t=t, n_groups=A_HEADS, hb=2, nv=2,
        q_map=lambda g, qb: (g, qb, 0),
        k_map=lambda g, kb: (A_HEADS + g, kb, 0),
        v_map=lambda g, kb: (2 * A_HEADS + g, kb, 0),
        b_map=lambda g, bt: (g, bt, 0, 0),
        diff_params=(a_lambda, a_subln_gain, lam_init))

    gain_b = jnp.concatenate([_tile_gain(b_qk_gain[0], B_HEADS, scale), _tile_gain(b_qk_gain[1], B_HEADS),
                              ones(B_HEADS)])
    flag_b = jnp.concatenate([ones(2 * B_HEADS), zeros(B_HEADS)])
    proj = proj_headmajor(h, w_in[:, 3 * wa:], gain_b, flag_b, tn=512)
    hb0 = 0
    os, lses = [], []
    for window, dilation in B_CONFIGS:
        span = window // dilation
        j = 2 * span - np.arange(3 * span)
        g_b = _bias_by_distance(table_t[A_HEADS:A_HEADS + B_HEADS], j * dilation, (j >= 0) & (j <= span))
        bias_b = toeplitz_tiles(g_b[:, None, :], span, 2 * span, 1, span)
        o, lse = dilated_group(proj, bias_b, dilation, span, hb0, hb0 + 1, hb0 + 2)
        os.append(o)
        lses.append(lse)
    bo = mix_dilated(os, lses)
    return matmul2_residual(ao, bo, w_out, x, tm=1024, tn=512)


def nsa_mixer(x, h, w_in, c_qk_gain, c_cmp_pe, c_cmp_w1, c_cmp_w2, w_out, rel_table):
    s, d = h.shape
    scale = HEAD_DIM ** -0.5
    g = C_GROUPS
    n_in = w_in.shape[1]
    n_pad = -(-n_in // 640) * 640
    ones = lambda n: jnp.ones((n,), F32)
    zeros = lambda n: jnp.zeros((n,), F32)
    kvw = g * HEAD_DIM
    gain = jnp.concatenate([_tile_gain(c_qk_gain[0], C_HEADS, scale), ones(2 * kvw),
                            _tile_gain(c_qk_gain[2], g), ones(kvw), _tile_gain(c_qk_gain[3], g), ones(kvw),
                            ones(n_pad - C_HEADS * HEAD_DIM - 6 * kvw)])
    flag = jnp.concatenate([ones(C_HEADS * HEAD_DIM), zeros(2 * kvw), ones(kvw), zeros(kvw), ones(kvw),
                            zeros(kvw), zeros(n_pad - C_HEADS * HEAD_DIM - 6 * kvw)])
    w_pad = jnp.pad(w_in.astype(BF16), ((0, 0), (0, n_pad - n_in)))
    proj = proj_headmajor(h, w_pad, gain, flag, tn=640)
    kv0 = C_HEADS
    table_t = rel_table.astype(F32).T[:C_HEADS]

    nc = s // CMP_STRIDE
    cw = CMP_STRIDE * HEAD_DIM
    xc = proj[kv0:kv0 + 2 * g].reshape(2, g, nc, cw)
    pe = c_cmp_pe.astype(F32).reshape(2, 2, 1, cw)
    w1 = c_cmp_w1.astype(BF16).reshape(2, 2, cw, c_cmp_w1.shape[-1])
    kvcmp = compress_kv(xc, pe, w1, c_cmp_w2.astype(BF16), c_qk_gain[1])
    if nc % LANES:
        kvcmp = jnp.pad(kvcmp, ((0, 0), (0, 0), (0, LANES - nc % LANES), (0, 0)))
    ncp = kvcmp.shape[2]

    tq = min(256, s)
    n_cmp = (s - CMP_LEN) // CMP_STRIDE + 1
    n_sel = s // SEL_LEN
    nselp = -(-n_sel // LANES) * LANES
    n_top = min(SEL_TOPK, n_sel)
    off = CMP_STRIDE * (LANES - 1)
    far_tile = -(-(FAR + off + CMP_LEN - 1) // tq)
    nd = far_tile + 2
    wx = -(-(tq + off) // LANES) * LANES
    dist_c = np.arange(tq * (nd - 1) + wx) - (tq + off + CMP_LEN - 1)
    vec_c = _bias_by_distance(table_t, dist_c, dist_c >= 0)
    g_c = jnp.stack([vec_c[:, tq * d:tq * d + wx] for d in range(nd)], axis=1)
    bias_c = toeplitz_tiles(g_c, LANES, tq, CMP_STRIDE, off, transpose=True)
    c_start = np.arange(ncp) * CMP_STRIDE
    s_start = np.arange(nselp) * SEL_LEN
    ov = ((c_start[:, None] < s_start[None, :] + SEL_LEN) & (c_start[:, None] + CMP_LEN > s_start[None, :])
          & (np.arange(ncp)[:, None] < n_cmp) & (np.arange(nselp)[None, :] < n_sel))
    oc, sel = cmp_attention(proj, kvcmp, bias_c, jnp.asarray(ov.astype(np.float32), BF16), tq, n_top)

    t = tq
    nds = min(-(-(FAR - 1) // t) + 2, s // t)
    bias_s = causal_bias_tiles(table_t, t, nds)
    key_blk = np.arange(s) // SEL_LEN
    key_neg = np.where(key_blk[:, None] == np.arange(nselp)[None, :], NEG, 0.0).reshape(s // t, t, nselp)
    osel = flash_attention(
        proj, proj, proj, bias_s, name="selected_attention", t=t, n_groups=g, hb=C_HPG, nv=1,
        q_map=lambda gg, qb: (gg, qb, 0),
        k_map=lambda gg, kb: (kv0 + 2 * g + gg, kb, 0),
        v_map=lambda gg, kb: (kv0 + 3 * g + gg, kb, 0),
        b_map=lambda gg, bt: (gg, bt, 0, 0),
        sel=sel, key_neg=jnp.asarray(key_neg.astype(np.float32), BF16))

    look = -(-(C_WINDOW - 1) // t)
    ndw = min(look + 1, s // t)
    bias_w = causal_bias_tiles(table_t, t, ndw, window=C_WINDOW)
    ow = flash_attention(
        proj, proj, proj, bias_w, name="window_attention", t=t, n_groups=g, hb=C_HPG, nv=1, lookback=look,
        q_map=lambda gg, qb: (gg, qb, 0),
        k_map=lambda gg, kb: (kv0 + 4 * g + gg, kb, 0),
        v_map=lambda gg, kb: (kv0 + 5 * g + gg, kb, 0),
        b_map=lambda gg, bt: (gg, bt, 0, 0))

    o = gate_merge(oc, osel, ow, proj, kv0 + 6 * g)
    return matmul_residual(o, w_out, x, tm=1024, tn=512)


def conv_ffn(x, h, w_gate, w_up, conv_w, conv_b, w_down):
    act = ffn_gate_up(h, w_gate, w_up, conv_w, conv_b)
    return matmul_residual(act, w_down, x, tm=1024, tn=256)


def kernel(x, rel_table, ev_norm, ev_w_in, a_qk_gain, a_lambda, a_subln_gain, b_qk_gain, ev_w_out,
           od_norm, od_w_in, c_qk_gain, c_cmp_pe, c_cmp_w1, c_cmp_w2, od_w_out,
           ffn_norm, ffn_w_gate, ffn_w_up, ffn_conv_w, ffn_conv_b, ffn_w_down):
    b, s, d = x.shape
    depth = ffn_norm.shape[0]
    outs = []
    for bi in range(b):
        y = x[bi].astype(F32)
        for i in range(depth):
            if i % 2 == 0:
                e = i // 2
                lam_init = 0.8 - 0.6 * math.exp(-0.3 * i)
                y = even_mixer(y, rmsnorm(y, ev_norm[e]), ev_w_in[e], a_qk_gain[e], a_lambda[e],
                               a_subln_gain[e], b_qk_gain[e], ev_w_out[e], rel_table, lam_init)
            else:
                o = i // 2
                y = nsa_mixer(y, rmsnorm(y, od_norm[o]), od_w_in[o], c_qk_gain[o], c_cmp_pe[o],
                              c_cmp_w1[o], c_cmp_w2[o], od_w_out[o], rel_table)
            y = conv_ffn(y, rmsnorm(y, ffn_norm[i]), ffn_w_gate[i], ffn_w_up[i], ffn_conv_w[i],
                         ffn_conv_b[i], ffn_w_down[i])
        outs.append(y)
    return jnp.stack(outs).astype(x.dtype)
```

```python
import functools
import math

import numpy as np
import jax
import jax.numpy as jnp
from jax import lax
from jax.experimental import pallas as pl
from jax.experimental.pallas import tpu as pltpu

HEAD_DIM = 128
A_HEADS = 8
B_HEADS = 16
B_CONFIGS = ((128, 1), (512, 4), (2048, 16))
C_HEADS = 32
C_GROUPS = 2
C_HPG = C_HEADS // C_GROUPS
CMP_LEN = 32
CMP_STRIDE = 16
SEL_LEN = 64
SEL_TOPK = 16
C_WINDOW = 512
REL_BUCKETS = 32
REL_MAX_DIST = 2048
EPS = 1e-6
NEG = -1e30
FORCE = 1e9

LANES = 128
VMEM_LIMIT = 56 * 1024 * 1024

F32 = jnp.float32
BF16 = jnp.bfloat16


def _cparams(sem):
    return pltpu.CompilerParams(dimension_semantics=sem, vmem_limit_bytes=VMEM_LIMIT)


def _dot(a, b):
    return jnp.dot(a, b, preferred_element_type=F32)


def _dot_nt(a, b):
    return lax.dot_general(a, b, (((1,), (1,)), ((), ())), preferred_element_type=F32)


def _bucket_np(dist):
    n = np.maximum(dist, 0).astype(np.int32)
    max_exact = REL_BUCKETS // 2
    nf = np.maximum(n, 1).astype(np.float32)
    ratio = np.log(nf / np.float32(max_exact)) / np.float32(math.log(REL_MAX_DIST / max_exact))
    large = np.minimum(max_exact + (ratio * np.float32(REL_BUCKETS - max_exact)).astype(np.int32),
                       REL_BUCKETS - 1)
    return np.where(n < max_exact, n, large).astype(np.int32)


def _far_distance():
    b = _bucket_np(np.arange(4 * REL_MAX_DIST))
    return int(np.max(np.nonzero(b != REL_BUCKETS - 1)[0])) + 1


FAR = _far_distance()


def _bias_by_distance(table_t, dist, valid):
    b = jnp.take(table_t, jnp.asarray(_bucket_np(dist)), axis=1)
    return jnp.where(jnp.asarray(valid), b, NEG).astype(F32)


def _toeplitz_kernel(g_ref, o_ref, *, rows, width, shift, stride, transpose):
    g = g_ref[0, 0]
    wx = g.shape[1]
    offset = (wx - shift) % wx
    if stride == 1 and not transpose and rows % LANES == 0 and width % LANES == 0 and offset % LANES == 0:
        nr, ncb = rows // LANES, width // LANES
        subs = {}
        for db in range(-(nr - 1), ncb):
            lo = LANES * db + offset - LANES
            gw = jnp.broadcast_to(g[:, lo:lo + 2 * LANES], (LANES, 2 * LANES))
            subs[db] = pltpu.roll(gw, LANES, 1, stride=1, stride_axis=0)[:, :LANES]
        for bi in range(nr):
            for bj in range(ncb):
                o_ref[0, 0, bi * LANES:(bi + 1) * LANES, bj * LANES:(bj + 1) * LANES] = subs[bj - bi]
        return
    x = jnp.broadcast_to(g, (rows, wx))
    y = pltpu.roll(x, shift, 1, stride=stride, stride_axis=0)[:, :width]
    o_ref[0, 0] = y.T if transpose else y


def toeplitz_tiles(g, rows, width, stride, offset, transpose=False):
    h, nd, wx = g.shape
    assert wx % LANES == 0 and 0 <= offset - stride * (rows - 1) and width - 1 + offset < wx
    shape = (width, rows) if transpose else (rows, width)
    return pl.pallas_call(
        functools.partial(_toeplitz_kernel, rows=rows, width=width, shift=(wx - offset) % wx, stride=stride,
                          transpose=transpose),
        out_shape=jax.ShapeDtypeStruct((h, nd) + shape, F32),
        grid=(h, nd),
        in_specs=[pl.BlockSpec((1, 1, 1, wx), lambda a, b: (a, b, 0, 0))],
        out_specs=pl.BlockSpec((1, 1) + shape, lambda a, b: (a, b, 0, 0)),
        name="toeplitz_tiles",
        compiler_params=_cparams(("parallel", "parallel")),
    )(g.reshape(h, nd, 1, wx))


def causal_bias_tiles(table_t, t, n_tiles, window=None):
    span = (n_tiles + 1) * t
    dist = (n_tiles * t) - np.arange(span)
    valid = dist >= 0 if window is None else (dist >= 0) & (dist < window)
    vrev = _bias_by_distance(table_t, dist, valid)
    g = jnp.stack([vrev[:, (n_tiles - d - 1) * t:(n_tiles - d + 1) * t] for d in range(n_tiles)], axis=1)
    return toeplitz_tiles(g, t, t, 1, t)


def _rmsnorm_kernel(x_ref, g_ref, o_ref):
    x = x_ref[...]
    ms = jnp.mean(x * x, axis=-1, keepdims=True)
    o_ref[...] = ((x * lax.rsqrt(ms + EPS)) * g_ref[...]).astype(o_ref.dtype)


def rmsnorm(x, gain):
    s, d = x.shape
    tr = min(256, s)
    return pl.pallas_call(
        _rmsnorm_kernel,
        out_shape=jax.ShapeDtypeStruct((s, d), BF16),
        grid=(s // tr,),
        in_specs=[pl.BlockSpec((tr, d), lambda i: (i, 0)), pl.BlockSpec((1, d), lambda i: (0, 0))],
        out_specs=pl.BlockSpec((tr, d), lambda i: (i, 0)),
        name="rmsnorm",
        compiler_params=_cparams(("parallel",)),
    )(x, gain.reshape(1, d).astype(F32))


ROW_SUB = 256


def col_blocks(w, tn):
    return w.astype(BF16)[None]


def _proj_kernel(x_ref, w_ref, gain_ref, flag_ref, o_ref, *, nblk, ts):
    w = w_ref[0].astype(BF16)
    for r0 in range(0, x_ref.shape[0], ts):
        acc = _dot(x_ref[r0:r0 + ts, :], w)
        for c in range(nblk):
            sl = slice(c * LANES, (c + 1) * LANES)
            blk = acc[:, sl]
            ms = jnp.mean(blk * blk, axis=-1, keepdims=True)
            r = jnp.where(flag_ref[:, sl] > 0, lax.rsqrt(ms + EPS), 1.0)
            o_ref[c, r0:r0 + ts, :] = ((blk * r) * gain_ref[:, sl]).astype(o_ref.dtype)


def proj_headmajor(x, w, gain, flag, tn, col0=0):
    s, k = x.shape
    n = gain.shape[0]
    tm = min(2048 if w.dtype == BF16 else 1024, s)
    nblk = tn // LANES
    cb0 = col0 // tn
    return pl.pallas_call(
        functools.partial(_proj_kernel, nblk=nblk, ts=min(ROW_SUB, tm)),
        out_shape=jax.ShapeDtypeStruct((n // LANES, s, LANES), BF16),
        grid=(s // tm, n // tn),
        in_specs=[pl.BlockSpec((tm, k), lambda i, j: (i, 0)),
                  pl.BlockSpec((1, k, tn), lambda i, j: (0, 0, cb0 + j)),
                  pl.BlockSpec((1, tn), lambda i, j: (0, j)),
                  pl.BlockSpec((1, tn), lambda i, j: (0, j))],
        out_specs=pl.BlockSpec((nblk, tm, LANES), lambda i, j: (j, i, 0)),
        name="proj_headmajor",
        compiler_params=_cparams(("parallel", "parallel")),
    )(x, w[None], gain.reshape(1, n).astype(F32), flag.reshape(1, n).astype(F32))


def _mm_res_kernel(x_ref, w_ref, r_ref, o_ref):
    o_ref[...] = r_ref[...] + _dot(x_ref[...], w_ref[0])


def matmul_residual(x, w, res, tm, tn):
    s, k = x.shape
    n = w.shape[1]
    tm = min(tm, s)
    tn = min(tn, n)
    return pl.pallas_call(
        _mm_res_kernel,
        out_shape=jax.ShapeDtypeStruct((s, n), F32),
        grid=(s // tm, n // tn),
        in_specs=[pl.BlockSpec((tm, k), lambda i, j: (i, 0), pipeline_mode=pl.Buffered(1)),
                  pl.BlockSpec((1, k, tn), lambda i, j: (0, 0, j)),
                  pl.BlockSpec((tm, tn), lambda i, j: (i, j))],
        out_specs=pl.BlockSpec((tm, tn), lambda i, j: (i, j)),
        name="matmul_residual",
        compiler_params=_cparams(("parallel", "arbitrary")),
    )(x, col_blocks(w, tn), res)


def _mm2_res_kernel(x1_ref, x2_ref, w_ref, r_ref, o_ref, *, k1):
    acc = _dot(x1_ref[...], w_ref[0, 0:k1, :]) + _dot(x2_ref[...], w_ref[0, k1:, :])
    o_ref[...] = r_ref[...] + acc


def matmul2_residual(x1, x2, w, res, tm, tn):
    s, k1 = x1.shape
    k2 = x2.shape[1]
    n = w.shape[1]
    tm = min(tm, s)
    tn = min(tn, n)
    return pl.pallas_call(
        functools.partial(_mm2_res_kernel, k1=k1),
        out_shape=jax.ShapeDtypeStruct((s, n), F32),
        grid=(s // tm, n // tn),
        in_specs=[pl.BlockSpec((tm, k1), lambda i, j: (i, 0)),
                  pl.BlockSpec((tm, k2), lambda i, j: (i, 0)),
                  pl.BlockSpec((1, k1 + k2, tn), lambda i, j: (0, 0, j)),
                  pl.BlockSpec((tm, tn), lambda i, j: (i, j))],
        out_specs=pl.BlockSpec((tm, tn), lambda i, j: (i, j)),
        name="matmul2_residual",
        compiler_params=_cparams(("parallel", "arbitrary")),
    )(x1, x2, col_blocks(w, tn), res)


def _ffn1_kernel(x_ref, wg_ref, wu_ref, cw_ref, cb_ref, o_ref, carry_ref, *, ts):
    i = pl.program_id(0)
    j = pl.program_id(1)

    @pl.when(i == 0)
    def _():
        carry_ref[j] = jnp.zeros(carry_ref.shape[1:], F32)

    prev = carry_ref[j]
    wg = wg_ref[0].astype(BF16)
    wu = wu_ref[0].astype(BF16)
    cw = cw_ref[...]
    cb = cb_ref[...]
    row = lax.broadcasted_iota(jnp.int32, (ts, wg.shape[1]), 0)
    for r0 in range(0, x_ref.shape[0], ts):
        x = x_ref[r0:r0 + ts, :]
        g = _dot(x, wg)
        u = _dot(x, wu)
        p7 = prev[7:8, :]
        p6 = prev[6:7, :]
        g1 = jnp.where(row == 0, p7, pltpu.roll(g, 1, 0))
        g2 = jnp.where(row == 0, p6, jnp.where(row == 1, p7, pltpu.roll(g, 2, 0)))
        gc = g2 * cw[0:1, :] + g1 * cw[1:2, :] + g * cw[2:3, :] + cb
        o_ref[r0:r0 + ts, :] = ((gc * jax.nn.sigmoid(gc)) * u).astype(o_ref.dtype)
        prev = g[ts - 8:ts, :]
    carry_ref[j] = prev


def ffn_gate_up(h, wg, wu, conv_w, conv_b):
    s, d = h.shape
    f = wg.shape[1]
    tm = min(2048, s)
    tn = 256
    nj = f // tn
    return pl.pallas_call(
        functools.partial(_ffn1_kernel, ts=min(ROW_SUB, tm)),
        out_shape=jax.ShapeDtypeStruct((s, f), BF16),
        grid=(s // tm, nj),
        in_specs=[pl.BlockSpec((tm, d), lambda i, j: (i, 0), pipeline_mode=pl.Buffered(1)),
                  pl.BlockSpec((1, d, tn), lambda i, j: (0, 0, j)),
                  pl.BlockSpec((1, d, tn), lambda i, j: (0, 0, j)),
                  pl.BlockSpec((3, tn), lambda i, j: (0, j)),
                  pl.BlockSpec((1, tn), lambda i, j: (0, j))],
        out_specs=pl.BlockSpec((tm, tn), lambda i, j: (i, j)),
        scratch_shapes=[pltpu.VMEM((nj, 8, tn), F32)],
        name="ffn_gate_up",
        compiler_params=_cparams(("arbitrary", "arbitrary")),
    )(h, wg[None], wu[None], conv_w.astype(F32), conv_b.reshape(1, f).astype(F32))


def _flash_kernel(qi_ref, kj_ref, bi_ref, fl_ref, *refs, hb, nv, use_sel, rc, diff_lam_init):
    q_ref, k_ref, v_ref, b_ref = refs[:4]
    refs = refs[4:]
    if use_sel:
        sel_ref, en_ref = refs[:2]
        refs = refs[2:]
    if diff_lam_init is not None:
        lam_ref, gain_ref = refs[:2]
        refs = refs[2:]
    o_ref, m_sc, acc_sc, s_sc, p_sc, al_sc = refs[:6]
    if use_sel:
        qa_sc = refs[6]
    per_head_k = k_ref.shape[0] == hb and hb > 1
    per_head_bias = b_ref.shape[0] == hb
    p = pl.program_id(1)
    flags = fl_ref[p]
    dv = nv * LANES
    t = q_ref.shape[1]
    tk = k_ref.shape[1]

    @pl.when((flags & 1) != 0)
    def _():
        m_sc[...] = jnp.full(m_sc.shape, -jnp.inf, F32)
        acc_sc[...] = jnp.zeros(acc_sc.shape, F32)
        if use_sel:
            notsel = (1.0 - sel_ref[0].astype(F32)).astype(BF16)
            for h in range(hb):
                qa_sc[h, :, 0:LANES] = q_ref[h]
                qa_sc[h, :, LANES:] = notsel

    v_aug = jnp.concatenate([v_ref[c] for c in range(nv)] + [jnp.ones((tk, LANES), BF16)], axis=-1)

    for h in range(hb):
        k = k_ref[h if per_head_k else 0]
        if use_sel:
            k = jnp.concatenate([k, en_ref[0]], axis=-1)
        qh = qa_sc[h] if use_sel else q_ref[h]
        s_sc[h] = _dot_nt(qh, k)
        hbias = h if per_head_bias else 0
        for r0 in range(0, t, rc):
            rows = slice(r0, r0 + rc)
            s = s_sc[h, rows, :] + b_ref[hbias, 0, rows, :]
            m_old = m_sc[h, rows, :]
            m_new = jnp.maximum(m_old, jnp.max(s, axis=-1, keepdims=True))
            al_sc[h, rows, :] = jnp.exp(m_old - m_new)
            m_sc[h, rows, :] = m_new
            p_sc[h, rows, :] = jnp.exp(s - jnp.tile(m_new, (1, tk // LANES))).astype(BF16)
        acc_sc[h] = jnp.tile(al_sc[h], (1, nv + 1)) * acc_sc[h] + _dot(p_sc[h], v_aug)

    def normalised(h):
        acc = acc_sc[h]
        return acc[:, 0:dv] / jnp.tile(acc[:, dv:], (1, nv))

    @pl.when((flags & 2) != 0)
    def _():
        if diff_lam_init is None:
            for h in range(hb):
                o_ref[:, h * dv:(h + 1) * dv] = normalised(h).astype(o_ref.dtype)
        else:
            lf = lam_ref[...]
            s1 = jnp.sum(lf[0:1, :] * lf[1:2, :], axis=-1, keepdims=True)
            s2 = jnp.sum(lf[2:3, :] * lf[3:4, :], axis=-1, keepdims=True)
            lam = jnp.exp(s1) - jnp.exp(s2) + diff_lam_init
            o = normalised(0) - lam * normalised(1)
            ms = jnp.mean(o * o, axis=-1, keepdims=True)
            y = ((o * lax.rsqrt(ms + EPS)) * gain_ref[...]) * (1.0 - diff_lam_init)
            o_ref[...] = y.astype(o_ref.dtype)


def _pairs(nq, lookback, max_bias):
    qi, kj, bi, fl = [], [], [], []
    for q in range(nq):
        lo = 0 if lookback is None else max(0, q - lookback)
        for k in range(lo, q + 1):
            qi.append(q)
            kj.append(k)
            bi.append(min(q - k, max_bias))
            fl.append((1 if k == lo else 0) | (2 if k == q else 0))
    return [jnp.asarray(np.asarray(a, np.int32)) for a in (qi, kj, bi, fl)]


def flash_attention(q, k, v, bias, *, name, t, n_groups, hb, nv, q_map, k_map, v_map, b_map,
                    lookback=None, sel=None, key_neg=None, diff_params=None):
    s = q.shape[1]
    nq = s // t
    qi, kj, bi, fl = _pairs(nq, lookback, bias.shape[1] - 1)
    n_pairs = int(qi.shape[0])
    dv = nv * LANES
    use_sel = sel is not None
    diff = diff_params is not None
    in_specs = [
        pl.BlockSpec((hb, t, LANES), lambda g, p, qi, kj, bi, fl: q_map(g, qi[p])),
        pl.BlockSpec((hb if diff else 1, t, LANES), lambda g, p, qi, kj, bi, fl: k_map(g, kj[p])),
        pl.BlockSpec((nv, t, LANES), lambda g, p, qi, kj, bi, fl: v_map(g, kj[p])),
        pl.BlockSpec((1 if diff else hb, 1, t, t), lambda g, p, qi, kj, bi, fl: b_map(g, bi[p])),
    ]
    args = [q, k, v, bias]
    scratch = [pltpu.VMEM((hb, t, LANES), F32), pltpu.VMEM((hb, t, dv + LANES), F32),
               pltpu.VMEM((hb, t, t), F32), pltpu.VMEM((hb, t, t), BF16), pltpu.VMEM((hb, t, LANES), F32)]
    if use_sel:
        nselp = sel.shape[2]
        in_specs += [
            pl.BlockSpec((1, t, nselp), lambda g, p, qi, kj, bi, fl: (g, qi[p], 0)),
            pl.BlockSpec((1, t, nselp), lambda g, p, qi, kj, bi, fl: (kj[p], 0, 0)),
        ]
        args += [sel, key_neg]
        scratch.append(pltpu.VMEM((hb, t, LANES + nselp), BF16))
    lam_init = None
    if diff:
        a_lambda, subln_gain, lam_init = diff_params
        in_specs += [
            pl.BlockSpec((4, HEAD_DIM), lambda g, p, qi, kj, bi, fl: (0, 0)),
            pl.BlockSpec((1, dv), lambda g, p, qi, kj, bi, fl: (0, 0)),
        ]
        args += [a_lambda.astype(F32), subln_gain.reshape(1, dv).astype(F32)]
    out_w = dv if diff else hb * dv
    grid_spec = pltpu.PrefetchScalarGridSpec(
        num_scalar_prefetch=4,
        grid=(n_groups, n_pairs),
        in_specs=in_specs,
        out_specs=pl.BlockSpec((t, out_w), lambda g, p, qi, kj, bi, fl: (qi[p], g)),
        scratch_shapes=scratch,
    )
    rc = max(8, min(t, (16 * 8 * LANES) // t))
    return pl.pallas_call(
        functools.partial(_flash_kernel, hb=hb, nv=nv, use_sel=use_sel, rc=rc, diff_lam_init=lam_init),
        out_shape=jax.ShapeDtypeStruct((s, n_groups * out_w), BF16 if diff else F32),
        grid_spec=grid_spec,
        name=name,
        compiler_params=_cparams(("parallel", "arbitrary")),
    )(qi, kj, bi, fl, *args)


def _dilated_kernel(q_ref, kp_ref, kc_ref, vp_ref, vc_ref, b_ref, o_ref, lse_ref, s_sc, p_sc, m_sc):
    span = q_ref.shape[1]
    rc = min(64, span)
    no_prev = jnp.where(pl.program_id(1) == 0, NEG, 0.0)
    col = lax.broadcasted_iota(jnp.int32, (1, 2 * span), 1)
    prev_mask = jnp.where(col < span, no_prev, 0.0)
    ones = jnp.ones((2 * span, LANES), BF16)
    for h in range(B_HEADS):
        keys = jnp.concatenate([kp_ref[h], kc_ref[h]], axis=0)
        s_sc[h] = _dot_nt(q_ref[h], keys)
        for r0 in range(0, span, rc):
            rows = slice(r0, r0 + rc)
            s = s_sc[h, rows, :] + (b_ref[h, 0, rows, :] + prev_mask)
            m = jnp.max(s, axis=-1, keepdims=True)
            m_sc[h, rows, :] = jnp.broadcast_to(m, (rc, LANES))
            p_sc[h, rows, :] = jnp.exp(s - m).astype(BF16)
        vals = jnp.concatenate([jnp.concatenate([vp_ref[h], vc_ref[h]], axis=0), ones], axis=-1)
        pv = _dot(p_sc[h], vals)
        l = pv[:, HEAD_DIM:]
        sl = slice(h * HEAD_DIM, (h + 1) * HEAD_DIM)
        o_ref[:, sl] = pv[:, 0:HEAD_DIM] / l
        lse_ref[:, sl] = m_sc[h] + jnp.log(l)


def dilated_group(proj_hm, bias, dilation, span, q_blk, k_blk, v_blk):
    nh, s, _ = proj_hm.shape
    l = s // dilation
    nb = l // span
    x = proj_hm.reshape(nh, l, dilation * LANES)
    w = B_HEADS * HEAD_DIM
    hspec = lambda blk, prev: pl.BlockSpec(
        (B_HEADS, span, LANES),
        (lambda r, n: (blk, jnp.maximum(n - 1, 0), r)) if prev else (lambda r, n: (blk, n, r)))
    o, lse = pl.pallas_call(
        _dilated_kernel,
        out_shape=[jax.ShapeDtypeStruct((l, dilation * w), F32)] * 2,
        grid=(dilation, nb),
        in_specs=[hspec(q_blk, False), hspec(k_blk, True), hspec(k_blk, False),
                  hspec(v_blk, True), hspec(v_blk, False),
                  pl.BlockSpec((B_HEADS, 1, span, 2 * span), lambda r, n: (0, 0, 0, 0))],
        out_specs=[pl.BlockSpec((span, w), lambda r, n: (n, r))] * 2,
        scratch_shapes=[pltpu.VMEM((B_HEADS, span, 2 * span), F32),
                        pltpu.VMEM((B_HEADS, span, 2 * span), BF16),
                        pltpu.VMEM((B_HEADS, span, LANES), F32)],
        name="dilated_group",
        compiler_params=_cparams(("parallel", "parallel")),
    )(x, x, x, x, x, bias)
    return o.reshape(s, w), lse.reshape(s, w)


def _mix3_kernel(o1, o2, o3, l1, l2, l3, out_ref):
    a, b, c = l1[...], l2[...], l3[...]
    m = jnp.maximum(jnp.maximum(a, b), c)
    ea, eb, ec = jnp.exp(a - m), jnp.exp(b - m), jnp.exp(c - m)
    out_ref[...] = ((ea * o1[...] + eb * o2[...] + ec * o3[...]) / (ea + eb + ec)).astype(out_ref.dtype)


def mix_dilated(os, lses):
    s, w = os[0].shape
    ts = min(256, s)
    spec = pl.BlockSpec((ts, w), lambda i: (i, 0))
    return pl.pallas_call(
        _mix3_kernel,
        out_shape=jax.ShapeDtypeStruct((s, w), BF16),
        grid=(s // ts,),
        in_specs=[spec] * 6,
        out_specs=spec,
        name="mix_dilated",
        compiler_params=_cparams(("parallel",)),
    )(*os, *lses)


def _compress_kernel(x_ref, pe_ref, w1_ref, w2_ref, gain_ref, o_ref):
    kv = pl.program_id(0)
    x = x_ref[0, 0].astype(F32)
    nc = x.shape[0]
    a = _dot((x + pe_ref[0, 0]).astype(BF16), w1_ref[0, 0])
    b = _dot((x + pe_ref[0, 1]).astype(BF16), w1_ref[0, 1])
    hid = jax.nn.gelu(a + pltpu.roll(b, nc - 1, 0))
    c = _dot(hid.astype(BF16), w2_ref[0])
    ms = jnp.mean(c * c, axis=-1, keepdims=True)
    normed = (c * lax.rsqrt(ms + EPS)) * gain_ref[...]
    o_ref[0, 0] = jnp.where(kv == 0, normed, c).astype(o_ref.dtype)


def compress_kv(xc, pe, w1, w2, gain):
    _, g, nc, cw = xc.shape
    hid = w1.shape[-1]
    return pl.pallas_call(
        _compress_kernel,
        out_shape=jax.ShapeDtypeStruct((2, g, nc, HEAD_DIM), BF16),
        grid=(2, g),
        in_specs=[pl.BlockSpec((1, 1, nc, cw), lambda a, b: (a, b, 0, 0)),
                  pl.BlockSpec((1, 2, 1, cw), lambda a, b: (a, 0, 0, 0)),
                  pl.BlockSpec((1, 2, cw, hid), lambda a, b: (a, 0, 0, 0)),
                  pl.BlockSpec((1, hid, HEAD_DIM), lambda a, b: (a, 0, 0)),
                  pl.BlockSpec((1, HEAD_DIM), lambda a, b: (0, 0))],
        out_specs=pl.BlockSpec((1, 1, nc, HEAD_DIM), lambda a, b: (a, b, 0, 0)),
        name="compress_kv",
        compiler_params=_cparams(("parallel", "parallel")),
    )(xc, pe, w1, w2, gain.reshape(1, HEAD_DIM).astype(F32))


def _cmp_kernel(q_ref, k_ref, v_ref, *rest, nkt, n_top):
    bias_refs = rest[:nkt]
    ov_ref, oc_ref, sel_ref, imp_sc, s_sc, p_sc = rest[nkt:]
    qi = pl.program_id(1)
    tq = q_ref.shape[1]
    ncp = k_ref.shape[2]
    rc = max(8, min(tq, (16 * 8 * LANES) // ncp))
    k = k_ref[0, 0]
    v = v_ref[0, 0]
    for h in range(C_HPG):
        s_sc[h] = _dot_nt(q_ref[h], k)
        for r0 in range(0, tq, rc):
            rows = slice(r0, r0 + rc)
            if nkt == 1:
                bias = bias_refs[0][h, 0, rows, :]
            else:
                bias = jnp.concatenate([b[h, 0, rows, :] for b in bias_refs], axis=-1)
            s = s_sc[h, rows, :] + bias
            m = jnp.max(s, axis=-1, keepdims=True)
            e = jnp.exp(s - m)
            pc = e * jnp.where(m > 0.5 * NEG, 1.0 / jnp.sum(e, axis=-1, keepdims=True), 0.0)
            p_sc[h, rows, :] = pc.astype(BF16)
            if h == 0:
                imp_sc[rows, :] = pc
            else:
                imp_sc[rows, :] += pc
        oc_ref[:, h * HEAD_DIM:(h + 1) * HEAD_DIM] = _dot(p_sc[h], v)
    imp = imp_sc[...]
    hi = imp.astype(BF16)
    lo = (imp - hi.astype(F32)).astype(BF16)
    ov = ov_ref[...]
    impn = _dot(hi, ov) + _dot(lo, ov)
    score_in = impn.T
    shape = score_in.shape
    t = qi * tq + lax.broadcasted_iota(jnp.int32, shape, 1)
    n = lax.broadcasted_iota(jnp.int32, shape, 0)
    cur = lax.shift_right_logical(t, int(math.log2(SEL_LEN)))
    forced = (n == 0) | (n == cur) | (n == cur - 1)
    valid = n <= cur
    score = jnp.where(valid, jnp.where(forced, FORCE, score_in), NEG)
    nf = n.astype(F32)
    sel = jnp.zeros(shape, F32)
    for _ in range(n_top):
        m = jnp.max(score, axis=0, keepdims=True)
        first = jnp.min(jnp.where(score == m, nf, 1e9), axis=0, keepdims=True)
        pick = nf == first
        sel = jnp.where(pick, 1.0, sel)
        score = jnp.where(pick, -jnp.inf, score)
    sel_ref[0] = jnp.where(valid, sel, 0.0).T.astype(sel_ref.dtype)


def cmp_attention(q_hm, kvcmp, bias, overlap, tq, n_top):
    s = q_hm.shape[1]
    ncp = kvcmp.shape[2]
    nkt = ncp // LANES
    nselp = overlap.shape[1]
    nd = bias.shape[1]
    per_tile = (LANES * CMP_STRIDE) // tq

    def bias_spec(kt):
        return pl.BlockSpec(
            (C_HPG, 1, tq, LANES),
            lambda g, i: (g, jnp.clip(i - per_tile * kt, -1, nd - 2) + 1, 0, 0))

    w = C_HEADS * HEAD_DIM
    return pl.pallas_call(
        functools.partial(_cmp_kernel, nkt=nkt, n_top=n_top),
        out_shape=[jax.ShapeDtypeStruct((s, w), F32),
                   jax.ShapeDtypeStruct((C_GROUPS, s, nselp), BF16)],
        grid=(C_GROUPS, s // tq),
        in_specs=[pl.BlockSpec((C_HPG, tq, LANES), lambda g, i: (g, i, 0)),
                  pl.BlockSpec((1, 1, ncp, LANES), lambda g, i: (0, g, 0, 0)),
                  pl.BlockSpec((1, 1, ncp, LANES), lambda g, i: (1, g, 0, 0))]
                 + [bias_spec(kt) for kt in range(nkt)]
                 + [pl.BlockSpec((ncp, nselp), lambda g, i: (0, 0))],
        out_specs=[pl.BlockSpec((tq, w // C_GROUPS), lambda g, i: (i, g)),
                   pl.BlockSpec((1, tq, nselp), lambda g, i: (g, i, 0))],
        scratch_shapes=[pltpu.VMEM((tq, ncp), F32), pltpu.VMEM((C_HPG, tq, ncp), F32),
                        pltpu.VMEM((C_HPG, tq, ncp), BF16)],
        name="cmp_attention",
        compiler_params=_cparams(("parallel", "parallel")),
    )(q_hm, kvcmp, kvcmp, *([bias] * nkt), overlap)


def _gate_kernel(oc_ref, os_ref, ow_ref, gt_ref, o_ref):
    gt = jax.nn.sigmoid(gt_ref[0].astype(F32))
    for h in range(C_HEADS):
        sl = slice(h * HEAD_DIM, (h + 1) * HEAD_DIM)
        g0 = gt[:, h:h + 1]
        g1 = gt[:, C_HEADS + h:C_HEADS + h + 1]
        g2 = gt[:, 2 * C_HEADS + h:2 * C_HEADS + h + 1]
        o_ref[:, sl] = (g0 * oc_ref[:, sl] + g1 * os_ref[:, sl] + g2 * ow_ref[:, sl]).astype(o_ref.dtype)


def gate_merge(oc, osel, ow, proj_hm, gate_blk):
    s, w = oc.shape
    ts = min(256, s)
    spec = pl.BlockSpec((ts, w), lambda i: (i, 0))
    return pl.pallas_call(
        _gate_kernel,
        out_shape=jax.ShapeDtypeStruct((s, w), BF16),
        grid=(s // ts,),
        in_specs=[spec, spec, spec, pl.BlockSpec((1, ts, LANES), lambda i: (gate_blk, i, 0))],
        out_specs=spec,
        name="gate_merge",
        compiler_params=_cparams(("parallel",)),
    )(oc, osel, ow, proj_hm)


def _tile_gain(g, reps, scale=1.0):
    return jnp.tile(g.astype(F32) * scale, reps)


def even_mixer(x, h, w_in, a_qk_gain, a_lambda, a_subln_gain, b_qk_gain, w_out, rel_table, lam_init):
    s = h.shape[0]
    scale = HEAD_DIM ** -0.5
    na = 2 * A_HEADS
    ones = lambda n: jnp.ones((n * HEAD_DIM,), F32)
    zeros = lambda n: jnp.zeros((n * HEAD_DIM,), F32)
    d = h.shape[1]
    wa = na * HEAD_DIM
    table_t = rel_table.astype(F32).T

    def pair_cols(w):
        return w.reshape(d, 2, A_HEADS, HEAD_DIM).transpose(0, 2, 1, 3).reshape(d, wa)

    w_a = jnp.concatenate([pair_cols(w_in[:, 0:wa]), pair_cols(w_in[:, wa:2 * wa]), w_in[:, 2 * wa:3 * wa]],
                          axis=1).astype(BF16)
    gain_a = jnp.concatenate([_tile_gain(a_qk_gain[0], na, scale), _tile_gain(a_qk_gain[1], na), ones(na)])
    flag_a = jnp.concatenate([ones(2 * na), zeros(na)])
    proj_a = proj_headmajor(h, w_a, gain_a, flag_a, tn=512)
    t = min(1024, s)
    nda = min(-(-(FAR - 1) // t) + 2, s // t)
    bias_a = causal_bias_tiles(table_t[:A_HEADS], t, nda)
    ao = flash_attention(
        proj_a, proj_a, proj_a, bias_a, name="diff_attention", t=t, n_groups=A_HEADS, hb=2, nv=2,
        q_map=lambda g, qb: (g, qb, 0),
        k_map=lambda g, kb: (A_HEADS + g, kb, 0),
        v_map=lambda g, kb: (2 * A_HEADS + g, kb, 0),
        b_map=lambda g, bt: (g, bt, 0, 0),
        diff_params=(a_lambda, a_subln_gain, lam_init))

    gain_b = jnp.concatenate([_tile_gain(b_qk_gain[0], B_HEADS, scale), _tile_gain(b_qk_gain[1], B_HEADS),
                              ones(B_HEADS)])
    flag_b = jnp.concatenate([ones(2 * B_HEADS), zeros(B_HEADS)])
    proj = proj_headmajor(h, w_in, gain_b, flag_b, tn=512, col0=3 * wa)
    hb0 = 0
    os, lses = [], []
    for window, dilation in B_CONFIGS:
        span = window // dilation
        j = 2 * span - np.arange(3 * span)
        g_b = _bias_by_distance(table_t[A_HEADS:A_HEADS + B_HEADS], j * dilation, (j >= 0) & (j <= span))
        bias_b = toeplitz_tiles(g_b[:, None, :], span, 2 * span, 1, span)
        o, lse = dilated_group(proj, bias_b, dilation, span, hb0, hb0 + 1, hb0 + 2)
        os.append(o)
        lses.append(lse)
    bo = mix_dilated(os, lses)
    return matmul2_residual(ao, bo, w_out, x, tm=1024, tn=512)


def nsa_mixer(x, h, w_in, c_qk_gain, c_cmp_pe, c_cmp_w1, c_cmp_w2, w_out, rel_table):
    s, d = h.shape
    scale = HEAD_DIM ** -0.5
    g = C_GROUPS
    n_in = w_in.shape[1]
    n_pad = -(-n_in // 640) * 640
    ones = lambda n: jnp.ones((n,), F32)
    zeros = lambda n: jnp.zeros((n,), F32)
    kvw = g * HEAD_DIM
    gain = jnp.concatenate([_tile_gain(c_qk_gain[0], C_HEADS, scale), ones(2 * kvw),
                            _tile_gain(c_qk_gain[2], g), ones(kvw), _tile_gain(c_qk_gain[3], g), ones(kvw),
                            ones(n_pad - C_HEADS * HEAD_DIM - 6 * kvw)])
    flag = jnp.concatenate([ones(C_HEADS * HEAD_DIM), zeros(2 * kvw), ones(kvw), zeros(kvw), ones(kvw),
                            zeros(kvw), zeros(n_pad - C_HEADS * HEAD_DIM - 6 * kvw)])
    w_pad = jnp.pad(w_in.astype(BF16), ((0, 0), (0, n_pad - n_in)))
    proj = proj_headmajor(h, w_pad, gain, flag, tn=640)
    kv0 = C_HEADS
    table_t = rel_table.astype(F32).T[:C_HEADS]

    nc = s // CMP_STRIDE
    cw = CMP_STRIDE * HEAD_DIM
    xc = proj[kv0:kv0 + 2 * g].reshape(2, g, nc, cw)
    pe = c_cmp_pe.astype(F32).reshape(2, 2, 1, cw)
    w1 = c_cmp_w1.astype(BF16).reshape(2, 2, cw, c_cmp_w1.shape[-1])
    kvcmp = compress_kv(xc, pe, w1, c_cmp_w2.astype(BF16), c_qk_gain[1])
    if nc % LANES:
        kvcmp = jnp.pad(kvcmp, ((0, 0), (0, 0), (0, LANES - nc % LANES), (0, 0)))
    ncp = kvcmp.shape[2]

    tq = min(256, s)
    n_cmp = (s - CMP_LEN) // CMP_STRIDE + 1
    n_sel = s // SEL_LEN
    nselp = -(-n_sel // LANES) * LANES
    n_top = min(SEL_TOPK, n_sel)
    off = CMP_STRIDE * (LANES - 1)
    far_tile = -(-(FAR + off + CMP_LEN - 1) // tq)
    nd = far_tile + 2
    wx = -(-(tq + off) // LANES) * LANES
    dist_c = np.arange(tq * (nd - 1) + wx) - (tq + off + CMP_LEN - 1)
    vec_c = _bias_by_distance(table_t, dist_c, dist_c >= 0)
    g_c = jnp.stack([vec_c[:, tq * d:tq * d + wx] for d in range(nd)], axis=1)
    bias_c = toeplitz_tiles(g_c, LANES, tq, CMP_STRIDE, off, transpose=True)
    c_start = np.arange(ncp) * CMP_STRIDE
    s_start = np.arange(nselp) * SEL_LEN
    ov = ((c_start[:, None] < s_start[None, :] + SEL_LEN) & (c_start[:, None] + CMP_LEN > s_start[None, :])
          & (np.arange(ncp)[:, None] < n_cmp) & (np.arange(nselp)[None, :] < n_sel))
    oc, sel = cmp_attention(proj, kvcmp, bias_c, jnp.asarray(ov.astype(np.float32), BF16), tq, n_top)

    t = tq
    nds = min(-(-(FAR - 1) // t) + 2, s // t)
    bias_s = causal_bias_tiles(table_t, t, nds)
    key_blk = np.arange(s) // SEL_LEN
    key_neg = np.where(key_blk[:, None] == np.arange(nselp)[None, :], NEG, 0.0).reshape(s // t, t, nselp)
    osel = flash_attention(
        proj, proj, proj, bias_s, name="selected_attention", t=t, n_groups=g, hb=C_HPG, nv=1,
        q_map=lambda gg, qb: (gg, qb, 0),
        k_map=lambda gg, kb: (kv0 + 2 * g + gg, kb, 0),
        v_map=lambda gg, kb: (kv0 + 3 * g + gg, kb, 0),
        b_map=lambda gg, bt: (gg, bt, 0, 0),
        sel=sel, key_neg=jnp.asarray(key_neg.astype(np.float32), BF16))

    look = -(-(C_WINDOW - 1) // t)
    ndw = min(look + 1, s // t)
    bias_w = causal_bias_tiles(table_t, t, ndw, window=C_WINDOW)
    ow = flash_attention(
        proj, proj, proj, bias_w, name="window_attention", t=t, n_groups=g, hb=C_HPG, nv=1, lookback=look,
        q_map=lambda gg, qb: (gg, qb, 0),
        k_map=lambda gg, kb: (kv0 + 4 * g + gg, kb, 0),
        v_map=lambda gg, kb: (kv0 + 5 * g + gg, kb, 0),
        b_map=lambda gg, bt: (gg, bt, 0, 0))

    o = gate_merge(oc, osel, ow, proj, kv0 + 6 * g)
    return matmul_residual(o, w_out, x, tm=1024, tn=512)


def conv_ffn(x, h, w_gate, w_up, conv_w, conv_b, w_down):
    act = ffn_gate_up(h, w_gate, w_up, conv_w, conv_b)
    return matmul_residual(act, w_down, x, tm=1024, tn=256)


def kernel(x, rel_table, ev_norm, ev_w_in, a_qk_gain, a_lambda, a_subln_gain, b_qk_gain, ev_w_out,
           od_norm, od_w_in, c_qk_gain, c_cmp_pe, c_cmp_w1, c_cmp_w2, od_w_out,
           ffn_norm, ffn_w_gate, ffn_w_up, ffn_conv_w, ffn_conv_b, ffn_w_down):
    b, s, d = x.shape
    depth = ffn_norm.shape[0]
    outs = []
    for bi in range(b):
        y = x[bi].astype(F32)
        for i in range(depth):
            if i % 2 == 0:
                e = i // 2
                lam_init = 0.8 - 0.6 * math.exp(-0.3 * i)
                y = even_mixer(y, rmsnorm(y, ev_norm[e]), ev_w_in[e], a_qk_gain[e], a_lambda[e],
                               a_subln_gain[e], b_qk_gain[e], ev_w_out[e], rel_table, lam_init)
            else:
                o = i // 2
                y = nsa_mixer(y, rmsnorm(y, od_norm[o]), od_w_in[o], c_qk_gain[o], c_cmp_pe[o],
                              c_cmp_w1[o], c_cmp_w2[o], od_w_out[o], rel_table)
            y = conv_ffn(y, rmsnorm(y, ffn_norm[i]), ffn_w_gate[i], ffn_w_up[i], ffn_conv_w[i],
                         ffn_conv_b[i], ffn_w_down[i])
        outs.append(y)
    return jnp.stack(outs).astype(x.dtype)
```

```python
import functools
import math

import numpy as np
import jax
import jax.numpy as jnp
from jax import lax
from jax.experimental import pallas as pl
from jax.experimental.pallas import tpu as pltpu

HEAD_DIM = 128
A_HEADS = 8
B_HEADS = 16
B_CONFIGS = ((128, 1), (512, 4), (2048, 16))
C_HEADS = 32
C_GROUPS = 2
C_HPG = C_HEADS // C_GROUPS
CMP_LEN = 32
CMP_STRIDE = 16
SEL_LEN = 64
SEL_TOPK = 16
C_WINDOW = 512
REL_BUCKETS = 32
REL_MAX_DIST = 2048
EPS = 1e-6
NEG = -1e30
FORCE = 1e9

LANES = 128
VMEM_LIMIT = 56 * 1024 * 1024

F32 = jnp.float32
BF16 = jnp.bfloat16


def _cparams(sem):
    return pltpu.CompilerParams(dimension_semantics=sem, vmem_limit_bytes=VMEM_LIMIT)


def _dot(a, b):
    return jnp.dot(a, b, preferred_element_type=F32)


def _dot_nt(a, b):
    return lax.dot_general(a, b, (((1,), (1,)), ((), ())), preferred_element_type=F32)


def _bucket_np(dist):
    n = np.maximum(dist, 0).astype(np.int32)
    max_exact = REL_BUCKETS // 2
    nf = np.maximum(n, 1).astype(np.float32)
    ratio = np.log(nf / np.float32(max_exact)) / np.float32(math.log(REL_MAX_DIST / max_exact))
    large = np.minimum(max_exact + (ratio * np.float32(REL_BUCKETS - max_exact)).astype(np.int32),
                       REL_BUCKETS - 1)
    return np.where(n < max_exact, n, large).astype(np.int32)


def _far_distance():
    b = _bucket_np(np.arange(4 * REL_MAX_DIST))
    return int(np.max(np.nonzero(b != REL_BUCKETS - 1)[0])) + 1


FAR = _far_distance()


def _bias_by_distance(table_t, dist, valid):
    b = jnp.take(table_t, jnp.asarray(_bucket_np(dist)), axis=1)
    return jnp.where(jnp.asarray(valid), b, NEG).astype(F32)


def _toeplitz_kernel(g_ref, o_ref, *, rows, width, shift, stride, transpose):
    for b in range(g_ref.shape[1]):
        _toeplitz_tile(g_ref[0, b], o_ref.at[0, b], rows, width, shift, stride, transpose)


def _toeplitz_tile(g, o_ref, rows, width, shift, stride, transpose):
    wx = g.shape[1]
    offset = (wx - shift) % wx
    if stride == 1 and not transpose and rows % LANES == 0 and width % LANES == 0 and offset % LANES == 0:
        nr, ncb = rows // LANES, width // LANES
        subs = {}
        for db in range(-(nr - 1), ncb):
            lo = LANES * db + offset - LANES
            gw = jnp.broadcast_to(g[:, lo:lo + 2 * LANES], (LANES, 2 * LANES))
            subs[db] = pltpu.roll(gw, LANES, 1, stride=1, stride_axis=0)[:, :LANES]
        for bi in range(nr):
            for bj in range(ncb):
                o_ref[bi * LANES:(bi + 1) * LANES, bj * LANES:(bj + 1) * LANES] = subs[bj - bi]
        return
    x = jnp.broadcast_to(g, (rows, wx))
    y = pltpu.roll(x, shift, 1, stride=stride, stride_axis=0)[:, :width]
    o_ref[...] = y.T if transpose else y


def toeplitz_tiles(g, rows, width, stride, offset, transpose=False):
    h, nd, wx = g.shape
    assert wx % LANES == 0 and 0 <= offset - stride * (rows - 1) and width - 1 + offset < wx
    shape = (width, rows) if transpose else (rows, width)
    tb = max(b for b in range(1, nd + 1) if nd % b == 0 and b * rows * width * 4 <= max(4 << 20, rows * width * 4))
    return pl.pallas_call(
        functools.partial(_toeplitz_kernel, rows=rows, width=width, shift=(wx - offset) % wx, stride=stride,
                          transpose=transpose),
        out_shape=jax.ShapeDtypeStruct((h, nd) + shape, F32),
        grid=(h, nd // tb),
        in_specs=[pl.BlockSpec((1, tb, 1, wx), lambda a, b: (a, b, 0, 0))],
        out_specs=pl.BlockSpec((1, tb) + shape, lambda a, b: (a, b, 0, 0)),
        name="toeplitz_tiles",
        compiler_params=_cparams(("parallel", "parallel")),
    )(g.reshape(h, nd, 1, wx))


def causal_bias_tiles(table_t, t, n_tiles, window=None):
    span = (n_tiles + 1) * t
    dist = (n_tiles * t) - np.arange(span)
    valid = dist >= 0 if window is None else (dist >= 0) & (dist < window)
    vrev = _bias_by_distance(table_t, dist, valid)
    g = jnp.stack([vrev[:, (n_tiles - d - 1) * t:(n_tiles - d + 1) * t] for d in range(n_tiles)], axis=1)
    return toeplitz_tiles(g, t, t, 1, t)


def _rmsnorm_kernel(x_ref, g_ref, o_ref):
    x = x_ref[...]
    ms = jnp.mean(x * x, axis=-1, keepdims=True)
    o_ref[...] = ((x * lax.rsqrt(ms + EPS)) * g_ref[...]).astype(o_ref.dtype)


def rmsnorm(x, gain):
    s, d = x.shape
    tr = min(256, s)
    return pl.pallas_call(
        _rmsnorm_kernel,
        out_shape=jax.ShapeDtypeStruct((s, d), BF16),
        grid=(s // tr,),
        in_specs=[pl.BlockSpec((tr, d), lambda i: (i, 0)), pl.BlockSpec((1, d), lambda i: (0, 0))],
        out_specs=pl.BlockSpec((tr, d), lambda i: (i, 0)),
        name="rmsnorm",
        compiler_params=_cparams(("parallel",)),
    )(x, gain.reshape(1, d).astype(F32))


ROW_SUB = 256


def col_blocks(w, tn):
    return w.astype(BF16)[None]


def _proj_kernel(x_ref, w_ref, gain_ref, flag_ref, *refs, nblk, ts, dilations):
    perm_refs = refs[:len(dilations)]
    o_ref = refs[len(dilations)]
    od_refs = refs[len(dilations) + 1:]
    w = w_ref[0].astype(BF16)
    for r0 in range(0, x_ref.shape[0], ts):
        acc = _dot(x_ref[r0:r0 + ts, :], w)
        for c in range(nblk):
            sl = slice(c * LANES, (c + 1) * LANES)
            blk = acc[:, sl]
            ms = jnp.mean(blk * blk, axis=-1, keepdims=True)
            r = jnp.where(flag_ref[:, sl] > 0, lax.rsqrt(ms + EPS), 1.0)
            y = ((blk * r) * gain_ref[:, sl]).astype(o_ref.dtype)
            o_ref[c, r0:r0 + ts, :] = y
            for d, perm_ref, od_ref in zip(dilations, perm_refs, od_refs):
                yp = _dot(perm_ref[...], y).astype(o_ref.dtype)
                m = ts // d
                for res in range(d):
                    od_ref[c, r0 // d:r0 // d + m, res * LANES:(res + 1) * LANES] = yp[res * m:(res + 1) * m, :]


def proj_headmajor(x, w, gain, flag, tn, col0=0, dilations=()):
    s, k = x.shape
    n = gain.shape[0]
    tm = min(2048 if w.dtype == BF16 else 1024, s)
    ts = min(ROW_SUB, tm)
    nblk = tn // LANES
    cb0 = col0 // tn
    perms = []
    for d in dilations:
        dst = np.arange(ts)
        src = (dst % (ts // d)) * d + dst // (ts // d)
        perms.append(jnp.asarray((src[:, None] == np.arange(ts)[None, :]).astype(np.float32), BF16))
    out_shape = [jax.ShapeDtypeStruct((n // LANES, s, LANES), BF16)]
    out_specs = [pl.BlockSpec((nblk, tm, LANES), lambda i, j: (j, i, 0))]
    for d in dilations:
        out_shape.append(jax.ShapeDtypeStruct((n // LANES, s // d, d * LANES), BF16))
        out_specs.append(pl.BlockSpec((nblk, tm // d, d * LANES), lambda i, j: (j, i, 0)))
    outs = pl.pallas_call(
        functools.partial(_proj_kernel, nblk=nblk, ts=ts, dilations=tuple(dilations)),
        out_shape=out_shape,
        grid=(s // tm, n // tn),
        in_specs=[pl.BlockSpec((tm, k), lambda i, j: (i, 0)),
                  pl.BlockSpec((1, k, tn), lambda i, j: (0, 0, cb0 + j)),
                  pl.BlockSpec((1, tn), lambda i, j: (0, j)),
                  pl.BlockSpec((1, tn), lambda i, j: (0, j))]
                 + [pl.BlockSpec((ts, ts), lambda i, j: (0, 0))] * len(dilations),
        out_specs=out_specs,
        name="proj_headmajor",
        compiler_params=_cparams(("parallel", "parallel")),
    )(x, w[None], gain.reshape(1, n).astype(F32), flag.reshape(1, n).astype(F32), *perms)
    return outs if dilations else outs[0]


def _mm_res_kernel(x_ref, w_ref, r_ref, o_ref):
    o_ref[...] = r_ref[...] + _dot(x_ref[...], w_ref[0])


def matmul_residual(x, w, res, tm, tn):
    s, k = x.shape
    n = w.shape[1]
    tm = min(tm, s)
    tn = min(tn, n)
    return pl.pallas_call(
        _mm_res_kernel,
        out_shape=jax.ShapeDtypeStruct((s, n), F32),
        grid=(s // tm, n // tn),
        in_specs=[pl.BlockSpec((tm, k), lambda i, j: (i, 0), pipeline_mode=pl.Buffered(1)),
                  pl.BlockSpec((1, k, tn), lambda i, j: (0, 0, j)),
                  pl.BlockSpec((tm, tn), lambda i, j: (i, j))],
        out_specs=pl.BlockSpec((tm, tn), lambda i, j: (i, j)),
        name="matmul_residual",
        compiler_params=_cparams(("parallel", "arbitrary")),
    )(x, col_blocks(w, tn), res)


def _mm2_res_kernel(x1_ref, x2_ref, w_ref, r_ref, o_ref, *, k1):
    acc = _dot(x1_ref[...], w_ref[0, 0:k1, :]) + _dot(x2_ref[...], w_ref[0, k1:, :])
    o_ref[...] = r_ref[...] + acc


def matmul2_residual(x1, x2, w, res, tm, tn):
    s, k1 = x1.shape
    k2 = x2.shape[1]
    n = w.shape[1]
    tm = min(tm, s)
    tn = min(tn, n)
    return pl.pallas_call(
        functools.partial(_mm2_res_kernel, k1=k1),
        out_shape=jax.ShapeDtypeStruct((s, n), F32),
        grid=(s // tm, n // tn),
        in_specs=[pl.BlockSpec((tm, k1), lambda i, j: (i, 0)),
                  pl.BlockSpec((tm, k2), lambda i, j: (i, 0)),
                  pl.BlockSpec((1, k1 + k2, tn), lambda i, j: (0, 0, j)),
                  pl.BlockSpec((tm, tn), lambda i, j: (i, j))],
        out_specs=pl.BlockSpec((tm, tn), lambda i, j: (i, j)),
        name="matmul2_residual",
        compiler_params=_cparams(("parallel", "arbitrary")),
    )(x1, x2, col_blocks(w, tn), res)


def _ffn1_kernel(x_ref, wg_ref, wu_ref, cw_ref, cb_ref, o_ref, carry_ref, *, ts):
    i = pl.program_id(0)
    j = pl.program_id(1)

    @pl.when(i == 0)
    def _():
        carry_ref[j] = jnp.zeros(carry_ref.shape[1:], F32)

    prev = carry_ref[j]
    wg = wg_ref[0].astype(BF16)
    wu = wu_ref[0].astype(BF16)
    cw = cw_ref[...]
    cb = cb_ref[...]
    row = lax.broadcasted_iota(jnp.int32, (ts, wg.shape[1]), 0)
    for r0 in range(0, x_ref.shape[0], ts):
        x = x_ref[r0:r0 + ts, :]
        g = _dot(x, wg)
        u = _dot(x, wu)
        p7 = prev[7:8, :]
        p6 = prev[6:7, :]
        g1 = jnp.where(row == 0, p7, pltpu.roll(g, 1, 0))
        g2 = jnp.where(row == 0, p6, jnp.where(row == 1, p7, pltpu.roll(g, 2, 0)))
        gc = g2 * cw[0:1, :] + g1 * cw[1:2, :] + g * cw[2:3, :] + cb
        o_ref[r0:r0 + ts, :] = ((gc * jax.nn.sigmoid(gc)) * u).astype(o_ref.dtype)
        prev = g[ts - 8:ts, :]
    carry_ref[j] = prev


def ffn_gate_up(h, wg, wu, layer, conv_w, conv_b):
    s, d = h.shape
    f = wg.shape[2]
    tm = min(2048, s)
    tn = 256
    nj = f // tn
    return pl.pallas_call(
        functools.partial(_ffn1_kernel, ts=min(ROW_SUB, tm)),
        out_shape=jax.ShapeDtypeStruct((s, f), BF16),
        grid=(s // tm, nj),
        in_specs=[pl.BlockSpec((tm, d), lambda i, j: (i, 0), pipeline_mode=pl.Buffered(1)),
                  pl.BlockSpec((1, d, tn), lambda i, j: (layer, 0, j)),
                  pl.BlockSpec((1, d, tn), lambda i, j: (layer, 0, j)),
                  pl.BlockSpec((3, tn), lambda i, j: (0, j)),
                  pl.BlockSpec((1, tn), lambda i, j: (0, j))],
        out_specs=pl.BlockSpec((tm, tn), lambda i, j: (i, j)),
        scratch_shapes=[pltpu.VMEM((nj, 8, tn), F32)],
        name="ffn_gate_up",
        compiler_params=_cparams(("arbitrary", "arbitrary")),
    )(h, wg, wu, conv_w.astype(F32), conv_b.reshape(1, f).astype(F32))


def _flash_kernel(qi_ref, kj_ref, bi_ref, fl_ref, *refs, hb, nv, use_sel, rc, diff_lam_init):
    q_ref, k_ref, v_ref, b_ref = refs[:4]
    refs = refs[4:]
    if use_sel:
        sel_ref, en_ref = refs[:2]
        refs = refs[2:]
    if diff_lam_init is not None:
        lam_ref, gain_ref = refs[:2]
        refs = refs[2:]
    o_ref, m_sc, acc_sc, s_sc, p_sc, al_sc = refs[:6]
    if use_sel:
        qa_sc = refs[6]
    per_head_k = k_ref.shape[0] == hb and hb > 1
    per_head_bias = b_ref.shape[0] == hb
    p = pl.program_id(1)
    flags = fl_ref[p]
    dv = nv * LANES
    t = q_ref.shape[1]
    tk = k_ref.shape[1]

    @pl.when((flags & 1) != 0)
    def _():
        m_sc[...] = jnp.full(m_sc.shape, -jnp.inf, F32)
        acc_sc[...] = jnp.zeros(acc_sc.shape, F32)
        if use_sel:
            notsel = (1.0 - sel_ref[0].astype(F32)).astype(BF16)
            for h in range(hb):
                qa_sc[h, :, 0:LANES] = q_ref[h]
                qa_sc[h, :, LANES:] = notsel

    v_aug = jnp.concatenate([v_ref[c] for c in range(nv)] + [jnp.ones((tk, LANES), BF16)], axis=-1)

    for h in range(hb):
        k = k_ref[h if per_head_k else 0]
        if use_sel:
            k = jnp.concatenate([k, en_ref[0]], axis=-1)
        qh = qa_sc[h] if use_sel else q_ref[h]
        s_sc[h] = _dot_nt(qh, k)
        hbias = h if per_head_bias else 0
        for r0 in range(0, t, rc):
            rows = slice(r0, r0 + rc)
            s = s_sc[h, rows, :] + b_ref[hbias, 0, rows, :]
            m_old = m_sc[h, rows, :]
            m_new = jnp.maximum(m_old, jnp.max(s, axis=-1, keepdims=True))
            al_sc[h, rows, :] = jnp.exp(m_old - m_new)
            m_sc[h, rows, :] = m_new
            p_sc[h, rows, :] = jnp.exp(s - jnp.tile(m_new, (1, tk // LANES))).astype(BF16)
        acc_sc[h] = jnp.tile(al_sc[h], (1, nv + 1)) * acc_sc[h] + _dot(p_sc[h], v_aug)

    def normalised(h):
        acc = acc_sc[h]
        return acc[:, 0:dv] / jnp.tile(acc[:, dv:], (1, nv))

    @pl.when((flags & 2) != 0)
    def _():
        if diff_lam_init is None:
            for h in range(hb):
                o_ref[:, h * dv:(h + 1) * dv] = normalised(h).astype(o_ref.dtype)
        else:
            lf = lam_ref[...]
            s1 = jnp.sum(lf[0:1, :] * lf[1:2, :], axis=-1, keepdims=True)
            s2 = jnp.sum(lf[2:3, :] * lf[3:4, :], axis=-1, keepdims=True)
            lam = jnp.exp(s1) - jnp.exp(s2) + diff_lam_init
            o = normalised(0) - lam * normalised(1)
            ms = jnp.mean(o * o, axis=-1, keepdims=True)
            y = ((o * lax.rsqrt(ms + EPS)) * gain_ref[...]) * (1.0 - diff_lam_init)
            o_ref[...] = y.astype(o_ref.dtype)


def _pairs(nq, lookback, max_bias):
    qi, kj, bi, fl = [], [], [], []
    for q in range(nq):
        lo = 0 if lookback is None else max(0, q - lookback)
        for k in range(lo, q + 1):
            qi.append(q)
            kj.append(k)
            bi.append(min(q - k, max_bias))
            fl.append((1 if k == lo else 0) | (2 if k == q else 0))
    return [jnp.asarray(np.asarray(a, np.int32)) for a in (qi, kj, bi, fl)]


def flash_attention(q, k, v, bias, *, name, t, n_groups, hb, nv, q_map, k_map, v_map, b_map,
                    lookback=None, sel=None, key_neg=None, diff_params=None):
    s = q.shape[1]
    nq = s // t
    qi, kj, bi, fl = _pairs(nq, lookback, bias.shape[1] - 1)
    n_pairs = int(qi.shape[0])
    dv = nv * LANES
    use_sel = sel is not None
    diff = diff_params is not None
    in_specs = [
        pl.BlockSpec((hb, t, LANES), lambda g, p, qi, kj, bi, fl: q_map(g, qi[p])),
        pl.BlockSpec((hb if diff else 1, t, LANES), lambda g, p, qi, kj, bi, fl: k_map(g, kj[p])),
        pl.BlockSpec((nv, t, LANES), lambda g, p, qi, kj, bi, fl: v_map(g, kj[p])),
        pl.BlockSpec((1 if diff else hb, 1, t, t), lambda g, p, qi, kj, bi, fl: b_map(g, bi[p])),
    ]
    args = [q, k, v, bias]
    scratch = [pltpu.VMEM((hb, t, LANES), F32), pltpu.VMEM((hb, t, dv + LANES), F32),
               pltpu.VMEM((hb, t, t), F32), pltpu.VMEM((hb, t, t), BF16), pltpu.VMEM((hb, t, LANES), F32)]
    if use_sel:
        nselp = sel.shape[2]
        in_specs += [
            pl.BlockSpec((1, t, nselp), lambda g, p, qi, kj, bi, fl: (g, qi[p], 0)),
            pl.BlockSpec((1, t, nselp), lambda g, p, qi, kj, bi, fl: (kj[p], 0, 0)),
        ]
        args += [sel, key_neg]
        scratch.append(pltpu.VMEM((hb, t, LANES + nselp), BF16))
    lam_init = None
    if diff:
        a_lambda, subln_gain, lam_init = diff_params
        in_specs += [
            pl.BlockSpec((4, HEAD_DIM), lambda g, p, qi, kj, bi, fl: (0, 0)),
            pl.BlockSpec((1, dv), lambda g, p, qi, kj, bi, fl: (0, 0)),
        ]
        args += [a_lambda.astype(F32), subln_gain.reshape(1, dv).astype(F32)]
    out_w = dv if diff else hb * dv
    grid_spec = pltpu.PrefetchScalarGridSpec(
        num_scalar_prefetch=4,
        grid=(n_groups, n_pairs),
        in_specs=in_specs,
        out_specs=pl.BlockSpec((t, out_w), lambda g, p, qi, kj, bi, fl: (qi[p], g)),
        scratch_shapes=scratch,
    )
    rc = max(8, min(t, (16 * 8 * LANES) // t))
    return pl.pallas_call(
        functools.partial(_flash_kernel, hb=hb, nv=nv, use_sel=use_sel, rc=rc, diff_lam_init=lam_init),
        out_shape=jax.ShapeDtypeStruct((s, n_groups * out_w), BF16 if diff else F32),
        grid_spec=grid_spec,
        name=name,
        compiler_params=_cparams(("parallel", "arbitrary")),
    )(qi, kj, bi, fl, *args)


def _dilated_kernel(q_ref, kp_ref, kc_ref, vp_ref, vc_ref, b_ref, o_ref, lse_ref, s_sc, p_sc, m_sc):
    span = q_ref.shape[1]
    rc = min(64, span)
    no_prev = jnp.where(pl.program_id(1) == 0, NEG, 0.0)
    col = lax.broadcasted_iota(jnp.int32, (1, 2 * span), 1)
    prev_mask = jnp.where(col < span, no_prev, 0.0)
    ones = jnp.ones((2 * span, LANES), BF16)
    for h in range(B_HEADS):
        keys = jnp.concatenate([kp_ref[h], kc_ref[h]], axis=0)
        s_sc[h] = _dot_nt(q_ref[h], keys)
        for r0 in range(0, span, rc):
            rows = slice(r0, r0 + rc)
            s = s_sc[h, rows, :] + (b_ref[h, 0, rows, :] + prev_mask)
            m = jnp.max(s, axis=-1, keepdims=True)
            m_sc[h, rows, :] = jnp.broadcast_to(m, (rc, LANES))
            p_sc[h, rows, :] = jnp.exp(s - m).astype(BF16)
        vals = jnp.concatenate([jnp.concatenate([vp_ref[h], vc_ref[h]], axis=0), ones], axis=-1)
        pv = _dot(p_sc[h], vals)
        l = pv[:, HEAD_DIM:]
        sl = slice(h * HEAD_DIM, (h + 1) * HEAD_DIM)
        o_ref[:, sl] = pv[:, 0:HEAD_DIM] / l
        lse_ref[:, sl] = m_sc[h] + jnp.log(l)


def dilated_group(x, bias, dilation, span, q_blk, k_blk, v_blk):
    nh, l, _ = x.shape
    s = l * dilation
    nb = l // span
    w = B_HEADS * HEAD_DIM
    hspec = lambda blk, prev: pl.BlockSpec(
        (B_HEADS, span, LANES),
        (lambda r, n: (blk, jnp.maximum(n - 1, 0), r)) if prev else (lambda r, n: (blk, n, r)))
    o, lse = pl.pallas_call(
        _dilated_kernel,
        out_shape=[jax.ShapeDtypeStruct((l, dilation * w), F32)] * 2,
        grid=(dilation, nb),
        in_specs=[hspec(q_blk, False), hspec(k_blk, True), hspec(k_blk, False),
                  hspec(v_blk, True), hspec(v_blk, False),
                  pl.BlockSpec((B_HEADS, 1, span, 2 * span), lambda r, n: (0, 0, 0, 0))],
        out_specs=[pl.BlockSpec((span, w), lambda r, n: (n, r))] * 2,
        scratch_shapes=[pltpu.VMEM((B_HEADS, span, 2 * span), F32),
                        pltpu.VMEM((B_HEADS, span, 2 * span), BF16),
                        pltpu.VMEM((B_HEADS, span, LANES), F32)],
        name="dilated_group",
        compiler_params=_cparams(("parallel", "parallel")),
    )(x, x, x, x, x, bias)
    return o.reshape(s, w), lse.reshape(s, w)


def _mix3_kernel(o1, o2, o3, l1, l2, l3, out_ref):
    a, b, c = l1[...], l2[...], l3[...]
    m = jnp.maximum(jnp.maximum(a, b), c)
    ea, eb, ec = jnp.exp(a - m), jnp.exp(b - m), jnp.exp(c - m)
    out_ref[...] = ((ea * o1[...] + eb * o2[...] + ec * o3[...]) / (ea + eb + ec)).astype(out_ref.dtype)


def mix_dilated(os, lses):
    s, w = os[0].shape
    ts = min(256, s)
    spec = pl.BlockSpec((ts, w), lambda i: (i, 0))
    return pl.pallas_call(
        _mix3_kernel,
        out_shape=jax.ShapeDtypeStruct((s, w), BF16),
        grid=(s // ts,),
        in_specs=[spec] * 6,
        out_specs=spec,
        name="mix_dilated",
        compiler_params=_cparams(("parallel",)),
    )(*os, *lses)


def _compress_kernel(x_ref, pe_ref, w1_ref, w2_ref, gain_ref, o_ref):
    kv = pl.program_id(0)
    x = x_ref[0, 0].astype(F32)
    nc = x.shape[0]
    a = _dot((x + pe_ref[0, 0]).astype(BF16), w1_ref[0, 0])
    b = _dot((x + pe_ref[0, 1]).astype(BF16), w1_ref[0, 1])
    hid = jax.nn.gelu(a + pltpu.roll(b, nc - 1, 0))
    c = _dot(hid.astype(BF16), w2_ref[0])
    ms = jnp.mean(c * c, axis=-1, keepdims=True)
    normed = (c * lax.rsqrt(ms + EPS)) * gain_ref[...]
    o_ref[0, 0] = jnp.where(kv == 0, normed, c).astype(o_ref.dtype)


def compress_kv(xc, pe, w1, w2, gain):
    _, g, nc, cw = xc.shape
    hid = w1.shape[-1]
    return pl.pallas_call(
        _compress_kernel,
        out_shape=jax.ShapeDtypeStruct((2, g, nc, HEAD_DIM), BF16),
        grid=(2, g),
        in_specs=[pl.BlockSpec((1, 1, nc, cw), lambda a, b: (a, b, 0, 0)),
                  pl.BlockSpec((1, 2, 1, cw), lambda a, b: (a, 0, 0, 0)),
                  pl.BlockSpec((1, 2, cw, hid), lambda a, b: (a, 0, 0, 0)),
                  pl.BlockSpec((1, hid, HEAD_DIM), lambda a, b: (a, 0, 0)),
                  pl.BlockSpec((1, HEAD_DIM), lambda a, b: (0, 0))],
        out_specs=pl.BlockSpec((1, 1, nc, HEAD_DIM), lambda a, b: (a, b, 0, 0)),
        name="compress_kv",
        compiler_params=_cparams(("parallel", "parallel")),
    )(xc, pe, w1, w2, gain.reshape(1, HEAD_DIM).astype(F32))


def _cmp_kernel(q_ref, k_ref, v_ref, *rest, nkt, n_top):
    bias_refs = rest[:nkt]
    ov_ref, oc_ref, sel_ref, imp_sc, s_sc, p_sc = rest[nkt:]
    qi = pl.program_id(1)
    tq = q_ref.shape[1]
    ncp = k_ref.shape[2]
    rc = max(8, min(tq, (16 * 8 * LANES) // ncp))
    k = k_ref[0, 0]
    v = v_ref[0, 0]
    for h in range(C_HPG):
        s_sc[h] = _dot_nt(q_ref[h], k)
        for r0 in range(0, tq, rc):
            rows = slice(r0, r0 + rc)
            if nkt == 1:
                bias = bias_refs[0][h, 0, rows, :]
            else:
                bias = jnp.concatenate([b[h, 0, rows, :] for b in bias_refs], axis=-1)
            s = s_sc[h, rows, :] + bias
            m = jnp.max(s, axis=-1, keepdims=True)
            e = jnp.exp(s - m)
            pc = e * jnp.where(m > 0.5 * NEG, 1.0 / jnp.sum(e, axis=-1, keepdims=True), 0.0)
            p_sc[h, rows, :] = pc.astype(BF16)
            if h == 0:
                imp_sc[rows, :] = pc
            else:
                imp_sc[rows, :] += pc
        oc_ref[:, h * HEAD_DIM:(h + 1) * HEAD_DIM] = _dot(p_sc[h], v)
    imp = imp_sc[...]
    hi = imp.astype(BF16)
    lo = (imp - hi.astype(F32)).astype(BF16)
    ov = ov_ref[...]
    impn = _dot(hi, ov) + _dot(lo, ov)
    score_in = impn.T
    shape = score_in.shape
    t = qi * tq + lax.broadcasted_iota(jnp.int32, shape, 1)
    n = lax.broadcasted_iota(jnp.int32, shape, 0)
    cur = lax.shift_right_logical(t, int(math.log2(SEL_LEN)))
    forced = (n == 0) | (n == cur) | (n == cur - 1)
    valid = n <= cur
    score = jnp.where(valid, jnp.where(forced, FORCE, score_in), NEG)
    nf = n.astype(F32)
    sel = jnp.zeros(shape, F32)
    for _ in range(n_top):
        m = jnp.max(score, axis=0, keepdims=True)
        first = jnp.min(jnp.where(score == m, nf, 1e9), axis=0, keepdims=True)
        pick = nf == first
        sel = jnp.where(pick, 1.0, sel)
        score = jnp.where(pick, -jnp.inf, score)
    sel_ref[0] = jnp.where(valid, sel, 0.0).T.astype(sel_ref.dtype)


def cmp_attention(q_hm, kvcmp, bias, overlap, tq, n_top):
    s = q_hm.shape[1]
    ncp = kvcmp.shape[2]
    nkt = ncp // LANES
    nselp = overlap.shape[1]
    nd = bias.shape[1]
    per_tile = (LANES * CMP_STRIDE) // tq

    def bias_spec(kt):
        return pl.BlockSpec(
            (C_HPG, 1, tq, LANES),
            lambda g, i: (g, jnp.clip(i - per_tile * kt, -1, nd - 2) + 1, 0, 0))

    w = C_HEADS * HEAD_DIM
    return pl.pallas_call(
        functools.partial(_cmp_kernel, nkt=nkt, n_top=n_top),
        out_shape=[jax.ShapeDtypeStruct((s, w), F32),
                   jax.ShapeDtypeStruct((C_GROUPS, s, nselp), BF16)],
        grid=(C_GROUPS, s // tq),
        in_specs=[pl.BlockSpec((C_HPG, tq, LANES), lambda g, i: (g, i, 0)),
                  pl.BlockSpec((1, 1, ncp, LANES), lambda g, i: (0, g, 0, 0)),
                  pl.BlockSpec((1, 1, ncp, LANES), lambda g, i: (1, g, 0, 0))]
                 + [bias_spec(kt) for kt in range(nkt)]
                 + [pl.BlockSpec((ncp, nselp), lambda g, i: (0, 0))],
        out_specs=[pl.BlockSpec((tq, w // C_GROUPS), lambda g, i: (i, g)),
                   pl.BlockSpec((1, tq, nselp), lambda g, i: (g, i, 0))],
        scratch_shapes=[pltpu.VMEM((tq, ncp), F32), pltpu.VMEM((C_HPG, tq, ncp), F32),
                        pltpu.VMEM((C_HPG, tq, ncp), BF16)],
        name="cmp_attention",
        compiler_params=_cparams(("parallel", "parallel")),
    )(q_hm, kvcmp, kvcmp, *([bias] * nkt), overlap)


def _gate_kernel(oc_ref, os_ref, ow_ref, gt_ref, o_ref):
    gt = jax.nn.sigmoid(gt_ref[0].astype(F32))
    for h in range(C_HEADS):
        sl = slice(h * HEAD_DIM, (h + 1) * HEAD_DIM)
        g0 = gt[:, h:h + 1]
        g1 = gt[:, C_HEADS + h:C_HEADS + h + 1]
        g2 = gt[:, 2 * C_HEADS + h:2 * C_HEADS + h + 1]
        o_ref[:, sl] = (g0 * oc_ref[:, sl] + g1 * os_ref[:, sl] + g2 * ow_ref[:, sl]).astype(o_ref.dtype)


def gate_merge(oc, osel, ow, proj_hm, gate_blk):
    s, w = oc.shape
    ts = min(256, s)
    spec = pl.BlockSpec((ts, w), lambda i: (i, 0))
    return pl.pallas_call(
        _gate_kernel,
        out_shape=jax.ShapeDtypeStruct((s, w), BF16),
        grid=(s // ts,),
        in_specs=[spec, spec, spec, pl.BlockSpec((1, ts, LANES), lambda i: (gate_blk, i, 0))],
        out_specs=spec,
        name="gate_merge",
        compiler_params=_cparams(("parallel",)),
    )(oc, osel, ow, proj_hm)


def _tile_gain(g, reps, scale=1.0):
    return jnp.tile(g.astype(F32) * scale, reps)


def even_mixer(x, h, w_in, a_qk_gain, a_lambda, a_subln_gain, b_qk_gain, w_out, rel_table, lam_init):
    s = h.shape[0]
    scale = HEAD_DIM ** -0.5
    na = 2 * A_HEADS
    ones = lambda n: jnp.ones((n * HEAD_DIM,), F32)
    zeros = lambda n: jnp.zeros((n * HEAD_DIM,), F32)
    d = h.shape[1]
    wa = na * HEAD_DIM
    table_t = rel_table.astype(F32).T

    def pair_cols(w):
        return w.reshape(d, 2, A_HEADS, HEAD_DIM).transpose(0, 2, 1, 3).reshape(d, wa)

    w_a = jnp.concatenate([pair_cols(w_in[:, 0:wa]), pair_cols(w_in[:, wa:2 * wa]), w_in[:, 2 * wa:3 * wa]],
                          axis=1).astype(BF16)
    gain_a = jnp.concatenate([_tile_gain(a_qk_gain[0], na, scale), _tile_gain(a_qk_gain[1], na), ones(na)])
    flag_a = jnp.concatenate([ones(2 * na), zeros(na)])
    proj_a = proj_headmajor(h, w_a, gain_a, flag_a, tn=512)
    t = min(1024, s)
    nda = min(-(-(FAR - 1) // t) + 2, s // t)
    bias_a = causal_bias_tiles(table_t[:A_HEADS], t, nda)
    ao = flash_attention(
        proj_a, proj_a, proj_a, bias_a, name="diff_attention", t=t, n_groups=A_HEADS, hb=2, nv=2,
        q_map=lambda g, qb: (g, qb, 0),
        k_map=lambda g, kb: (A_HEADS + g, kb, 0),
        v_map=lambda g, kb: (2 * A_HEADS + g, kb, 0),
        b_map=lambda g, bt: (g, bt, 0, 0),
        diff_params=(a_lambda, a_subln_gain, lam_init))

    gain_b = jnp.concatenate([_tile_gain(b_qk_gain[0], B_HEADS, scale), _tile_gain(b_qk_gain[1], B_HEADS),
                              ones(B_HEADS)])
    flag_b = jnp.concatenate([ones(2 * B_HEADS), zeros(B_HEADS)])
    dilations = sorted({dil for _, dil in B_CONFIGS if dil > 1})
    projs = proj_headmajor(h, w_in, gain_b, flag_b, tn=512, col0=3 * wa, dilations=dilations)
    proj_by_dilation = dict(zip([1] + dilations, projs if dilations else [projs]))
    hb0 = 0
    os, lses = [], []
    for window, dilation in B_CONFIGS:
        proj = proj_by_dilation[dilation]
        span = window // dilation
        j = 2 * span - np.arange(3 * span)
        g_b = _bias_by_distance(table_t[A_HEADS:A_HEADS + B_HEADS], j * dilation, (j >= 0) & (j <= span))
        bias_b = toeplitz_tiles(g_b[:, None, :], span, 2 * span, 1, span)
        o, lse = dilated_group(proj, bias_b, dilation, span, hb0, hb0 + 1, hb0 + 2)
        os.append(o)
        lses.append(lse)
    bo = mix_dilated(os, lses)
    return matmul2_residual(ao, bo, w_out, x, tm=1024, tn=512)


def nsa_mixer(x, h, w_in, c_qk_gain, c_cmp_pe, c_cmp_w1, c_cmp_w2, w_out, rel_table):
    s, d = h.shape
    scale = HEAD_DIM ** -0.5
    g = C_GROUPS
    n_in = w_in.shape[1]
    n_pad = -(-n_in // 640) * 640
    ones = lambda n: jnp.ones((n,), F32)
    zeros = lambda n: jnp.zeros((n,), F32)
    kvw = g * HEAD_DIM
    gain = jnp.concatenate([_tile_gain(c_qk_gain[0], C_HEADS, scale), ones(2 * kvw),
                            _tile_gain(c_qk_gain[2], g), ones(kvw), _tile_gain(c_qk_gain[3], g), ones(kvw),
                            ones(n_pad - C_HEADS * HEAD_DIM - 6 * kvw)])
    flag = jnp.concatenate([ones(C_HEADS * HEAD_DIM), zeros(2 * kvw), ones(kvw), zeros(kvw), ones(kvw),
                            zeros(kvw), zeros(n_pad - C_HEADS * HEAD_DIM - 6 * kvw)])
    w_pad = jnp.pad(w_in.astype(BF16), ((0, 0), (0, n_pad - n_in)))
    proj = proj_headmajor(h, w_pad, gain, flag, tn=640)
    kv0 = C_HEADS
    table_t = rel_table.astype(F32).T[:C_HEADS]

    nc = s // CMP_STRIDE
    cw = CMP_STRIDE * HEAD_DIM
    xc = proj[kv0:kv0 + 2 * g].reshape(2, g, nc, cw)
    pe = c_cmp_pe.astype(F32).reshape(2, 2, 1, cw)
    w1 = c_cmp_w1.astype(BF16).reshape(2, 2, cw, c_cmp_w1.shape[-1])
    kvcmp = compress_kv(xc, pe, w1, c_cmp_w2.astype(BF16), c_qk_gain[1])
    if nc % LANES:
        kvcmp = jnp.pad(kvcmp, ((0, 0), (0, 0), (0, LANES - nc % LANES), (0, 0)))
    ncp = kvcmp.shape[2]

    tq = min(256, s)
    n_cmp = (s - CMP_LEN) // CMP_STRIDE + 1
    n_sel = s // SEL_LEN
    nselp = -(-n_sel // LANES) * LANES
    n_top = min(SEL_TOPK, n_sel)
    off = CMP_STRIDE * (LANES - 1)
    far_tile = -(-(FAR + off + CMP_LEN - 1) // tq)
    nd = far_tile + 2
    wx = -(-(tq + off) // LANES) * LANES
    dist_c = np.arange(tq * (nd - 1) + wx) - (tq + off + CMP_LEN - 1)
    vec_c = _bias_by_distance(table_t, dist_c, dist_c >= 0)
    g_c = jnp.stack([vec_c[:, tq * d:tq * d + wx] for d in range(nd)], axis=1)
    bias_c = toeplitz_tiles(g_c, LANES, tq, CMP_STRIDE, off, transpose=True)
    c_start = np.arange(ncp) * CMP_STRIDE
    s_start = np.arange(nselp) * SEL_LEN
    ov = ((c_start[:, None] < s_start[None, :] + SEL_LEN) & (c_start[:, None] + CMP_LEN > s_start[None, :])
          & (np.arange(ncp)[:, None] < n_cmp) & (np.arange(nselp)[None, :] < n_sel))
    oc, sel = cmp_attention(proj, kvcmp, bias_c, jnp.asarray(ov.astype(np.float32), BF16), tq, n_top)

    t = tq
    nds = min(-(-(FAR - 1) // t) + 2, s // t)
    bias_s = causal_bias_tiles(table_t, t, nds)
    key_blk = np.arange(s) // SEL_LEN
    key_neg = np.where(key_blk[:, None] == np.arange(nselp)[None, :], NEG, 0.0).reshape(s // t, t, nselp)
    osel = flash_attention(
        proj, proj, proj, bias_s, name="selected_attention", t=t, n_groups=g, hb=C_HPG, nv=1,
        q_map=lambda gg, qb: (gg, qb, 0),
        k_map=lambda gg, kb: (kv0 + 2 * g + gg, kb, 0),
        v_map=lambda gg, kb: (kv0 + 3 * g + gg, kb, 0),
        b_map=lambda gg, bt: (gg, bt, 0, 0),
        sel=sel, key_neg=jnp.asarray(key_neg.astype(np.float32), BF16))

    look = -(-(C_WINDOW - 1) // t)
    ndw = min(look + 1, s // t)
    bias_w = causal_bias_tiles(table_t, t, ndw, window=C_WINDOW)
    ow = flash_attention(
        proj, proj, proj, bias_w, name="window_attention", t=t, n_groups=g, hb=C_HPG, nv=1, lookback=look,
        q_map=lambda gg, qb: (gg, qb, 0),
        k_map=lambda gg, kb: (kv0 + 4 * g + gg, kb, 0),
        v_map=lambda gg, kb: (kv0 + 5 * g + gg, kb, 0),
        b_map=lambda gg, bt: (gg, bt, 0, 0))

    o = gate_merge(oc, osel, ow, proj, kv0 + 6 * g)
    return matmul_residual(o, w_out, x, tm=1024, tn=512)


def conv_ffn(x, h, w_gate, w_up, layer, conv_w, conv_b, w_down):
    act = ffn_gate_up(h, w_gate, w_up, layer, conv_w, conv_b)
    return matmul_residual(act, w_down, x, tm=1024, tn=256)


def kernel(x, rel_table, ev_norm, ev_w_in, a_qk_gain, a_lambda, a_subln_gain, b_qk_gain, ev_w_out,
           od_norm, od_w_in, c_qk_gain, c_cmp_pe, c_cmp_w1, c_cmp_w2, od_w_out,
           ffn_norm, ffn_w_gate, ffn_w_up, ffn_conv_w, ffn_conv_b, ffn_w_down):
    b, s, d = x.shape
    depth = ffn_norm.shape[0]
    outs = []
    for bi in range(b):
        y = x[bi].astype(F32)
        for i in range(depth):
            if i % 2 == 0:
                e = i // 2
                lam_init = 0.8 - 0.6 * math.exp(-0.3 * i)
                y = even_mixer(y, rmsnorm(y, ev_norm[e]), ev_w_in[e], a_qk_gain[e], a_lambda[e],
                               a_subln_gain[e], b_qk_gain[e], ev_w_out[e], rel_table, lam_init)
            else:
                o = i // 2
                y = nsa_mixer(y, rmsnorm(y, od_norm[o]), od_w_in[o], c_qk_gain[o], c_cmp_pe[o],
                              c_cmp_w1[o], c_cmp_w2[o], od_w_out[o], rel_table)
            y = conv_ffn(y, rmsnorm(y, ffn_norm[i]), ffn_w_gate, ffn_w_up, i, ffn_conv_w[i],
                         ffn_conv_b[i], ffn_w_down[i])
        outs.append(y)
    return jnp.stack(outs).astype(x.dtype)
```

```python
import functools
import math

import numpy as np
import jax
import jax.numpy as jnp
from jax import lax
from jax.experimental import pallas as pl
from jax.experimental.pallas import tpu as pltpu

HEAD_DIM = 128
A_HEADS = 8
B_HEADS = 16
B_CONFIGS = ((128, 1), (512, 4), (2048, 16))
C_HEADS = 32
C_GROUPS = 2
C_HPG = C_HEADS // C_GROUPS
CMP_LEN = 32
CMP_STRIDE = 16
SEL_LEN = 64
SEL_TOPK = 16
C_WINDOW = 512
REL_BUCKETS = 32
REL_MAX_DIST = 2048
EPS = 1e-6
NEG = -1e30
FORCE = 1e9

LANES = 128
VMEM_LIMIT = 56 * 1024 * 1024

F32 = jnp.float32
BF16 = jnp.bfloat16


def _cparams(sem):
    return pltpu.CompilerParams(dimension_semantics=sem, vmem_limit_bytes=VMEM_LIMIT)


def _dot(a, b):
    return jnp.dot(a, b, preferred_element_type=F32)


def _dot_nt(a, b):
    return lax.dot_general(a, b, (((1,), (1,)), ((), ())), preferred_element_type=F32)


def _bucket_np(dist):
    n = np.maximum(dist, 0).astype(np.int32)
    max_exact = REL_BUCKETS // 2
    nf = np.maximum(n, 1).astype(np.float32)
    ratio = np.log(nf / np.float32(max_exact)) / np.float32(math.log(REL_MAX_DIST / max_exact))
    large = np.minimum(max_exact + (ratio * np.float32(REL_BUCKETS - max_exact)).astype(np.int32),
                       REL_BUCKETS - 1)
    return np.where(n < max_exact, n, large).astype(np.int32)


def _far_distance():
    b = _bucket_np(np.arange(4 * REL_MAX_DIST))
    return int(np.max(np.nonzero(b != REL_BUCKETS - 1)[0])) + 1


FAR = _far_distance()


def _bias_by_distance(table_t, dist, valid):
    b = jnp.take(table_t, jnp.asarray(_bucket_np(dist)), axis=1)
    return jnp.where(jnp.asarray(valid), b, NEG).astype(F32)


def _toeplitz_kernel(g_ref, o_ref, *, rows, width, shift, stride, transpose):
    for b in range(g_ref.shape[1]):
        _toeplitz_tile(g_ref[0, b], o_ref.at[0, b], rows, width, shift, stride, transpose)


def _toeplitz_tile(g, o_ref, rows, width, shift, stride, transpose):
    wx = g.shape[1]
    offset = (wx - shift) % wx
    if stride == 1 and not transpose and rows % LANES == 0 and width % LANES == 0 and offset % LANES == 0:
        nr, ncb = rows // LANES, width // LANES
        subs = {}
        for db in range(-(nr - 1), ncb):
            lo = LANES * db + offset - LANES
            gw = jnp.broadcast_to(g[:, lo:lo + 2 * LANES], (LANES, 2 * LANES))
            subs[db] = pltpu.roll(gw, LANES, 1, stride=1, stride_axis=0)[:, :LANES]
        for bi in range(nr):
            for bj in range(ncb):
                o_ref[bi * LANES:(bi + 1) * LANES, bj * LANES:(bj + 1) * LANES] = subs[bj - bi]
        return
    x = jnp.broadcast_to(g, (rows, wx))
    y = pltpu.roll(x, shift, 1, stride=stride, stride_axis=0)[:, :width]
    o_ref[...] = y.T if transpose else y


def toeplitz_tiles(g, rows, width, stride, offset, transpose=False):
    h, nd, wx = g.shape
    assert wx % LANES == 0 and 0 <= offset - stride * (rows - 1) and width - 1 + offset < wx
    shape = (width, rows) if transpose else (rows, width)
    tb = max(b for b in range(1, nd + 1) if nd % b == 0 and b * rows * width * 4 <= max(4 << 20, rows * width * 4))
    return pl.pallas_call(
        functools.partial(_toeplitz_kernel, rows=rows, width=width, shift=(wx - offset) % wx, stride=stride,
                          transpose=transpose),
        out_shape=jax.ShapeDtypeStruct((h, nd) + shape, F32),
        grid=(h, nd // tb),
        in_specs=[pl.BlockSpec((1, tb, 1, wx), lambda a, b: (a, b, 0, 0))],
        out_specs=pl.BlockSpec((1, tb) + shape, lambda a, b: (a, b, 0, 0)),
        name="toeplitz_tiles",
        compiler_params=_cparams(("parallel", "parallel")),
    )(g.reshape(h, nd, 1, wx))


def causal_bias_tiles(table_t, t, n_tiles, window=None):
    span = (n_tiles + 1) * t
    dist = (n_tiles * t) - np.arange(span)
    valid = dist >= 0 if window is None else (dist >= 0) & (dist < window)
    vrev = _bias_by_distance(table_t, dist, valid)
    g = jnp.stack([vrev[:, (n_tiles - d - 1) * t:(n_tiles - d + 1) * t] for d in range(n_tiles)], axis=1)
    return toeplitz_tiles(g, t, t, 1, t)


def _rmsnorm_kernel(x_ref, g_ref, o_ref):
    x = x_ref[...]
    ms = jnp.mean(x * x, axis=-1, keepdims=True)
    o_ref[...] = ((x * lax.rsqrt(ms + EPS)) * g_ref[...]).astype(o_ref.dtype)


def rmsnorm(x, gain):
    s, d = x.shape
    tr = min(256, s)
    return pl.pallas_call(
        _rmsnorm_kernel,
        out_shape=jax.ShapeDtypeStruct((s, d), BF16),
        grid=(s // tr,),
        in_specs=[pl.BlockSpec((tr, d), lambda i: (i, 0)), pl.BlockSpec((1, d), lambda i: (0, 0))],
        out_specs=pl.BlockSpec((tr, d), lambda i: (i, 0)),
        name="rmsnorm",
        compiler_params=_cparams(("parallel",)),
    )(x, gain.reshape(1, d).astype(F32))


ROW_SUB = 256


def col_blocks(w, tn):
    return w.astype(BF16)[None]


def _proj_kernel(x_ref, w_ref, gain_ref, flag_ref, *refs, nblk, ts, dilations):
    perm_refs = refs[:len(dilations)]
    o_ref = refs[len(dilations)]
    od_refs = refs[len(dilations) + 1:]
    w = w_ref[0].astype(BF16)
    for r0 in range(0, x_ref.shape[0], ts):
        acc = _dot(x_ref[r0:r0 + ts, :], w)
        ys = []
        for c in range(nblk):
            sl = slice(c * LANES, (c + 1) * LANES)
            blk = acc[:, sl]
            ms = jnp.mean(blk * blk, axis=-1, keepdims=True)
            r = jnp.where(flag_ref[:, sl] > 0, lax.rsqrt(ms + EPS), 1.0)
            y = ((blk * r) * gain_ref[:, sl]).astype(o_ref.dtype)
            o_ref[c, r0:r0 + ts, :] = y
            ys.append(y)
        for d, perm_ref, od_ref in zip(dilations, perm_refs, od_refs):
            yp = _dot(perm_ref[...], jnp.concatenate(ys, axis=-1)).astype(o_ref.dtype)
            m = ts // d
            for c in range(nblk):
                for res in range(d):
                    od_ref[c, r0 // d:r0 // d + m, res * LANES:(res + 1) * LANES] = (
                        yp[res * m:(res + 1) * m, c * LANES:(c + 1) * LANES])


def proj_headmajor(x, w, gain, flag, tn, col0=0, dilations=()):
    s, k = x.shape
    n = gain.shape[0]
    tm = min(2048 if w.dtype == BF16 else 1024, s)
    ts = min(ROW_SUB, tm)
    nblk = tn // LANES
    cb0 = col0 // tn
    perms = []
    for d in dilations:
        dst = np.arange(ts)
        src = (dst % (ts // d)) * d + dst // (ts // d)
        perms.append(jnp.asarray((src[:, None] == np.arange(ts)[None, :]).astype(np.float32), BF16))
    out_shape = [jax.ShapeDtypeStruct((n // LANES, s, LANES), BF16)]
    out_specs = [pl.BlockSpec((nblk, tm, LANES), lambda i, j: (j, i, 0))]
    for d in dilations:
        out_shape.append(jax.ShapeDtypeStruct((n // LANES, s // d, d * LANES), BF16))
        out_specs.append(pl.BlockSpec((nblk, tm // d, d * LANES), lambda i, j: (j, i, 0)))
    outs = pl.pallas_call(
        functools.partial(_proj_kernel, nblk=nblk, ts=ts, dilations=tuple(dilations)),
        out_shape=out_shape,
        grid=(s // tm, n // tn),
        in_specs=[pl.BlockSpec((tm, k), lambda i, j: (i, 0)),
                  pl.BlockSpec((1, k, tn), lambda i, j: (0, 0, cb0 + j)),
                  pl.BlockSpec((1, tn), lambda i, j: (0, j)),
                  pl.BlockSpec((1, tn), lambda i, j: (0, j))]
                 + [pl.BlockSpec((ts, ts), lambda i, j: (0, 0))] * len(dilations),
        out_specs=out_specs,
        name="proj_headmajor",
        compiler_params=_cparams(("parallel", "parallel")),
    )(x, w[None], gain.reshape(1, n).astype(F32), flag.reshape(1, n).astype(F32), *perms)
    return outs if dilations else outs[0]


def _mm_res_kernel(x_ref, w_ref, r_ref, o_ref):
    o_ref[...] = r_ref[...] + _dot(x_ref[...], w_ref[0])


def matmul_residual(x, w, res, tm, tn):
    s, k = x.shape
    n = w.shape[1]
    tm = min(tm, s)
    tn = min(tn, n)
    return pl.pallas_call(
        _mm_res_kernel,
        out_shape=jax.ShapeDtypeStruct((s, n), F32),
        grid=(s // tm, n // tn),
        in_specs=[pl.BlockSpec((tm, k), lambda i, j: (i, 0), pipeline_mode=pl.Buffered(1)),
                  pl.BlockSpec((1, k, tn), lambda i, j: (0, 0, j)),
                  pl.BlockSpec((tm, tn), lambda i, j: (i, j))],
        out_specs=pl.BlockSpec((tm, tn), lambda i, j: (i, j)),
        name="matmul_residual",
        compiler_params=_cparams(("parallel", "arbitrary")),
    )(x, col_blocks(w, tn), res)


def _mm2_res_kernel(x1_ref, x2_ref, w_ref, r_ref, o_ref, *, k1):
    acc = _dot(x1_ref[...], w_ref[0, 0:k1, :]) + _dot(x2_ref[...], w_ref[0, k1:, :])
    o_ref[...] = r_ref[...] + acc


def matmul2_residual(x1, x2, w, res, tm, tn):
    s, k1 = x1.shape
    k2 = x2.shape[1]
    n = w.shape[1]
    tm = min(tm, s)
    tn = min(tn, n)
    return pl.pallas_call(
        functools.partial(_mm2_res_kernel, k1=k1),
        out_shape=jax.ShapeDtypeStruct((s, n), F32),
        grid=(s // tm, n // tn),
        in_specs=[pl.BlockSpec((tm, k1), lambda i, j: (i, 0)),
                  pl.BlockSpec((tm, k2), lambda i, j: (i, 0)),
                  pl.BlockSpec((1, k1 + k2, tn), lambda i, j: (0, 0, j)),
                  pl.BlockSpec((tm, tn), lambda i, j: (i, j))],
        out_specs=pl.BlockSpec((tm, tn), lambda i, j: (i, j)),
        name="matmul2_residual",
        compiler_params=_cparams(("parallel", "arbitrary")),
    )(x1, x2, col_blocks(w, tn), res)


def _ffn1_kernel(x_ref, wg_ref, wu_ref, cw_ref, cb_ref, o_ref, carry_ref, *, ts):
    i = pl.program_id(0)
    j = pl.program_id(1)

    @pl.when(i == 0)
    def _():
        carry_ref[j] = jnp.zeros(carry_ref.shape[1:], F32)

    prev = carry_ref[j]
    wg = wg_ref[0].astype(BF16)
    wu = wu_ref[0].astype(BF16)
    cw = cw_ref[...]
    cb = cb_ref[...]
    row = lax.broadcasted_iota(jnp.int32, (ts, wg.shape[1]), 0)
    for r0 in range(0, x_ref.shape[0], ts):
        x = x_ref[r0:r0 + ts, :]
        g = _dot(x, wg)
        u = _dot(x, wu)
        p7 = prev[7:8, :]
        p6 = prev[6:7, :]
        g1 = jnp.where(row == 0, p7, pltpu.roll(g, 1, 0))
        g2 = jnp.where(row == 0, p6, jnp.where(row == 1, p7, pltpu.roll(g, 2, 0)))
        gc = g2 * cw[0:1, :] + g1 * cw[1:2, :] + g * cw[2:3, :] + cb
        o_ref[r0:r0 + ts, :] = ((gc * jax.nn.sigmoid(gc)) * u).astype(o_ref.dtype)
        prev = g[ts - 8:ts, :]
    carry_ref[j] = prev


def ffn_gate_up(h, wg, wu, layer, conv_w, conv_b):
    s, d = h.shape
    f = wg.shape[2]
    tm = min(2048, s)
    tn = 256
    nj = f // tn
    return pl.pallas_call(
        functools.partial(_ffn1_kernel, ts=min(ROW_SUB, tm)),
        out_shape=jax.ShapeDtypeStruct((s, f), BF16),
        grid=(s // tm, nj),
        in_specs=[pl.BlockSpec((tm, d), lambda i, j: (i, 0), pipeline_mode=pl.Buffered(1)),
                  pl.BlockSpec((1, d, tn), lambda i, j: (layer, 0, j)),
                  pl.BlockSpec((1, d, tn), lambda i, j: (layer, 0, j)),
                  pl.BlockSpec((3, tn), lambda i, j: (0, j)),
                  pl.BlockSpec((1, tn), lambda i, j: (0, j))],
        out_specs=pl.BlockSpec((tm, tn), lambda i, j: (i, j)),
        scratch_shapes=[pltpu.VMEM((nj, 8, tn), F32)],
        name="ffn_gate_up",
        compiler_params=_cparams(("arbitrary", "arbitrary")),
    )(h, wg, wu, conv_w.astype(F32), conv_b.reshape(1, f).astype(F32))


def _flash_kernel(qi_ref, kj_ref, bi_ref, fl_ref, *refs, hb, nv, use_sel, rc, diff_lam_init):
    q_ref, k_ref, v_ref, b_ref = refs[:4]
    refs = refs[4:]
    if use_sel:
        sel_ref, en_ref = refs[:2]
        refs = refs[2:]
    if diff_lam_init is not None:
        lam_ref, gain_ref = refs[:2]
        refs = refs[2:]
    o_ref, m_sc, acc_sc, s_sc, p_sc, al_sc = refs[:6]
    if use_sel:
        qa_sc = refs[6]
    per_head_k = k_ref.shape[0] == hb and hb > 1
    per_head_bias = b_ref.shape[0] == hb
    p = pl.program_id(1)
    flags = fl_ref[p]
    dv = nv * LANES
    t = q_ref.shape[1]
    tk = k_ref.shape[1]

    @pl.when((flags & 1) != 0)
    def _():
        m_sc[...] = jnp.full(m_sc.shape, -jnp.inf, F32)
        acc_sc[...] = jnp.zeros(acc_sc.shape, F32)
        if use_sel:
            notsel = (1.0 - sel_ref[0].astype(F32)).astype(BF16)
            for h in range(hb):
                qa_sc[h, :, 0:LANES] = q_ref[h]
                qa_sc[h, :, LANES:] = notsel

    v_aug = jnp.concatenate([v_ref[c] for c in range(nv)] + [jnp.ones((tk, LANES), BF16)], axis=-1)

    for h in range(hb):
        k = k_ref[h if per_head_k else 0]
        if use_sel:
            k = jnp.concatenate([k, en_ref[0]], axis=-1)
        qh = qa_sc[h] if use_sel else q_ref[h]
        s_sc[h] = _dot_nt(qh, k)
        hbias = h if per_head_bias else 0
        for r0 in range(0, t, rc):
            rows = slice(r0, r0 + rc)
            s = s_sc[h, rows, :] + b_ref[hbias, 0, rows, :]
            m_old = m_sc[h, rows, :]
            m_new = jnp.maximum(m_old, jnp.max(s, axis=-1, keepdims=True))
            al_sc[h, rows, :] = jnp.exp(m_old - m_new)
            m_sc[h, rows, :] = m_new
            p_sc[h, rows, :] = jnp.exp(s - jnp.tile(m_new, (1, tk // LANES))).astype(BF16)
        acc_sc[h] = jnp.tile(al_sc[h], (1, nv + 1)) * acc_sc[h] + _dot(p_sc[h], v_aug)

    def normalised(h):
        acc = acc_sc[h]
        return acc[:, 0:dv] / jnp.tile(acc[:, dv:], (1, nv))

    @pl.when((flags & 2) != 0)
    def _():
        if diff_lam_init is None:
            for h in range(hb):
                o_ref[:, h * dv:(h + 1) * dv] = normalised(h).astype(o_ref.dtype)
        else:
            lf = lam_ref[...]
            s1 = jnp.sum(lf[0:1, :] * lf[1:2, :], axis=-1, keepdims=True)
            s2 = jnp.sum(lf[2:3, :] * lf[3:4, :], axis=-1, keepdims=True)
            lam = jnp.exp(s1) - jnp.exp(s2) + diff_lam_init
            o = normalised(0) - lam * normalised(1)
            ms = jnp.mean(o * o, axis=-1, keepdims=True)
            y = ((o * lax.rsqrt(ms + EPS)) * gain_ref[...]) * (1.0 - diff_lam_init)
            o_ref[...] = y.astype(o_ref.dtype)


def _pairs(nq, lookback, max_bias):
    qi, kj, bi, fl = [], [], [], []
    for q in range(nq):
        lo = 0 if lookback is None else max(0, q - lookback)
        for k in range(lo, q + 1):
            qi.append(q)
            kj.append(k)
            bi.append(min(q - k, max_bias))
            fl.append((1 if k == lo else 0) | (2 if k == q else 0))
    return [jnp.asarray(np.asarray(a, np.int32)) for a in (qi, kj, bi, fl)]


def flash_attention(q, k, v, bias, *, name, t, n_groups, hb, nv, q_map, k_map, v_map, b_map,
                    lookback=None, sel=None, key_neg=None, diff_params=None):
    s = q.shape[1]
    nq = s // t
    qi, kj, bi, fl = _pairs(nq, lookback, bias.shape[1] - 1)
    n_pairs = int(qi.shape[0])
    dv = nv * LANES
    use_sel = sel is not None
    diff = diff_params is not None
    in_specs = [
        pl.BlockSpec((hb, t, LANES), lambda g, p, qi, kj, bi, fl: q_map(g, qi[p])),
        pl.BlockSpec((hb if diff else 1, t, LANES), lambda g, p, qi, kj, bi, fl: k_map(g, kj[p])),
        pl.BlockSpec((nv, t, LANES), lambda g, p, qi, kj, bi, fl: v_map(g, kj[p])),
        pl.BlockSpec((1 if diff else hb, 1, t, t), lambda g, p, qi, kj, bi, fl: b_map(g, bi[p])),
    ]
    args = [q, k, v, bias]
    scratch = [pltpu.VMEM((hb, t, LANES), F32), pltpu.VMEM((hb, t, dv + LANES), F32),
               pltpu.VMEM((hb, t, t), F32), pltpu.VMEM((hb, t, t), BF16), pltpu.VMEM((hb, t, LANES), F32)]
    if use_sel:
        nselp = sel.shape[2]
        in_specs += [
            pl.BlockSpec((1, t, nselp), lambda g, p, qi, kj, bi, fl: (g, qi[p], 0)),
            pl.BlockSpec((1, t, nselp), lambda g, p, qi, kj, bi, fl: (kj[p], 0, 0)),
        ]
        args += [sel, key_neg]
        scratch.append(pltpu.VMEM((hb, t, LANES + nselp), BF16))
    lam_init = None
    if diff:
        a_lambda, subln_gain, lam_init = diff_params
        in_specs += [
            pl.BlockSpec((4, HEAD_DIM), lambda g, p, qi, kj, bi, fl: (0, 0)),
            pl.BlockSpec((1, dv), lambda g, p, qi, kj, bi, fl: (0, 0)),
        ]
        args += [a_lambda.astype(F32), subln_gain.reshape(1, dv).astype(F32)]
    out_w = dv if diff else hb * dv
    grid_spec = pltpu.PrefetchScalarGridSpec(
        num_scalar_prefetch=4,
        grid=(n_groups, n_pairs),
        in_specs=in_specs,
        out_specs=pl.BlockSpec((t, out_w), lambda g, p, qi, kj, bi, fl: (qi[p], g)),
        scratch_shapes=scratch,
    )
    rc = max(8, min(t, (16 * 8 * LANES) // t))
    return pl.pallas_call(
        functools.partial(_flash_kernel, hb=hb, nv=nv, use_sel=use_sel, rc=rc, diff_lam_init=lam_init),
        out_shape=jax.ShapeDtypeStruct((s, n_groups * out_w), BF16),
        grid_spec=grid_spec,
        name=name,
        compiler_params=_cparams(("parallel", "arbitrary")),
    )(qi, kj, bi, fl, *args)


def _dilated_kernel(q_ref, kp_ref, kc_ref, vp_ref, vc_ref, b_ref, o_ref, lse_ref, s_sc, p_sc, m_sc):
    span = q_ref.shape[1]
    rc = min(64, span)
    no_prev = jnp.where(pl.program_id(1) == 0, NEG, 0.0)
    col = lax.broadcasted_iota(jnp.int32, (1, 2 * span), 1)
    prev_mask = jnp.where(col < span, no_prev, 0.0)
    ones = jnp.ones((2 * span, LANES), BF16)
    for h in range(B_HEADS):
        keys = jnp.concatenate([kp_ref[h], kc_ref[h]], axis=0)
        s_sc[h] = _dot_nt(q_ref[h], keys)
        for r0 in range(0, span, rc):
            rows = slice(r0, r0 + rc)
            s = s_sc[h, rows, :] + (b_ref[h, 0, rows, :] + prev_mask)
            m = jnp.max(s, axis=-1, keepdims=True)
            m_sc[h, rows, :] = jnp.broadcast_to(m, (rc, LANES))
            p_sc[h, rows, :] = jnp.exp(s - m).astype(BF16)
        vals = jnp.concatenate([jnp.concatenate([vp_ref[h], vc_ref[h]], axis=0), ones], axis=-1)
        pv = _dot(p_sc[h], vals)
        l = pv[:, HEAD_DIM:]
        sl = slice(h * HEAD_DIM, (h + 1) * HEAD_DIM)
        o_ref[:, sl] = (pv[:, 0:HEAD_DIM] / l).astype(o_ref.dtype)
        lse_ref[:, sl] = m_sc[h] + jnp.log(l)


def dilated_group(x, bias, dilation, span, q_blk, k_blk, v_blk):
    nh, l, _ = x.shape
    s = l * dilation
    nb = l // span
    w = B_HEADS * HEAD_DIM
    hspec = lambda blk, prev: pl.BlockSpec(
        (B_HEADS, span, LANES),
        (lambda r, n: (blk, jnp.maximum(n - 1, 0), r)) if prev else (lambda r, n: (blk, n, r)))
    o, lse = pl.pallas_call(
        _dilated_kernel,
        out_shape=[jax.ShapeDtypeStruct((l, dilation * w), BF16), jax.ShapeDtypeStruct((l, dilation * w), F32)],
        grid=(dilation, nb),
        in_specs=[hspec(q_blk, False), hspec(k_blk, True), hspec(k_blk, False),
                  hspec(v_blk, True), hspec(v_blk, False),
                  pl.BlockSpec((B_HEADS, 1, span, 2 * span), lambda r, n: (0, 0, 0, 0))],
        out_specs=[pl.BlockSpec((span, w), lambda r, n: (n, r))] * 2,
        scratch_shapes=[pltpu.VMEM((B_HEADS, span, 2 * span), F32),
                        pltpu.VMEM((B_HEADS, span, 2 * span), BF16),
                        pltpu.VMEM((B_HEADS, span, LANES), F32)],
        name="dilated_group",
        compiler_params=_cparams(("parallel", "parallel")),
    )(x, x, x, x, x, bias)
    return o.reshape(s, w), lse.reshape(s, w)


def _mix3_kernel(o1, o2, o3, l1, l2, l3, out_ref):
    a, b, c = l1[...], l2[...], l3[...]
    m = jnp.maximum(jnp.maximum(a, b), c)
    ea, eb, ec = jnp.exp(a - m), jnp.exp(b - m), jnp.exp(c - m)
    num = ea * o1[...].astype(F32) + eb * o2[...].astype(F32) + ec * o3[...].astype(F32)
    out_ref[...] = (num / (ea + eb + ec)).astype(out_ref.dtype)


def mix_dilated(os, lses):
    s, w = os[0].shape
    ts = min(256, s)
    spec = pl.BlockSpec((ts, w), lambda i: (i, 0))
    return pl.pallas_call(
        _mix3_kernel,
        out_shape=jax.ShapeDtypeStruct((s, w), BF16),
        grid=(s // ts,),
        in_specs=[spec] * 6,
        out_specs=spec,
        name="mix_dilated",
        compiler_params=_cparams(("parallel",)),
    )(*os, *lses)


def _compress_kernel(x_ref, pe_ref, w1_ref, w2_ref, gain_ref, o_ref):
    kv = pl.program_id(0)
    x = x_ref[0, 0].astype(F32)
    nc = x.shape[0]
    a = _dot((x + pe_ref[0, 0]).astype(BF16), w1_ref[0, 0])
    b = _dot((x + pe_ref[0, 1]).astype(BF16), w1_ref[0, 1])
    hid = jax.nn.gelu(a + pltpu.roll(b, nc - 1, 0))
    c = _dot(hid.astype(BF16), w2_ref[0])
    ms = jnp.mean(c * c, axis=-1, keepdims=True)
    normed = (c * lax.rsqrt(ms + EPS)) * gain_ref[...]
    o_ref[0, 0] = jnp.where(kv == 0, normed, c).astype(o_ref.dtype)


def compress_kv(xc, pe, w1, w2, gain):
    _, g, nc, cw = xc.shape
    hid = w1.shape[-1]
    return pl.pallas_call(
        _compress_kernel,
        out_shape=jax.ShapeDtypeStruct((2, g, nc, HEAD_DIM), BF16),
        grid=(2, g),
        in_specs=[pl.BlockSpec((1, 1, nc, cw), lambda a, b: (a, b, 0, 0)),
                  pl.BlockSpec((1, 2, 1, cw), lambda a, b: (a, 0, 0, 0)),
                  pl.BlockSpec((1, 2, cw, hid), lambda a, b: (a, 0, 0, 0)),
                  pl.BlockSpec((1, hid, HEAD_DIM), lambda a, b: (a, 0, 0)),
                  pl.BlockSpec((1, HEAD_DIM), lambda a, b: (0, 0))],
        out_specs=pl.BlockSpec((1, 1, nc, HEAD_DIM), lambda a, b: (a, b, 0, 0)),
        name="compress_kv",
        compiler_params=_cparams(("parallel", "parallel")),
    )(xc, pe, w1, w2, gain.reshape(1, HEAD_DIM).astype(F32))


def _cmp_kernel(q_ref, k_ref, v_ref, *rest, nkt, n_top):
    bias_refs = rest[:nkt]
    ov_ref, oc_ref, sel_ref, imp_sc, s_sc, p_sc = rest[nkt:]
    qi = pl.program_id(1)
    tq = q_ref.shape[1]
    ncp = k_ref.shape[2]
    rc = max(8, min(tq, (16 * 8 * LANES) // ncp))
    k = k_ref[0, 0]
    v = v_ref[0, 0]
    for h in range(C_HPG):
        s_sc[h] = _dot_nt(q_ref[h], k)
        for r0 in range(0, tq, rc):
            rows = slice(r0, r0 + rc)
            if nkt == 1:
                bias = bias_refs[0][h, 0, rows, :]
            else:
                bias = jnp.concatenate([b[h, 0, rows, :] for b in bias_refs], axis=-1)
            s = s_sc[h, rows, :] + bias
            m = jnp.max(s, axis=-1, keepdims=True)
            e = jnp.exp(s - m)
            pc = e * jnp.where(m > 0.5 * NEG, 1.0 / jnp.sum(e, axis=-1, keepdims=True), 0.0)
            p_sc[h, rows, :] = pc.astype(BF16)
            if h == 0:
                imp_sc[rows, :] = pc
            else:
                imp_sc[rows, :] += pc
        oc_ref[:, h * HEAD_DIM:(h + 1) * HEAD_DIM] = _dot(p_sc[h], v).astype(oc_ref.dtype)
    imp = imp_sc[...]
    hi = imp.astype(BF16)
    lo = (imp - hi.astype(F32)).astype(BF16)
    ov = ov_ref[...]
    impn = _dot(hi, ov) + _dot(lo, ov)
    score_in = impn.T
    shape = score_in.shape
    t = qi * tq + lax.broadcasted_iota(jnp.int32, shape, 1)
    n = lax.broadcasted_iota(jnp.int32, shape, 0)
    cur = lax.shift_right_logical(t, int(math.log2(SEL_LEN)))
    forced = (n == 0) | (n == cur) | (n == cur - 1)
    valid = n <= cur
    score = jnp.where(valid, jnp.where(forced, FORCE, score_in), NEG)
    nf = n.astype(F32)
    sel = jnp.zeros(shape, F32)
    for _ in range(n_top):
        m = jnp.max(score, axis=0, keepdims=True)
        first = jnp.min(jnp.where(score == m, nf, 1e9), axis=0, keepdims=True)
        pick = nf == first
        sel = jnp.where(pick, 1.0, sel)
        score = jnp.where(pick, -jnp.inf, score)
    sel_ref[0] = jnp.where(valid, sel, 0.0).T.astype(sel_ref.dtype)


def cmp_attention(q_hm, kvcmp, bias, overlap, tq, n_top):
    s = q_hm.shape[1]
    ncp = kvcmp.shape[2]
    nkt = ncp // LANES
    nselp = overlap.shape[1]
    nd = bias.shape[1]
    per_tile = (LANES * CMP_STRIDE) // tq

    def bias_spec(kt):
        return pl.BlockSpec(
            (C_HPG, 1, tq, LANES),
            lambda g, i: (g, jnp.clip(i - per_tile * kt, -1, nd - 2) + 1, 0, 0))

    w = C_HEADS * HEAD_DIM
    return pl.pallas_call(
        functools.partial(_cmp_kernel, nkt=nkt, n_top=n_top),
        out_shape=[jax.ShapeDtypeStruct((s, w), BF16),
                   jax.ShapeDtypeStruct((C_GROUPS, s, nselp), BF16)],
        grid=(C_GROUPS, s // tq),
        in_specs=[pl.BlockSpec((C_HPG, tq, LANES), lambda g, i: (g, i, 0)),
                  pl.BlockSpec((1, 1, ncp, LANES), lambda g, i: (0, g, 0, 0)),
                  pl.BlockSpec((1, 1, ncp, LANES), lambda g, i: (1, g, 0, 0))]
                 + [bias_spec(kt) for kt in range(nkt)]
                 + [pl.BlockSpec((ncp, nselp), lambda g, i: (0, 0))],
        out_specs=[pl.BlockSpec((tq, w // C_GROUPS), lambda g, i: (i, g)),
                   pl.BlockSpec((1, tq, nselp), lambda g, i: (g, i, 0))],
        scratch_shapes=[pltpu.VMEM((tq, ncp), F32), pltpu.VMEM((C_HPG, tq, ncp), F32),
                        pltpu.VMEM((C_HPG, tq, ncp), BF16)],
        name="cmp_attention",
        compiler_params=_cparams(("parallel", "parallel")),
    )(q_hm, kvcmp, kvcmp, *([bias] * nkt), overlap)


def _gate_kernel(oc_ref, os_ref, ow_ref, gt_ref, o_ref):
    gt = jax.nn.sigmoid(gt_ref[0].astype(F32))
    for h in range(C_HEADS):
        sl = slice(h * HEAD_DIM, (h + 1) * HEAD_DIM)
        g0 = gt[:, h:h + 1]
        g1 = gt[:, C_HEADS + h:C_HEADS + h + 1]
        g2 = gt[:, 2 * C_HEADS + h:2 * C_HEADS + h + 1]
        mixed = (g0 * oc_ref[:, sl].astype(F32) + g1 * os_ref[:, sl].astype(F32)
                 + g2 * ow_ref[:, sl].astype(F32))
        o_ref[:, sl] = mixed.astype(o_ref.dtype)


def gate_merge(oc, osel, ow, proj_hm, gate_blk):
    s, w = oc.shape
    ts = min(256, s)
    spec = pl.BlockSpec((ts, w), lambda i: (i, 0))
    return pl.pallas_call(
        _gate_kernel,
        out_shape=jax.ShapeDtypeStruct((s, w), BF16),
        grid=(s // ts,),
        in_specs=[spec, spec, spec, pl.BlockSpec((1, ts, LANES), lambda i: (gate_blk, i, 0))],
        out_specs=spec,
        name="gate_merge",
        compiler_params=_cparams(("parallel",)),
    )(oc, osel, ow, proj_hm)


def _tile_gain(g, reps, scale=1.0):
    return jnp.tile(g.astype(F32) * scale, reps)


def even_mixer(x, h, w_in, a_qk_gain, a_lambda, a_subln_gain, b_qk_gain, w_out, rel_table, lam_init):
    s = h.shape[0]
    scale = HEAD_DIM ** -0.5
    na = 2 * A_HEADS
    ones = lambda n: jnp.ones((n * HEAD_DIM,), F32)
    zeros = lambda n: jnp.zeros((n * HEAD_DIM,), F32)
    d = h.shape[1]
    wa = na * HEAD_DIM
    table_t = rel_table.astype(F32).T

    def pair_cols(w):
        return w.reshape(d, 2, A_HEADS, HEAD_DIM).transpose(0, 2, 1, 3).reshape(d, wa)

    w_a = jnp.concatenate([pair_cols(w_in[:, 0:wa]), pair_cols(w_in[:, wa:2 * wa]), w_in[:, 2 * wa:3 * wa]],
                          axis=1).astype(BF16)
    gain_a = jnp.concatenate([_tile_gain(a_qk_gain[0], na, scale), _tile_gain(a_qk_gain[1], na), ones(na)])
    flag_a = jnp.concatenate([ones(2 * na), zeros(na)])
    proj_a = proj_headmajor(h, w_a, gain_a, flag_a, tn=512)
    t = min(1024, s)
    nda = min(-(-(FAR - 1) // t) + 2, s // t)
    bias_a = causal_bias_tiles(table_t[:A_HEADS], t, nda)
    ao = flash_attention(
        proj_a, proj_a, proj_a, bias_a, name="diff_attention", t=t, n_groups=A_HEADS, hb=2, nv=2,
        q_map=lambda g, qb: (g, qb, 0),
        k_map=lambda g, kb: (A_HEADS + g, kb, 0),
        v_map=lambda g, kb: (2 * A_HEADS + g, kb, 0),
        b_map=lambda g, bt: (g, bt, 0, 0),
        diff_params=(a_lambda, a_subln_gain, lam_init))

    gain_b = jnp.concatenate([_tile_gain(b_qk_gain[0], B_HEADS, scale), _tile_gain(b_qk_gain[1], B_HEADS),
                              ones(B_HEADS)])
    flag_b = jnp.concatenate([ones(2 * B_HEADS), zeros(B_HEADS)])
    dilations = sorted({dil for _, dil in B_CONFIGS if dil > 1})
    projs = proj_headmajor(h, w_in, gain_b, flag_b, tn=512, col0=3 * wa, dilations=dilations)
    proj_by_dilation = dict(zip([1] + dilations, projs if dilations else [projs]))
    hb0 = 0
    os, lses = [], []
    for window, dilation in B_CONFIGS:
        proj = proj_by_dilation[dilation]
        span = window // dilation
        j = 2 * span - np.arange(3 * span)
        g_b = _bias_by_distance(table_t[A_HEADS:A_HEADS + B_HEADS], j * dilation, (j >= 0) & (j <= span))
        bias_b = toeplitz_tiles(g_b[:, None, :], span, 2 * span, 1, span)
        o, lse = dilated_group(proj, bias_b, dilation, span, hb0, hb0 + 1, hb0 + 2)
        os.append(o)
        lses.append(lse)
    bo = mix_dilated(os, lses)
    return matmul2_residual(ao, bo, w_out, x, tm=1024, tn=512)


def nsa_mixer(x, h, w_in, c_qk_gain, c_cmp_pe, c_cmp_w1, c_cmp_w2, w_out, rel_table):
    s, d = h.shape
    scale = HEAD_DIM ** -0.5
    g = C_GROUPS
    n_in = w_in.shape[1]
    n_pad = -(-n_in // 640) * 640
    ones = lambda n: jnp.ones((n,), F32)
    zeros = lambda n: jnp.zeros((n,), F32)
    kvw = g * HEAD_DIM
    gain = jnp.concatenate([_tile_gain(c_qk_gain[0], C_HEADS, scale), ones(2 * kvw),
                            _tile_gain(c_qk_gain[2], g), ones(kvw), _tile_gain(c_qk_gain[3], g), ones(kvw),
                            ones(n_pad - C_HEADS * HEAD_DIM - 6 * kvw)])
    flag = jnp.concatenate([ones(C_HEADS * HEAD_DIM), zeros(2 * kvw), ones(kvw), zeros(kvw), ones(kvw),
                            zeros(kvw), zeros(n_pad - C_HEADS * HEAD_DIM - 6 * kvw)])
    w_pad = jnp.pad(w_in.astype(BF16), ((0, 0), (0, n_pad - n_in)))
    proj = proj_headmajor(h, w_pad, gain, flag, tn=640)
    kv0 = C_HEADS
    table_t = rel_table.astype(F32).T[:C_HEADS]

    nc = s // CMP_STRIDE
    cw = CMP_STRIDE * HEAD_DIM
    xc = proj[kv0:kv0 + 2 * g].reshape(2, g, nc, cw)
    pe = c_cmp_pe.astype(F32).reshape(2, 2, 1, cw)
    w1 = c_cmp_w1.astype(BF16).reshape(2, 2, cw, c_cmp_w1.shape[-1])
    kvcmp = compress_kv(xc, pe, w1, c_cmp_w2.astype(BF16), c_qk_gain[1])
    if nc % LANES:
        kvcmp = jnp.pad(kvcmp, ((0, 0), (0, 0), (0, LANES - nc % LANES), (0, 0)))
    ncp = kvcmp.shape[2]

    tq = min(256, s)
    n_cmp = (s - CMP_LEN) // CMP_STRIDE + 1
    n_sel = s // SEL_LEN
    nselp = -(-n_sel // LANES) * LANES
    n_top = min(SEL_TOPK, n_sel)
    off = CMP_STRIDE * (LANES - 1)
    far_tile = -(-(FAR + off + CMP_LEN - 1) // tq)
    nd = far_tile + 2
    wx = -(-(tq + off) // LANES) * LANES
    dist_c = np.arange(tq * (nd - 1) + wx) - (tq + off + CMP_LEN - 1)
    vec_c = _bias_by_distance(table_t, dist_c, dist_c >= 0)
    g_c = jnp.stack([vec_c[:, tq * d:tq * d + wx] for d in range(nd)], axis=1)
    bias_c = toeplitz_tiles(g_c, LANES, tq, CMP_STRIDE, off, transpose=True)
    c_start = np.arange(ncp) * CMP_STRIDE
    s_start = np.arange(nselp) * SEL_LEN
    ov = ((c_start[:, None] < s_start[None, :] + SEL_LEN) & (c_start[:, None] + CMP_LEN > s_start[None, :])
          & (np.arange(ncp)[:, None] < n_cmp) & (np.arange(nselp)[None, :] < n_sel))
    oc, sel = cmp_attention(proj, kvcmp, bias_c, jnp.asarray(ov.astype(np.float32), BF16), tq, n_top)

    t = tq
    nds = min(-(-(FAR - 1) // t) + 2, s // t)
    bias_s = causal_bias_tiles(table_t, t, nds)
    key_blk = np.arange(s) // SEL_LEN
    key_neg = np.where(key_blk[:, None] == np.arange(nselp)[None, :], NEG, 0.0).reshape(s // t, t, nselp)
    osel = flash_attention(
        proj, proj, proj, bias_s, name="selected_attention", t=t, n_groups=g, hb=C_HPG, nv=1,
        q_map=lambda gg, qb: (gg, qb, 0),
        k_map=lambda gg, kb: (kv0 + 2 * g + gg, kb, 0),
        v_map=lambda gg, kb: (kv0 + 3 * g + gg, kb, 0),
        b_map=lambda gg, bt: (gg, bt, 0, 0),
        sel=sel, key_neg=jnp.asarray(key_neg.astype(np.float32), BF16))

    look = -(-(C_WINDOW - 1) // t)
    ndw = min(look + 1, s // t)
    bias_w = causal_bias_tiles(table_t, t, ndw, window=C_WINDOW)
    ow = flash_attention(
        proj, proj, proj, bias_w, name="window_attention", t=t, n_groups=g, hb=C_HPG, nv=1, lookback=look,
        q_map=lambda gg, qb: (gg, qb, 0),
        k_map=lambda gg, kb: (kv0 + 4 * g + gg, kb, 0),
        v_map=lambda gg, kb: (kv0 + 5 * g + gg, kb, 0),
        b_map=lambda gg, bt: (gg, bt, 0, 0))

    o = gate_merge(oc, osel, ow, proj, kv0 + 6 * g)
    return matmul_residual(o, w_out, x, tm=1024, tn=512)


def conv_ffn(x, h, w_gate, w_up, layer, conv_w, conv_b, w_down):
    act = ffn_gate_up(h, w_gate, w_up, layer, conv_w, conv_b)
    return matmul_residual(act, w_down, x, tm=1024, tn=256)


def kernel(x, rel_table, ev_norm, ev_w_in, a_qk_gain, a_lambda, a_subln_gain, b_qk_gain, ev_w_out,
           od_norm, od_w_in, c_qk_gain, c_cmp_pe, c_cmp_w1, c_cmp_w2, od_w_out,
           ffn_norm, ffn_w_gate, ffn_w_up, ffn_conv_w, ffn_conv_b, ffn_w_down):
    b, s, d = x.shape
    depth = ffn_norm.shape[0]
    outs = []
    for bi in range(b):
        y = x[bi].astype(F32)
        for i in range(depth):
            if i % 2 == 0:
                e = i // 2
                lam_init = 0.8 - 0.6 * math.exp(-0.3 * i)
                y = even_mixer(y, rmsnorm(y, ev_norm[e]), ev_w_in[e], a_qk_gain[e], a_lambda[e],
                               a_subln_gain[e], b_qk_gain[e], ev_w_out[e], rel_table, lam_init)
            else:
                o = i // 2
                y = nsa_mixer(y, rmsnorm(y, od_norm[o]), od_w_in[o], c_qk_gain[o], c_cmp_pe[o],
                              c_cmp_w1[o], c_cmp_w2[o], od_w_out[o], rel_table)
            y = conv_ffn(y, rmsnorm(y, ffn_norm[i]), ffn_w_gate, ffn_w_up, i, ffn_conv_w[i],
                         ffn_conv_b[i], ffn_w_down[i])
        outs.append(y)
    return jnp.stack(outs).astype(x.dtype)
```

```python
import functools
import math

import numpy as np
import jax
import jax.numpy as jnp
from jax import lax
from jax.experimental import pallas as pl
from jax.experimental.pallas import tpu as pltpu

HEAD_DIM = 128
A_HEADS = 8
B_HEADS = 16
B_CONFIGS = ((128, 1), (512, 4), (2048, 16))
C_HEADS = 32
C_GROUPS = 2
C_HPG = C_HEADS // C_GROUPS
CMP_LEN = 32
CMP_STRIDE = 16
SEL_LEN = 64
SEL_TOPK = 16
C_WINDOW = 512
REL_BUCKETS = 32
REL_MAX_DIST = 2048
EPS = 1e-6
NEG = -1e30
FORCE = 1e9

LANES = 128
VMEM_LIMIT = 56 * 1024 * 1024

F32 = jnp.float32
BF16 = jnp.bfloat16


def _cparams(sem):
    return pltpu.CompilerParams(dimension_semantics=sem, vmem_limit_bytes=VMEM_LIMIT)


def _dot(a, b):
    return jnp.dot(a, b, preferred_element_type=F32)


def _dot_nt(a, b):
    return lax.dot_general(a, b, (((1,), (1,)), ((), ())), preferred_element_type=F32)


def _bucket_np(dist):
    n = np.maximum(dist, 0).astype(np.int32)
    max_exact = REL_BUCKETS // 2
    nf = np.maximum(n, 1).astype(np.float32)
    ratio = np.log(nf / np.float32(max_exact)) / np.float32(math.log(REL_MAX_DIST / max_exact))
    large = np.minimum(max_exact + (ratio * np.float32(REL_BUCKETS - max_exact)).astype(np.int32),
                       REL_BUCKETS - 1)
    return np.where(n < max_exact, n, large).astype(np.int32)


def _far_distance():
    b = _bucket_np(np.arange(4 * REL_MAX_DIST))
    return int(np.max(np.nonzero(b != REL_BUCKETS - 1)[0])) + 1


FAR = _far_distance()


def _bias_by_distance(table_t, dist, valid):
    b = jnp.take(table_t, jnp.asarray(_bucket_np(dist)), axis=1)
    return jnp.where(jnp.asarray(valid), b, NEG).astype(F32)


def _toeplitz_kernel(g_ref, o_ref, *, rows, width, shift, stride, transpose):
    for b in range(g_ref.shape[1]):
        _toeplitz_tile(g_ref[0, b], o_ref.at[0, b], rows, width, shift, stride, transpose)


def _toeplitz_tile(g, o_ref, rows, width, shift, stride, transpose):
    wx = g.shape[1]
    offset = (wx - shift) % wx
    if stride == 1 and not transpose and rows % LANES == 0 and width % LANES == 0 and offset % LANES == 0:
        nr, ncb = rows // LANES, width // LANES
        subs = {}
        for db in range(-(nr - 1), ncb):
            lo = LANES * db + offset - LANES
            gw = jnp.broadcast_to(g[:, lo:lo + 2 * LANES], (LANES, 2 * LANES))
            subs[db] = pltpu.roll(gw, LANES, 1, stride=1, stride_axis=0)[:, :LANES]
        for bi in range(nr):
            for bj in range(ncb):
                o_ref[bi * LANES:(bi + 1) * LANES, bj * LANES:(bj + 1) * LANES] = subs[bj - bi]
        return
    x = jnp.broadcast_to(g, (rows, wx))
    y = pltpu.roll(x, shift, 1, stride=stride, stride_axis=0)[:, :width]
    o_ref[...] = y.T if transpose else y


def toeplitz_tiles(g, rows, width, stride, offset, transpose=False):
    h, nd, wx = g.shape
    assert wx % LANES == 0 and 0 <= offset - stride * (rows - 1) and width - 1 + offset < wx
    shape = (width, rows) if transpose else (rows, width)
    tb = max(b for b in range(1, nd + 1) if nd % b == 0 and b * rows * width * 4 <= max(4 << 20, rows * width * 4))
    return pl.pallas_call(
        functools.partial(_toeplitz_kernel, rows=rows, width=width, shift=(wx - offset) % wx, stride=stride,
                          transpose=transpose),
        out_shape=jax.ShapeDtypeStruct((h, nd) + shape, F32),
        grid=(h, nd // tb),
        in_specs=[pl.BlockSpec((1, tb, 1, wx), lambda a, b: (a, b, 0, 0))],
        out_specs=pl.BlockSpec((1, tb) + shape, lambda a, b: (a, b, 0, 0)),
        name="toeplitz_tiles",
        compiler_params=_cparams(("parallel", "parallel")),
    )(g.reshape(h, nd, 1, wx))


def causal_bias_tiles(table_t, t, n_tiles, window=None, kv_mult=1):
    tk = kv_mult * t
    span = n_tiles * t + tk
    dist = (n_tiles * t) - np.arange(span)
    valid = dist >= 0 if window is None else (dist >= 0) & (dist < window)
    vrev = _bias_by_distance(table_t, dist, valid)
    g = jnp.stack([vrev[:, (n_tiles - d - 1) * t:(n_tiles - d) * t + tk] for d in range(n_tiles)], axis=1)
    return toeplitz_tiles(g, t, tk, 1, t)


def n_bias_tiles(t, tk, nq):
    return min(-(-(FAR + tk - 1) // t) + 1, nq)


def _rmsnorm_kernel(x_ref, g_ref, o_ref):
    x = x_ref[...]
    ms = jnp.mean(x * x, axis=-1, keepdims=True)
    o_ref[...] = ((x * lax.rsqrt(ms + EPS)) * g_ref[...]).astype(o_ref.dtype)


def rmsnorm(x, gain):
    s, d = x.shape
    tr = min(256, s)
    return pl.pallas_call(
        _rmsnorm_kernel,
        out_shape=jax.ShapeDtypeStruct((s, d), BF16),
        grid=(s // tr,),
        in_specs=[pl.BlockSpec((tr, d), lambda i: (i, 0)), pl.BlockSpec((1, d), lambda i: (0, 0))],
        out_specs=pl.BlockSpec((tr, d), lambda i: (i, 0)),
        name="rmsnorm",
        compiler_params=_cparams(("parallel",)),
    )(x, gain.reshape(1, d).astype(F32))


ROW_SUB = 256


def col_blocks(w, tn):
    return w.astype(BF16)[None]


def _proj_kernel(x_ref, w_ref, gain_ref, flag_ref, *refs, nblk, ts, dilations):
    perm_refs = refs[:len(dilations)]
    o_ref = refs[len(dilations)]
    od_refs = refs[len(dilations) + 1:]
    w = w_ref[0].astype(BF16)
    for r0 in range(0, x_ref.shape[0], ts):
        acc = _dot(x_ref[r0:r0 + ts, :], w)
        ys = []
        for c in range(nblk):
            sl = slice(c * LANES, (c + 1) * LANES)
            blk = acc[:, sl]
            ms = jnp.mean(blk * blk, axis=-1, keepdims=True)
            r = jnp.where(flag_ref[:, sl] > 0, lax.rsqrt(ms + EPS), 1.0)
            y = ((blk * r) * gain_ref[:, sl]).astype(o_ref.dtype)
            o_ref[c, r0:r0 + ts, :] = y
            ys.append(y)
        for d, perm_ref, od_ref in zip(dilations, perm_refs, od_refs):
            yp = _dot(perm_ref[...], jnp.concatenate(ys, axis=-1)).astype(o_ref.dtype)
            m = ts // d
            for c in range(nblk):
                for res in range(d):
                    od_ref[c, r0 // d:r0 // d + m, res * LANES:(res + 1) * LANES] = (
                        yp[res * m:(res + 1) * m, c * LANES:(c + 1) * LANES])


def proj_headmajor(x, w, gain, flag, tn, col0=0, dilations=()):
    s, k = x.shape
    n = gain.shape[0]
    tm = min(2048 if w.dtype == BF16 else 1024, s)
    ts = min(ROW_SUB, tm)
    nblk = tn // LANES
    cb0 = col0 // tn
    perms = []
    for d in dilations:
        dst = np.arange(ts)
        src = (dst % (ts // d)) * d + dst // (ts // d)
        perms.append(jnp.asarray((src[:, None] == np.arange(ts)[None, :]).astype(np.float32), BF16))
    out_shape = [jax.ShapeDtypeStruct((n // LANES, s, LANES), BF16)]
    out_specs = [pl.BlockSpec((nblk, tm, LANES), lambda i, j: (j, i, 0))]
    for d in dilations:
        out_shape.append(jax.ShapeDtypeStruct((n // LANES, s // d, d * LANES), BF16))
        out_specs.append(pl.BlockSpec((nblk, tm // d, d * LANES), lambda i, j: (j, i, 0)))
    outs = pl.pallas_call(
        functools.partial(_proj_kernel, nblk=nblk, ts=ts, dilations=tuple(dilations)),
        out_shape=out_shape,
        grid=(s // tm, n // tn),
        in_specs=[pl.BlockSpec((tm, k), lambda i, j: (i, 0)),
                  pl.BlockSpec((1, k, tn), lambda i, j: (0, 0, cb0 + j)),
                  pl.BlockSpec((1, tn), lambda i, j: (0, j)),
                  pl.BlockSpec((1, tn), lambda i, j: (0, j))]
                 + [pl.BlockSpec((ts, ts), lambda i, j: (0, 0))] * len(dilations),
        out_specs=out_specs,
        name="proj_headmajor",
        compiler_params=_cparams(("parallel", "parallel")),
    )(x, w[None], gain.reshape(1, n).astype(F32), flag.reshape(1, n).astype(F32), *perms)
    return outs if dilations else outs[0]


def _mm_res_kernel(x_ref, w_ref, r_ref, o_ref):
    o_ref[...] = r_ref[...] + _dot(x_ref[...], w_ref[0])


def matmul_residual(x, w, res, tm, tn):
    s, k = x.shape
    n = w.shape[1]
    tm = min(tm, s)
    tn = min(tn, n)
    return pl.pallas_call(
        _mm_res_kernel,
        out_shape=jax.ShapeDtypeStruct((s, n), F32),
        grid=(s // tm, n // tn),
        in_specs=[pl.BlockSpec((tm, k), lambda i, j: (i, 0), pipeline_mode=pl.Buffered(1)),
                  pl.BlockSpec((1, k, tn), lambda i, j: (0, 0, j)),
                  pl.BlockSpec((tm, tn), lambda i, j: (i, j))],
        out_specs=pl.BlockSpec((tm, tn), lambda i, j: (i, j)),
        name="matmul_residual",
        compiler_params=_cparams(("parallel", "arbitrary")),
    )(x, col_blocks(w, tn), res)


def _mm2_res_kernel(x1_ref, x2_ref, w_ref, r_ref, o_ref, *, k1):
    acc = _dot(x1_ref[...], w_ref[0, 0:k1, :]) + _dot(x2_ref[...], w_ref[0, k1:, :])
    o_ref[...] = r_ref[...] + acc


def matmul2_residual(x1, x2, w, res, tm, tn):
    s, k1 = x1.shape
    k2 = x2.shape[1]
    n = w.shape[1]
    tm = min(tm, s)
    tn = min(tn, n)
    return pl.pallas_call(
        functools.partial(_mm2_res_kernel, k1=k1),
        out_shape=jax.ShapeDtypeStruct((s, n), F32),
        grid=(s // tm, n // tn),
        in_specs=[pl.BlockSpec((tm, k1), lambda i, j: (i, 0)),
                  pl.BlockSpec((tm, k2), lambda i, j: (i, 0)),
                  pl.BlockSpec((1, k1 + k2, tn), lambda i, j: (0, 0, j)),
                  pl.BlockSpec((tm, tn), lambda i, j: (i, j))],
        out_specs=pl.BlockSpec((tm, tn), lambda i, j: (i, j)),
        name="matmul2_residual",
        compiler_params=_cparams(("parallel", "arbitrary")),
    )(x1, x2, col_blocks(w, tn), res)


def _ffn1_kernel(x_ref, wg_ref, wu_ref, cw_ref, cb_ref, o_ref, carry_ref, *, ts):
    i = pl.program_id(0)
    j = pl.program_id(1)

    @pl.when(i == 0)
    def _():
        carry_ref[j] = jnp.zeros(carry_ref.shape[1:], F32)

    prev = carry_ref[j]
    wg = wg_ref[0].astype(BF16)
    wu = wu_ref[0].astype(BF16)
    cw = cw_ref[...]
    cb = cb_ref[...]
    row = lax.broadcasted_iota(jnp.int32, (ts, wg.shape[1]), 0)
    for r0 in range(0, x_ref.shape[0], ts):
        x = x_ref[r0:r0 + ts, :]
        g = _dot(x, wg)
        u = _dot(x, wu)
        p7 = prev[7:8, :]
        p6 = prev[6:7, :]
        g1 = jnp.where(row == 0, p7, pltpu.roll(g, 1, 0))
        g2 = jnp.where(row == 0, p6, jnp.where(row == 1, p7, pltpu.roll(g, 2, 0)))
        gc = g2 * cw[0:1, :] + g1 * cw[1:2, :] + g * cw[2:3, :] + cb
        o_ref[r0:r0 + ts, :] = ((gc * jax.nn.sigmoid(gc)) * u).astype(o_ref.dtype)
        prev = g[ts - 8:ts, :]
    carry_ref[j] = prev


def ffn_gate_up(h, wg, wu, layer, conv_w, conv_b):
    s, d = h.shape
    f = wg.shape[2]
    tm = min(2048, s)
    tn = 256
    nj = f // tn
    return pl.pallas_call(
        functools.partial(_ffn1_kernel, ts=min(ROW_SUB, tm)),
        out_shape=jax.ShapeDtypeStruct((s, f), BF16),
        grid=(s // tm, nj),
        in_specs=[pl.BlockSpec((tm, d), lambda i, j: (i, 0), pipeline_mode=pl.Buffered(1)),
                  pl.BlockSpec((1, d, tn), lambda i, j: (layer, 0, j)),
                  pl.BlockSpec((1, d, tn), lambda i, j: (layer, 0, j)),
                  pl.BlockSpec((3, tn), lambda i, j: (0, j)),
                  pl.BlockSpec((1, tn), lambda i, j: (0, j))],
        out_specs=pl.BlockSpec((tm, tn), lambda i, j: (i, j)),
        scratch_shapes=[pltpu.VMEM((nj, 8, tn), F32)],
        name="ffn_gate_up",
        compiler_params=_cparams(("arbitrary", "arbitrary")),
    )(h, wg, wu, conv_w.astype(F32), conv_b.reshape(1, f).astype(F32))


def _flash_kernel(qi_ref, kj_ref, bi_ref, fl_ref, *refs, hb, nv, use_sel, rc, diff_lam_init):
    q_ref, k_ref, v_ref, b_ref = refs[:4]
    refs = refs[4:]
    if use_sel:
        sel_ref, en_ref = refs[:2]
        refs = refs[2:]
    if diff_lam_init is not None:
        lam_ref, gain_ref = refs[:2]
        refs = refs[2:]
    o_ref, m_sc, acc_sc, s_sc, p_sc, al_sc = refs[:6]
    if use_sel:
        qa_sc = refs[6]
    per_head_k = k_ref.shape[0] == hb and hb > 1
    per_head_bias = b_ref.shape[0] == hb
    p = pl.program_id(1)
    flags = fl_ref[p]
    dv = nv * LANES
    t = q_ref.shape[1]
    tk = k_ref.shape[1]

    @pl.when((flags & 1) != 0)
    def _():
        m_sc[...] = jnp.full(m_sc.shape, -jnp.inf, F32)
        acc_sc[...] = jnp.zeros(acc_sc.shape, F32)
        if use_sel:
            notsel = (1.0 - sel_ref[0].astype(F32)).astype(BF16)
            for h in range(hb):
                qa_sc[h, :, 0:LANES] = q_ref[h]
                qa_sc[h, :, LANES:] = notsel

    v_aug = jnp.concatenate([v_ref[c] for c in range(nv)] + [jnp.ones((tk, LANES), BF16)], axis=-1)

    for h in range(hb):
        k = k_ref[h if per_head_k else 0]
        if use_sel:
            k = jnp.concatenate([k, en_ref[0]], axis=-1)
        qh = qa_sc[h] if use_sel else q_ref[h]
        s_sc[h] = _dot_nt(qh, k)
        hbias = h if per_head_bias else 0
        for r0 in range(0, t, rc):
            rows = slice(r0, r0 + rc)
            s = s_sc[h, rows, :] + b_ref[hbias, 0, rows, :]
            m_old = m_sc[h, rows, :]
            m_new = jnp.maximum(m_old, jnp.max(s, axis=-1, keepdims=True))
            al_sc[h, rows, :] = jnp.exp(m_old - m_new)
            m_sc[h, rows, :] = m_new
            p_sc[h, rows, :] = jnp.exp(s - jnp.tile(m_new, (1, tk // LANES))).astype(BF16)
        acc_sc[h] = jnp.tile(al_sc[h], (1, nv + 1)) * acc_sc[h] + _dot(p_sc[h], v_aug)

    def normalised(h):
        acc = acc_sc[h]
        return acc[:, 0:dv] / jnp.tile(acc[:, dv:], (1, nv))

    @pl.when((flags & 2) != 0)
    def _():
        if diff_lam_init is None:
            for h in range(hb):
                o_ref[:, h * dv:(h + 1) * dv] = normalised(h).astype(o_ref.dtype)
        else:
            lf = lam_ref[...]
            s1 = jnp.sum(lf[0:1, :] * lf[1:2, :], axis=-1, keepdims=True)
            s2 = jnp.sum(lf[2:3, :] * lf[3:4, :], axis=-1, keepdims=True)
            lam = jnp.exp(s1) - jnp.exp(s2) + diff_lam_init
            o = normalised(0) - lam * normalised(1)
            ms = jnp.mean(o * o, axis=-1, keepdims=True)
            y = ((o * lax.rsqrt(ms + EPS)) * gain_ref[...]) * (1.0 - diff_lam_init)
            o_ref[...] = y.astype(o_ref.dtype)


def _pairs(nq, lookback, max_bias, kv_mult):
    assert lookback is None or kv_mult == 1
    qi, kj, bi, fl = [], [], [], []
    for q in range(nq):
        lo = 0 if lookback is None else max(0, q - lookback)
        hi = q // kv_mult
        for k in range(lo, hi + 1):
            qi.append(q)
            kj.append(k)
            bi.append(min(q - kv_mult * k, max_bias))
            fl.append((1 if k == lo else 0) | (2 if k == hi else 0))
    return [jnp.asarray(np.asarray(a, np.int32)) for a in (qi, kj, bi, fl)]


def flash_attention(q, k, v, bias, *, name, t, n_groups, hb, nv, q_map, k_map, v_map, b_map,
                    lookback=None, sel=None, key_neg=None, diff_params=None):
    s = q.shape[1]
    nq = s // t
    tk = bias.shape[3]
    qi, kj, bi, fl = _pairs(nq, lookback, bias.shape[1] - 1, tk // t)
    n_pairs = int(qi.shape[0])
    dv = nv * LANES
    use_sel = sel is not None
    diff = diff_params is not None
    in_specs = [
        pl.BlockSpec((hb, t, LANES), lambda g, p, qi, kj, bi, fl: q_map(g, qi[p])),
        pl.BlockSpec((hb if diff else 1, tk, LANES), lambda g, p, qi, kj, bi, fl: k_map(g, kj[p])),
        pl.BlockSpec((nv, tk, LANES), lambda g, p, qi, kj, bi, fl: v_map(g, kj[p])),
        pl.BlockSpec((1 if diff else hb, 1, t, tk), lambda g, p, qi, kj, bi, fl: b_map(g, bi[p])),
    ]
    args = [q, k, v, bias]
    scratch = [pltpu.VMEM((hb, t, LANES), F32), pltpu.VMEM((hb, t, dv + LANES), F32),
               pltpu.VMEM((hb, t, tk), F32), pltpu.VMEM((hb, t, tk), BF16), pltpu.VMEM((hb, t, LANES), F32)]
    if use_sel:
        nselp = sel.shape[2]
        in_specs += [
            pl.BlockSpec((1, t, nselp), lambda g, p, qi, kj, bi, fl: (g, qi[p], 0)),
            pl.BlockSpec((1, tk, nselp), lambda g, p, qi, kj, bi, fl: (kj[p], 0, 0)),
        ]
        args += [sel, key_neg]
        scratch.append(pltpu.VMEM((hb, t, LANES + nselp), BF16))
    lam_init = None
    if diff:
        a_lambda, subln_gain, lam_init = diff_params
        in_specs += [
            pl.BlockSpec((4, HEAD_DIM), lambda g, p, qi, kj, bi, fl: (0, 0)),
            pl.BlockSpec((1, dv), lambda g, p, qi, kj, bi, fl: (0, 0)),
        ]
        args += [a_lambda.astype(F32), subln_gain.reshape(1, dv).astype(F32)]
    out_w = dv if diff else hb * dv
    grid_spec = pltpu.PrefetchScalarGridSpec(
        num_scalar_prefetch=4,
        grid=(n_groups, n_pairs),
        in_specs=in_specs,
        out_specs=pl.BlockSpec((t, out_w), lambda g, p, qi, kj, bi, fl: (qi[p], g)),
        scratch_shapes=scratch,
    )
    rc = max(8, min(t, (16 * 8 * LANES) // tk))
    return pl.pallas_call(
        functools.partial(_flash_kernel, hb=hb, nv=nv, use_sel=use_sel, rc=rc, diff_lam_init=lam_init),
        out_shape=jax.ShapeDtypeStruct((s, n_groups * out_w), BF16),
        grid_spec=grid_spec,
        name=name,
        compiler_params=_cparams(("parallel", "arbitrary")),
    )(qi, kj, bi, fl, *args)


def _dilated_kernel(q_ref, kp_ref, kc_ref, vp_ref, vc_ref, b_ref, o_ref, lse_ref, s_sc, p_sc, m_sc):
    span = q_ref.shape[1]
    rc = min(64, span)
    no_prev = jnp.where(pl.program_id(1) == 0, NEG, 0.0)
    col = lax.broadcasted_iota(jnp.int32, (1, 2 * span), 1)
    prev_mask = jnp.where(col < span, no_prev, 0.0)
    ones = jnp.ones((2 * span, LANES), BF16)
    for h in range(B_HEADS):
        keys = jnp.concatenate([kp_ref[h], kc_ref[h]], axis=0)
        s_sc[h] = _dot_nt(q_ref[h], keys)
        for r0 in range(0, span, rc):
            rows = slice(r0, r0 + rc)
            s = s_sc[h, rows, :] + (b_ref[h, 0, rows, :] + prev_mask)
            m = jnp.max(s, axis=-1, keepdims=True)
            m_sc[h, rows, :] = jnp.broadcast_to(m, (rc, LANES))
            p_sc[h, rows, :] = jnp.exp(s - m).astype(BF16)
        vals = jnp.concatenate([jnp.concatenate([vp_ref[h], vc_ref[h]], axis=0), ones], axis=-1)
        pv = _dot(p_sc[h], vals)
        l = pv[:, HEAD_DIM:]
        sl = slice(h * HEAD_DIM, (h + 1) * HEAD_DIM)
        o_ref[:, sl] = (pv[:, 0:HEAD_DIM] / l).astype(o_ref.dtype)
        lse_ref[:, sl] = m_sc[h] + jnp.log(l)


def dilated_group(x, bias, dilation, span, q_blk, k_blk, v_blk):
    nh, l, _ = x.shape
    s = l * dilation
    nb = l // span
    w = B_HEADS * HEAD_DIM
    hspec = lambda blk, prev: pl.BlockSpec(
        (B_HEADS, span, LANES),
        (lambda r, n: (blk, jnp.maximum(n - 1, 0), r)) if prev else (lambda r, n: (blk, n, r)))
    o, lse = pl.pallas_call(
        _dilated_kernel,
        out_shape=[jax.ShapeDtypeStruct((l, dilation * w), BF16), jax.ShapeDtypeStruct((l, dilation * w), F32)],
        grid=(dilation, nb),
        in_specs=[hspec(q_blk, False), hspec(k_blk, True), hspec(k_blk, False),
                  hspec(v_blk, True), hspec(v_blk, False),
                  pl.BlockSpec((B_HEADS, 1, span, 2 * span), lambda r, n: (0, 0, 0, 0))],
        out_specs=[pl.BlockSpec((span, w), lambda r, n: (n, r))] * 2,
        scratch_shapes=[pltpu.VMEM((B_HEADS, span, 2 * span), F32),
                        pltpu.VMEM((B_HEADS, span, 2 * span), BF16),
                        pltpu.VMEM((B_HEADS, span, LANES), F32)],
        name="dilated_group",
        compiler_params=_cparams(("parallel", "parallel")),
    )(x, x, x, x, x, bias)
    return o.reshape(s, w), lse.reshape(s, w)


def _mix3_kernel(o1, o2, o3, l1, l2, l3, out_ref):
    a, b, c = l1[...], l2[...], l3[...]
    m = jnp.maximum(jnp.maximum(a, b), c)
    ea, eb, ec = jnp.exp(a - m), jnp.exp(b - m), jnp.exp(c - m)
    num = ea * o1[...].astype(F32) + eb * o2[...].astype(F32) + ec * o3[...].astype(F32)
    out_ref[...] = (num / (ea + eb + ec)).astype(out_ref.dtype)


def mix_dilated(os, lses):
    s, w = os[0].shape
    ts = min(256, s)
    spec = pl.BlockSpec((ts, w), lambda i: (i, 0))
    return pl.pallas_call(
        _mix3_kernel,
        out_shape=jax.ShapeDtypeStruct((s, w), BF16),
        grid=(s // ts,),
        in_specs=[spec] * 6,
        out_specs=spec,
        name="mix_dilated",
        compiler_params=_cparams(("parallel",)),
    )(*os, *lses)


def _compress_kernel(x_ref, pe_ref, w1_ref, w2_ref, gain_ref, o_ref):
    kv = pl.program_id(0)
    x = x_ref[0, 0].astype(F32)
    nc = x.shape[0]
    a = _dot((x + pe_ref[0, 0]).astype(BF16), w1_ref[0, 0])
    b = _dot((x + pe_ref[0, 1]).astype(BF16), w1_ref[0, 1])
    hid = jax.nn.gelu(a + pltpu.roll(b, nc - 1, 0))
    c = _dot(hid.astype(BF16), w2_ref[0])
    ms = jnp.mean(c * c, axis=-1, keepdims=True)
    normed = (c * lax.rsqrt(ms + EPS)) * gain_ref[...]
    o_ref[0, 0] = jnp.where(kv == 0, normed, c).astype(o_ref.dtype)


def compress_kv(xc, pe, w1, w2, gain):
    _, g, nc, cw = xc.shape
    hid = w1.shape[-1]
    return pl.pallas_call(
        _compress_kernel,
        out_shape=jax.ShapeDtypeStruct((2, g, nc, HEAD_DIM), BF16),
        grid=(2, g),
        in_specs=[pl.BlockSpec((1, 1, nc, cw), lambda a, b: (a, b, 0, 0)),
                  pl.BlockSpec((1, 2, 1, cw), lambda a, b: (a, 0, 0, 0)),
                  pl.BlockSpec((1, 2, cw, hid), lambda a, b: (a, 0, 0, 0)),
                  pl.BlockSpec((1, hid, HEAD_DIM), lambda a, b: (a, 0, 0)),
                  pl.BlockSpec((1, HEAD_DIM), lambda a, b: (0, 0))],
        out_specs=pl.BlockSpec((1, 1, nc, HEAD_DIM), lambda a, b: (a, b, 0, 0)),
        name="compress_kv",
        compiler_params=_cparams(("parallel", "parallel")),
    )(xc, pe, w1, w2, gain.reshape(1, HEAD_DIM).astype(F32))


def _cmp_kernel(q_ref, k_ref, v_ref, *rest, nkt, n_top):
    bias_refs = rest[:nkt]
    ov_ref, oc_ref, sel_ref, imp_sc, s_sc, p_sc = rest[nkt:]
    qi = pl.program_id(1)
    tq = q_ref.shape[1]
    ncp = k_ref.shape[2]
    rc = max(8, min(tq, (16 * 8 * LANES) // ncp))
    k = k_ref[0, 0]
    v = v_ref[0, 0]
    for h in range(C_HPG):
        s_sc[h] = _dot_nt(q_ref[h], k)
        for r0 in range(0, tq, rc):
            rows = slice(r0, r0 + rc)
            if nkt == 1:
                bias = bias_refs[0][h, 0, rows, :]
            else:
                bias = jnp.concatenate([b[h, 0, rows, :] for b in bias_refs], axis=-1)
            s = s_sc[h, rows, :] + bias
            m = jnp.max(s, axis=-1, keepdims=True)
            e = jnp.exp(s - m)
            pc = e * jnp.where(m > 0.5 * NEG, 1.0 / jnp.sum(e, axis=-1, keepdims=True), 0.0)
            p_sc[h, rows, :] = pc.astype(BF16)
            if h == 0:
                imp_sc[rows, :] = pc
            else:
                imp_sc[rows, :] += pc
        oc_ref[:, h * HEAD_DIM:(h + 1) * HEAD_DIM] = _dot(p_sc[h], v).astype(oc_ref.dtype)
    imp = imp_sc[...]
    hi = imp.astype(BF16)
    lo = (imp - hi.astype(F32)).astype(BF16)
    ov = ov_ref[...]
    impn = _dot(hi, ov) + _dot(lo, ov)
    score_in = impn.T
    shape = score_in.shape
    t = qi * tq + lax.broadcasted_iota(jnp.int32, shape, 1)
    n = lax.broadcasted_iota(jnp.int32, shape, 0)
    cur = lax.shift_right_logical(t, int(math.log2(SEL_LEN)))
    forced = (n == 0) | (n == cur) | (n == cur - 1)
    valid = n <= cur
    score = jnp.where(valid, jnp.where(forced, FORCE, score_in), NEG)
    nf = n.astype(F32)
    sel = jnp.zeros(shape, F32)
    for _ in range(n_top):
        m = jnp.max(score, axis=0, keepdims=True)
        first = jnp.min(jnp.where(score == m, nf, 1e9), axis=0, keepdims=True)
        pick = nf == first
        sel = jnp.where(pick, 1.0, sel)
        score = jnp.where(pick, -jnp.inf, score)
    sel_ref[0] = jnp.where(valid, sel, 0.0).T.astype(sel_ref.dtype)


def cmp_attention(q_hm, kvcmp, bias, overlap, tq, n_top):
    s = q_hm.shape[1]
    ncp = kvcmp.shape[2]
    nkt = ncp // LANES
    nselp = overlap.shape[1]
    nd = bias.shape[1]
    per_tile = (LANES * CMP_STRIDE) // tq

    def bias_spec(kt):
        return pl.BlockSpec(
            (C_HPG, 1, tq, LANES),
            lambda g, i: (g, jnp.clip(i - per_tile * kt, -1, nd - 2) + 1, 0, 0))

    w = C_HEADS * HEAD_DIM
    return pl.pallas_call(
        functools.partial(_cmp_kernel, nkt=nkt, n_top=n_top),
        out_shape=[jax.ShapeDtypeStruct((s, w), BF16),
                   jax.ShapeDtypeStruct((C_GROUPS, s, nselp), BF16)],
        grid=(C_GROUPS, s // tq),
        in_specs=[pl.BlockSpec((C_HPG, tq, LANES), lambda g, i: (g, i, 0)),
                  pl.BlockSpec((1, 1, ncp, LANES), lambda g, i: (0, g, 0, 0)),
                  pl.BlockSpec((1, 1, ncp, LANES), lambda g, i: (1, g, 0, 0))]
                 + [bias_spec(kt) for kt in range(nkt)]
                 + [pl.BlockSpec((ncp, nselp), lambda g, i: (0, 0))],
        out_specs=[pl.BlockSpec((tq, w // C_GROUPS), lambda g, i: (i, g)),
                   pl.BlockSpec((1, tq, nselp), lambda g, i: (g, i, 0))],
        scratch_shapes=[pltpu.VMEM((tq, ncp), F32), pltpu.VMEM((C_HPG, tq, ncp), F32),
                        pltpu.VMEM((C_HPG, tq, ncp), BF16)],
        name="cmp_attention",
        compiler_params=_cparams(("parallel", "parallel")),
    )(q_hm, kvcmp, kvcmp, *([bias] * nkt), overlap)


def _gate_kernel(oc_ref, os_ref, ow_ref, gt_ref, o_ref):
    gt = jax.nn.sigmoid(gt_ref[0].astype(F32))
    for h in range(C_HEADS):
        sl = slice(h * HEAD_DIM, (h + 1) * HEAD_DIM)
        g0 = gt[:, h:h + 1]
        g1 = gt[:, C_HEADS + h:C_HEADS + h + 1]
        g2 = gt[:, 2 * C_HEADS + h:2 * C_HEADS + h + 1]
        mixed = (g0 * oc_ref[:, sl].astype(F32) + g1 * os_ref[:, sl].astype(F32)
                 + g2 * ow_ref[:, sl].astype(F32))
        o_ref[:, sl] = mixed.astype(o_ref.dtype)


def gate_merge(oc, osel, ow, proj_hm, gate_blk):
    s, w = oc.shape
    ts = min(256, s)
    spec = pl.BlockSpec((ts, w), lambda i: (i, 0))
    return pl.pallas_call(
        _gate_kernel,
        out_shape=jax.ShapeDtypeStruct((s, w), BF16),
        grid=(s // ts,),
        in_specs=[spec, spec, spec, pl.BlockSpec((1, ts, LANES), lambda i: (gate_blk, i, 0))],
        out_specs=spec,
        name="gate_merge",
        compiler_params=_cparams(("parallel",)),
    )(oc, osel, ow, proj_hm)


def _tile_gain(g, reps, scale=1.0):
    return jnp.tile(g.astype(F32) * scale, reps)


def even_mixer(x, h, w_in, a_qk_gain, a_lambda, a_subln_gain, b_qk_gain, w_out, rel_table, lam_init):
    s = h.shape[0]
    scale = HEAD_DIM ** -0.5
    na = 2 * A_HEADS
    ones = lambda n: jnp.ones((n * HEAD_DIM,), F32)
    zeros = lambda n: jnp.zeros((n * HEAD_DIM,), F32)
    d = h.shape[1]
    wa = na * HEAD_DIM
    table_t = rel_table.astype(F32).T

    def pair_cols(w):
        return w.reshape(d, 2, A_HEADS, HEAD_DIM).transpose(0, 2, 1, 3).reshape(d, wa)

    w_a = jnp.concatenate([pair_cols(w_in[:, 0:wa]), pair_cols(w_in[:, wa:2 * wa]), w_in[:, 2 * wa:3 * wa]],
                          axis=1).astype(BF16)
    gain_a = jnp.concatenate([_tile_gain(a_qk_gain[0], na, scale), _tile_gain(a_qk_gain[1], na), ones(na)])
    flag_a = jnp.concatenate([ones(2 * na), zeros(na)])
    proj_a = proj_headmajor(h, w_a, gain_a, flag_a, tn=512)
    t = min(1024, s)
    bias_a = causal_bias_tiles(table_t[:A_HEADS], t, n_bias_tiles(t, t, s // t))
    ao = flash_attention(
        proj_a, proj_a, proj_a, bias_a, name="diff_attention", t=t, n_groups=A_HEADS, hb=2, nv=2,
        q_map=lambda g, qb: (g, qb, 0),
        k_map=lambda g, kb: (A_HEADS + g, kb, 0),
        v_map=lambda g, kb: (2 * A_HEADS + g, kb, 0),
        b_map=lambda g, bt: (g, bt, 0, 0),
        diff_params=(a_lambda, a_subln_gain, lam_init))

    gain_b = jnp.concatenate([_tile_gain(b_qk_gain[0], B_HEADS, scale), _tile_gain(b_qk_gain[1], B_HEADS),
                              ones(B_HEADS)])
    flag_b = jnp.concatenate([ones(2 * B_HEADS), zeros(B_HEADS)])
    dilations = sorted({dil for _, dil in B_CONFIGS if dil > 1})
    projs = proj_headmajor(h, w_in, gain_b, flag_b, tn=512, col0=3 * wa, dilations=dilations)
    proj_by_dilation = dict(zip([1] + dilations, projs if dilations else [projs]))
    hb0 = 0
    os, lses = [], []
    for window, dilation in B_CONFIGS:
        proj = proj_by_dilation[dilation]
        span = window // dilation
        j = 2 * span - np.arange(3 * span)
        g_b = _bias_by_distance(table_t[A_HEADS:A_HEADS + B_HEADS], j * dilation, (j >= 0) & (j <= span))
        bias_b = toeplitz_tiles(g_b[:, None, :], span, 2 * span, 1, span)
        o, lse = dilated_group(proj, bias_b, dilation, span, hb0, hb0 + 1, hb0 + 2)
        os.append(o)
        lses.append(lse)
    bo = mix_dilated(os, lses)
    return matmul2_residual(ao, bo, w_out, x, tm=1024, tn=512)


def nsa_mixer(x, h, w_in, c_qk_gain, c_cmp_pe, c_cmp_w1, c_cmp_w2, w_out, rel_table):
    s, d = h.shape
    scale = HEAD_DIM ** -0.5
    g = C_GROUPS
    n_in = w_in.shape[1]
    n_pad = -(-n_in // 640) * 640
    ones = lambda n: jnp.ones((n,), F32)
    zeros = lambda n: jnp.zeros((n,), F32)
    kvw = g * HEAD_DIM
    gain = jnp.concatenate([_tile_gain(c_qk_gain[0], C_HEADS, scale), ones(2 * kvw),
                            _tile_gain(c_qk_gain[2], g), ones(kvw), _tile_gain(c_qk_gain[3], g), ones(kvw),
                            ones(n_pad - C_HEADS * HEAD_DIM - 6 * kvw)])
    flag = jnp.concatenate([ones(C_HEADS * HEAD_DIM), zeros(2 * kvw), ones(kvw), zeros(kvw), ones(kvw),
                            zeros(kvw), zeros(n_pad - C_HEADS * HEAD_DIM - 6 * kvw)])
    w_pad = jnp.pad(w_in.astype(BF16), ((0, 0), (0, n_pad - n_in)))
    proj = proj_headmajor(h, w_pad, gain, flag, tn=640)
    kv0 = C_HEADS
    table_t = rel_table.astype(F32).T[:C_HEADS]

    nc = s // CMP_STRIDE
    cw = CMP_STRIDE * HEAD_DIM
    xc = proj[kv0:kv0 + 2 * g].reshape(2, g, nc, cw)
    pe = c_cmp_pe.astype(F32).reshape(2, 2, 1, cw)
    w1 = c_cmp_w1.astype(BF16).reshape(2, 2, cw, c_cmp_w1.shape[-1])
    kvcmp = compress_kv(xc, pe, w1, c_cmp_w2.astype(BF16), c_qk_gain[1])
    if nc % LANES:
        kvcmp = jnp.pad(kvcmp, ((0, 0), (0, 0), (0, LANES - nc % LANES), (0, 0)))
    ncp = kvcmp.shape[2]

    tq = min(256, s)
    n_cmp = (s - CMP_LEN) // CMP_STRIDE + 1
    n_sel = s // SEL_LEN
    nselp = -(-n_sel // LANES) * LANES
    n_top = min(SEL_TOPK, n_sel)
    off = CMP_STRIDE * (LANES - 1)
    far_tile = -(-(FAR + off + CMP_LEN - 1) // tq)
    nd = far_tile + 2
    wx = -(-(tq + off) // LANES) * LANES
    dist_c = np.arange(tq * (nd - 1) + wx) - (tq + off + CMP_LEN - 1)
    vec_c = _bias_by_distance(table_t, dist_c, dist_c >= 0)
    g_c = jnp.stack([vec_c[:, tq * d:tq * d + wx] for d in range(nd)], axis=1)
    bias_c = toeplitz_tiles(g_c, LANES, tq, CMP_STRIDE, off, transpose=True)
    c_start = np.arange(ncp) * CMP_STRIDE
    s_start = np.arange(nselp) * SEL_LEN
    ov = ((c_start[:, None] < s_start[None, :] + SEL_LEN) & (c_start[:, None] + CMP_LEN > s_start[None, :])
          & (np.arange(ncp)[:, None] < n_cmp) & (np.arange(nselp)[None, :] < n_sel))
    oc, sel = cmp_attention(proj, kvcmp, bias_c, jnp.asarray(ov.astype(np.float32), BF16), tq, n_top)

    t = tq
    kv_mult = 2 if s % (2 * t) == 0 else 1
    tk = kv_mult * t
    bias_s = causal_bias_tiles(table_t, t, n_bias_tiles(t, tk, s // t), kv_mult=kv_mult)
    key_blk = np.arange(s) // SEL_LEN
    key_neg = np.where(key_blk[:, None] == np.arange(nselp)[None, :], NEG, 0.0).reshape(s // tk, tk, nselp)
    osel = flash_attention(
        proj, proj, proj, bias_s, name="selected_attention", t=t, n_groups=g, hb=C_HPG, nv=1,
        q_map=lambda gg, qb: (gg, qb, 0),
        k_map=lambda gg, kb: (kv0 + 2 * g + gg, kb, 0),
        v_map=lambda gg, kb: (kv0 + 3 * g + gg, kb, 0),
        b_map=lambda gg, bt: (gg, bt, 0, 0),
        sel=sel, key_neg=jnp.asarray(key_neg.astype(np.float32), BF16))

    look = -(-(C_WINDOW - 1) // t)
    ndw = min(look + 1, s // t)
    bias_w = causal_bias_tiles(table_t, t, ndw, window=C_WINDOW)
    ow = flash_attention(
        proj, proj, proj, bias_w, name="window_attention", t=t, n_groups=g, hb=C_HPG, nv=1, lookback=look,
        q_map=lambda gg, qb: (gg, qb, 0),
        k_map=lambda gg, kb: (kv0 + 4 * g + gg, kb, 0),
        v_map=lambda gg, kb: (kv0 + 5 * g + gg, kb, 0),
        b_map=lambda gg, bt: (gg, bt, 0, 0))

    o = gate_merge(oc, osel, ow, proj, kv0 + 6 * g)
    return matmul_residual(o, w_out, x, tm=1024, tn=512)


def conv_ffn(x, h, w_gate, w_up, layer, conv_w, conv_b, w_down):
    act = ffn_gate_up(h, w_gate, w_up, layer, conv_w, conv_b)
    return matmul_residual(act, w_down, x, tm=1024, tn=256)


def kernel(x, rel_table, ev_norm, ev_w_in, a_qk_gain, a_lambda, a_subln_gain, b_qk_gain, ev_w_out,
           od_norm, od_w_in, c_qk_gain, c_cmp_pe, c_cmp_w1, c_cmp_w2, od_w_out,
           ffn_norm, ffn_w_gate, ffn_w_up, ffn_conv_w, ffn_conv_b, ffn_w_down):
    b, s, d = x.shape
    depth = ffn_norm.shape[0]
    outs = []
    for bi in range(b):
        y = x[bi].astype(F32)
        for i in range(depth):
            if i % 2 == 0:
                e = i // 2
                lam_init = 0.8 - 0.6 * math.exp(-0.3 * i)
                y = even_mixer(y, rmsnorm(y, ev_norm[e]), ev_w_in[e], a_qk_gain[e], a_lambda[e],
                               a_subln_gain[e], b_qk_gain[e], ev_w_out[e], rel_table, lam_init)
            else:
                o = i // 2
                y = nsa_mixer(y, rmsnorm(y, od_norm[o]), od_w_in[o], c_qk_gain[o], c_cmp_pe[o],
                              c_cmp_w1[o], c_cmp_w2[o], od_w_out[o], rel_table)
            y = conv_ffn(y, rmsnorm(y, ffn_norm[i]), ffn_w_gate, ffn_w_up, i, ffn_conv_w[i],
                         ffn_conv_b[i], ffn_w_down[i])
        outs.append(y)
    return jnp.stack(outs).astype(x.dtype)
```

```python
import functools
import math

import numpy as np
import jax
import jax.numpy as jnp
from jax import lax
from jax.experimental import pallas as pl
from jax.experimental.pallas import tpu as pltpu

HEAD_DIM = 128
A_HEADS = 8
B_HEADS = 16
B_CONFIGS = ((128, 1), (512, 4), (2048, 16))
C_HEADS = 32
C_GROUPS = 2
C_HPG = C_HEADS // C_GROUPS
CMP_LEN = 32
CMP_STRIDE = 16
SEL_LEN = 64
SEL_TOPK = 16
C_WINDOW = 512
REL_BUCKETS = 32
REL_MAX_DIST = 2048
EPS = 1e-6
NEG = -1e30
FORCE = 1e9

LANES = 128
VMEM_LIMIT = 56 * 1024 * 1024

F32 = jnp.float32
BF16 = jnp.bfloat16


def _cparams(sem):
    return pltpu.CompilerParams(dimension_semantics=sem, vmem_limit_bytes=VMEM_LIMIT)


def _dot(a, b):
    return jnp.dot(a, b, preferred_element_type=F32)


def _dot_nt(a, b):
    return lax.dot_general(a, b, (((1,), (1,)), ((), ())), preferred_element_type=F32)


def _bucket_np(dist):
    n = np.maximum(dist, 0).astype(np.int32)
    max_exact = REL_BUCKETS // 2
    nf = np.maximum(n, 1).astype(np.float32)
    ratio = np.log(nf / np.float32(max_exact)) / np.float32(math.log(REL_MAX_DIST / max_exact))
    large = np.minimum(max_exact + (ratio * np.float32(REL_BUCKETS - max_exact)).astype(np.int32),
                       REL_BUCKETS - 1)
    return np.where(n < max_exact, n, large).astype(np.int32)


def _far_distance():
    b = _bucket_np(np.arange(4 * REL_MAX_DIST))
    return int(np.max(np.nonzero(b != REL_BUCKETS - 1)[0])) + 1


FAR = _far_distance()


def _bias_by_distance(table_t, dist, valid):
    b = jnp.take(table_t, jnp.asarray(_bucket_np(dist)), axis=1)
    return jnp.where(jnp.asarray(valid), b, NEG).astype(F32)


def _toeplitz_kernel(g_ref, o_ref, *, rows, width, shift, stride, transpose):
    for b in range(g_ref.shape[1]):
        _toeplitz_tile(g_ref[0, b], o_ref.at[0, b], rows, width, shift, stride, transpose)


def _toeplitz_tile(g, o_ref, rows, width, shift, stride, transpose):
    wx = g.shape[1]
    offset = (wx - shift) % wx
    if stride == 1 and not transpose and rows % LANES == 0 and width % LANES == 0 and offset % LANES == 0:
        nr, ncb = rows // LANES, width // LANES
        subs = {}
        for db in range(-(nr - 1), ncb):
            lo = LANES * db + offset - LANES
            gw = jnp.broadcast_to(g[:, lo:lo + 2 * LANES], (LANES, 2 * LANES))
            subs[db] = pltpu.roll(gw, LANES, 1, stride=1, stride_axis=0)[:, :LANES]
        for bi in range(nr):
            for bj in range(ncb):
                o_ref[bi * LANES:(bi + 1) * LANES, bj * LANES:(bj + 1) * LANES] = subs[bj - bi]
        return
    x = jnp.broadcast_to(g, (rows, wx))
    y = pltpu.roll(x, shift, 1, stride=stride, stride_axis=0)[:, :width]
    o_ref[...] = y.T if transpose else y


def toeplitz_tiles(g, rows, width, stride, offset, transpose=False):
    h, nd, wx = g.shape
    assert wx % LANES == 0 and 0 <= offset - stride * (rows - 1) and width - 1 + offset < wx
    shape = (width, rows) if transpose else (rows, width)
    tb = max(b for b in range(1, nd + 1) if nd % b == 0 and b * rows * width * 4 <= max(4 << 20, rows * width * 4))
    return pl.pallas_call(
        functools.partial(_toeplitz_kernel, rows=rows, width=width, shift=(wx - offset) % wx, stride=stride,
                          transpose=transpose),
        out_shape=jax.ShapeDtypeStruct((h, nd) + shape, F32),
        grid=(h, nd // tb),
        in_specs=[pl.BlockSpec((1, tb, 1, wx), lambda a, b: (a, b, 0, 0))],
        out_specs=pl.BlockSpec((1, tb) + shape, lambda a, b: (a, b, 0, 0)),
        name="toeplitz_tiles",
        compiler_params=_cparams(("parallel", "parallel")),
    )(g.reshape(h, nd, 1, wx))


def causal_bias_tiles(table_t, t, n_tiles, window=None, kv_mult=1):
    tk = kv_mult * t
    span = n_tiles * t + tk
    dist = (n_tiles * t) - np.arange(span)
    valid = dist >= 0 if window is None else (dist >= 0) & (dist < window)
    vrev = _bias_by_distance(table_t, dist, valid)
    g = jnp.stack([vrev[:, (n_tiles - d - 1) * t:(n_tiles - d) * t + tk] for d in range(n_tiles)], axis=1)
    return toeplitz_tiles(g, t, tk, 1, t)


def n_bias_tiles(t, tk, nq):
    return min(-(-(FAR + tk - 1) // t) + 1, nq)


def _rmsnorm_kernel(x_ref, g_ref, o_ref):
    x = x_ref[...]
    ms = jnp.mean(x * x, axis=-1, keepdims=True)
    o_ref[...] = ((x * lax.rsqrt(ms + EPS)) * g_ref[...]).astype(o_ref.dtype)


def rmsnorm(x, gain):
    s, d = x.shape
    tr = min(256, s)
    return pl.pallas_call(
        _rmsnorm_kernel,
        out_shape=jax.ShapeDtypeStruct((s, d), BF16),
        grid=(s // tr,),
        in_specs=[pl.BlockSpec((tr, d), lambda i: (i, 0)), pl.BlockSpec((1, d), lambda i: (0, 0))],
        out_specs=pl.BlockSpec((tr, d), lambda i: (i, 0)),
        name="rmsnorm",
        compiler_params=_cparams(("parallel",)),
    )(x, gain.reshape(1, d).astype(F32))


ROW_SUB = 256


def col_blocks(w, tn):
    return w.astype(BF16)[None]


def _proj_kernel(x_ref, w_ref, gain_ref, flag_ref, *refs, nblk, ts, dilations):
    perm_refs = refs[:len(dilations)]
    o_ref = refs[len(dilations)]
    od_refs = refs[len(dilations) + 1:]
    w = w_ref[0].astype(BF16)
    for r0 in range(0, x_ref.shape[0], ts):
        acc = _dot(x_ref[r0:r0 + ts, :], w)
        ys = []
        for c in range(nblk):
            sl = slice(c * LANES, (c + 1) * LANES)
            blk = acc[:, sl]
            ms = jnp.mean(blk * blk, axis=-1, keepdims=True)
            r = jnp.where(flag_ref[:, sl] > 0, lax.rsqrt(ms + EPS), 1.0)
            y = ((blk * r) * gain_ref[:, sl]).astype(o_ref.dtype)
            o_ref[c, r0:r0 + ts, :] = y
            ys.append(y)
        for d, perm_ref, od_ref in zip(dilations, perm_refs, od_refs):
            yp = _dot(perm_ref[...], jnp.concatenate(ys, axis=-1)).astype(o_ref.dtype)
            m = ts // d
            for c in range(nblk):
                for res in range(d):
                    od_ref[c, r0 // d:r0 // d + m, res * LANES:(res + 1) * LANES] = (
                        yp[res * m:(res + 1) * m, c * LANES:(c + 1) * LANES])


def proj_headmajor(x, w, gain, flag, tn, col0=0, dilations=()):
    s, k = x.shape
    n = gain.shape[0]
    tm = min(2048 if w.dtype == BF16 else 1024, s)
    ts = min(ROW_SUB, tm)
    nblk = tn // LANES
    cb0 = col0 // tn
    perms = []
    for d in dilations:
        dst = np.arange(ts)
        src = (dst % (ts // d)) * d + dst // (ts // d)
        perms.append(jnp.asarray((src[:, None] == np.arange(ts)[None, :]).astype(np.float32), BF16))
    out_shape = [jax.ShapeDtypeStruct((n // LANES, s, LANES), BF16)]
    out_specs = [pl.BlockSpec((nblk, tm, LANES), lambda i, j: (j, i, 0))]
    for d in dilations:
        out_shape.append(jax.ShapeDtypeStruct((n // LANES, s // d, d * LANES), BF16))
        out_specs.append(pl.BlockSpec((nblk, tm // d, d * LANES), lambda i, j: (j, i, 0)))
    outs = pl.pallas_call(
        functools.partial(_proj_kernel, nblk=nblk, ts=ts, dilations=tuple(dilations)),
        out_shape=out_shape,
        grid=(s // tm, n // tn),
        in_specs=[pl.BlockSpec((tm, k), lambda i, j: (i, 0)),
                  pl.BlockSpec((1, k, tn), lambda i, j: (0, 0, cb0 + j)),
                  pl.BlockSpec((1, tn), lambda i, j: (0, j)),
                  pl.BlockSpec((1, tn), lambda i, j: (0, j))]
                 + [pl.BlockSpec((ts, ts), lambda i, j: (0, 0))] * len(dilations),
        out_specs=out_specs,
        name="proj_headmajor",
        compiler_params=_cparams(("parallel", "parallel")),
    )(x, w[None], gain.reshape(1, n).astype(F32), flag.reshape(1, n).astype(F32), *perms)
    return outs if dilations else outs[0]


def _mm_res_kernel(x_ref, w_ref, r_ref, o_ref):
    o_ref[...] = r_ref[...] + _dot(x_ref[...], w_ref[0])


def matmul_residual(x, w, res, tm, tn):
    s, k = x.shape
    n = w.shape[1]
    tm = min(tm, s)
    tn = min(tn, n)
    return pl.pallas_call(
        _mm_res_kernel,
        out_shape=jax.ShapeDtypeStruct((s, n), F32),
        grid=(s // tm, n // tn),
        in_specs=[pl.BlockSpec((tm, k), lambda i, j: (i, 0), pipeline_mode=pl.Buffered(1)),
                  pl.BlockSpec((1, k, tn), lambda i, j: (0, 0, j)),
                  pl.BlockSpec((tm, tn), lambda i, j: (i, j))],
        out_specs=pl.BlockSpec((tm, tn), lambda i, j: (i, j)),
        name="matmul_residual",
        compiler_params=_cparams(("parallel", "arbitrary")),
    )(x, col_blocks(w, tn), res)


def _mm2_res_kernel(x1_ref, x2_ref, w_ref, r_ref, o_ref, *, k1):
    acc = _dot(x1_ref[...], w_ref[0, 0:k1, :]) + _dot(x2_ref[...], w_ref[0, k1:, :])
    o_ref[...] = r_ref[...] + acc


def matmul2_residual(x1, x2, w, res, tm, tn):
    s, k1 = x1.shape
    k2 = x2.shape[1]
    n = w.shape[1]
    tm = min(tm, s)
    tn = min(tn, n)
    return pl.pallas_call(
        functools.partial(_mm2_res_kernel, k1=k1),
        out_shape=jax.ShapeDtypeStruct((s, n), F32),
        grid=(s // tm, n // tn),
        in_specs=[pl.BlockSpec((tm, k1), lambda i, j: (i, 0)),
                  pl.BlockSpec((tm, k2), lambda i, j: (i, 0)),
                  pl.BlockSpec((1, k1 + k2, tn), lambda i, j: (0, 0, j)),
                  pl.BlockSpec((tm, tn), lambda i, j: (i, j))],
        out_specs=pl.BlockSpec((tm, tn), lambda i, j: (i, j)),
        name="matmul2_residual",
        compiler_params=_cparams(("parallel", "arbitrary")),
    )(x1, x2, col_blocks(w, tn), res)


def _ffn1_kernel(x_ref, wg_ref, wu_ref, cw_ref, cb_ref, o_ref, carry_ref, *, ts):
    i = pl.program_id(0)
    j = pl.program_id(1)

    @pl.when(i == 0)
    def _():
        carry_ref[j] = jnp.zeros(carry_ref.shape[1:], F32)

    prev = carry_ref[j]
    wg = wg_ref[0].astype(BF16)
    wu = wu_ref[0].astype(BF16)
    cw = cw_ref[...]
    cb = cb_ref[...]
    row = lax.broadcasted_iota(jnp.int32, (ts, wg.shape[1]), 0)
    for r0 in range(0, x_ref.shape[0], ts):
        x = x_ref[r0:r0 + ts, :]
        g = _dot(x, wg)
        u = _dot(x, wu)
        p7 = prev[7:8, :]
        p6 = prev[6:7, :]
        g1 = jnp.where(row == 0, p7, pltpu.roll(g, 1, 0))
        g2 = jnp.where(row == 0, p6, jnp.where(row == 1, p7, pltpu.roll(g, 2, 0)))
        gc = g2 * cw[0:1, :] + g1 * cw[1:2, :] + g * cw[2:3, :] + cb
        o_ref[r0:r0 + ts, :] = ((gc * jax.nn.sigmoid(gc)) * u).astype(o_ref.dtype)
        prev = g[ts - 8:ts, :]
    carry_ref[j] = prev


def ffn_gate_up(h, wg, wu, layer, conv_w, conv_b):
    s, d = h.shape
    f = wg.shape[2]
    tm = min(2048, s)
    tn = 256
    nj = f // tn
    return pl.pallas_call(
        functools.partial(_ffn1_kernel, ts=min(ROW_SUB, tm)),
        out_shape=jax.ShapeDtypeStruct((s, f), BF16),
        grid=(s // tm, nj),
        in_specs=[pl.BlockSpec((tm, d), lambda i, j: (i, 0), pipeline_mode=pl.Buffered(1)),
                  pl.BlockSpec((1, d, tn), lambda i, j: (layer, 0, j)),
                  pl.BlockSpec((1, d, tn), lambda i, j: (layer, 0, j)),
                  pl.BlockSpec((3, tn), lambda i, j: (0, j)),
                  pl.BlockSpec((1, tn), lambda i, j: (0, j))],
        out_specs=pl.BlockSpec((tm, tn), lambda i, j: (i, j)),
        scratch_shapes=[pltpu.VMEM((nj, 8, tn), F32)],
        name="ffn_gate_up",
        compiler_params=_cparams(("arbitrary", "arbitrary")),
    )(h, wg, wu, conv_w.astype(F32), conv_b.reshape(1, f).astype(F32))


def _flash_kernel(qi_ref, kj_ref, bi_ref, fl_ref, *refs, hb, nv, use_sel, rc, diff_lam_init):
    q_ref, k_ref, v_ref, b_ref = refs[:4]
    refs = refs[4:]
    if use_sel:
        sel_ref, en_ref = refs[:2]
        refs = refs[2:]
    if diff_lam_init is not None:
        lam_ref, gain_ref = refs[:2]
        refs = refs[2:]
    o_ref, m_sc, acc_sc, s_sc, p_sc, al_sc = refs[:6]
    if use_sel:
        qa_sc = refs[6]
    per_head_k = k_ref.shape[0] == hb and hb > 1
    per_head_bias = b_ref.shape[0] == hb
    p = pl.program_id(1)
    flags = fl_ref[p]
    dv = nv * LANES
    t = q_ref.shape[1]
    tk = k_ref.shape[1]

    @pl.when((flags & 1) != 0)
    def _():
        m_sc[...] = jnp.full(m_sc.shape, -jnp.inf, F32)
        acc_sc[...] = jnp.zeros(acc_sc.shape, F32)
        if use_sel:
            notsel = (1.0 - sel_ref[0].astype(F32)).astype(BF16)
            for h in range(hb):
                qa_sc[h, :, 0:LANES] = q_ref[h]
                qa_sc[h, :, LANES:] = notsel

    v_aug = jnp.concatenate([v_ref[c] for c in range(nv)] + [jnp.ones((tk, LANES), BF16)], axis=-1)

    for h in range(hb):
        k = k_ref[h if per_head_k else 0]
        if use_sel:
            k = jnp.concatenate([k, en_ref[0]], axis=-1)
        qh = qa_sc[h] if use_sel else q_ref[h]
        s_sc[h] = _dot_nt(qh, k)
        hbias = h if per_head_bias else 0
        for r0 in range(0, t, rc):
            rows = slice(r0, r0 + rc)
            s = s_sc[h, rows, :] + b_ref[hbias, 0, rows, :]
            m_old = m_sc[h, rows, :]
            m_new = jnp.maximum(m_old, jnp.max(s, axis=-1, keepdims=True))
            al_sc[h, rows, :] = jnp.exp(m_old - m_new)
            m_sc[h, rows, :] = m_new
            p_sc[h, rows, :] = jnp.exp(s - jnp.tile(m_new, (1, tk // LANES))).astype(BF16)
        acc_sc[h] = jnp.tile(al_sc[h], (1, nv + 1)) * acc_sc[h] + _dot(p_sc[h], v_aug)

    def normalised(h):
        acc = acc_sc[h]
        return acc[:, 0:dv] / jnp.tile(acc[:, dv:], (1, nv))

    @pl.when((flags & 2) != 0)
    def _():
        if diff_lam_init is None:
            for h in range(hb):
                o_ref[:, h * dv:(h + 1) * dv] = normalised(h).astype(o_ref.dtype)
        else:
            lf = lam_ref[...]
            s1 = jnp.sum(lf[0:1, :] * lf[1:2, :], axis=-1, keepdims=True)
            s2 = jnp.sum(lf[2:3, :] * lf[3:4, :], axis=-1, keepdims=True)
            lam = jnp.exp(s1) - jnp.exp(s2) + diff_lam_init
            o = normalised(0) - lam * normalised(1)
            ms = jnp.mean(o * o, axis=-1, keepdims=True)
            y = ((o * lax.rsqrt(ms + EPS)) * gain_ref[...]) * (1.0 - diff_lam_init)
            o_ref[...] = y.astype(o_ref.dtype)


def _pairs(nq, lookback, max_bias, kv_mult):
    assert lookback is None or kv_mult == 1
    qi, kj, bi, fl = [], [], [], []
    for q in range(nq):
        lo = 0 if lookback is None else max(0, q - lookback)
        hi = q // kv_mult
        for k in range(lo, hi + 1):
            qi.append(q)
            kj.append(k)
            bi.append(min(q - kv_mult * k, max_bias))
            fl.append((1 if k == lo else 0) | (2 if k == hi else 0))
    return [jnp.asarray(np.asarray(a, np.int32)) for a in (qi, kj, bi, fl)]


def flash_attention(q, k, v, bias, *, name, t, n_groups, hb, nv, q_map, k_map, v_map, b_map,
                    lookback=None, sel=None, key_neg=None, diff_params=None):
    s = q.shape[1]
    nq = s // t
    tk = bias.shape[3]
    qi, kj, bi, fl = _pairs(nq, lookback, bias.shape[1] - 1, tk // t)
    n_pairs = int(qi.shape[0])
    dv = nv * LANES
    use_sel = sel is not None
    diff = diff_params is not None
    in_specs = [
        pl.BlockSpec((hb, t, LANES), lambda g, p, qi, kj, bi, fl: q_map(g, qi[p])),
        pl.BlockSpec((hb if diff else 1, tk, LANES), lambda g, p, qi, kj, bi, fl: k_map(g, kj[p])),
        pl.BlockSpec((nv, tk, LANES), lambda g, p, qi, kj, bi, fl: v_map(g, kj[p])),
        pl.BlockSpec((1 if diff else hb, 1, t, tk), lambda g, p, qi, kj, bi, fl: b_map(g, bi[p])),
    ]
    args = [q, k, v, bias]
    scratch = [pltpu.VMEM((hb, t, LANES), F32), pltpu.VMEM((hb, t, dv + LANES), F32),
               pltpu.VMEM((hb, t, tk), F32), pltpu.VMEM((hb, t, tk), BF16), pltpu.VMEM((hb, t, LANES), F32)]
    if use_sel:
        nselp = sel.shape[2]
        in_specs += [
            pl.BlockSpec((1, t, nselp), lambda g, p, qi, kj, bi, fl: (g, qi[p], 0)),
            pl.BlockSpec((1, tk, nselp), lambda g, p, qi, kj, bi, fl: (kj[p], 0, 0)),
        ]
        args += [sel, key_neg]
        scratch.append(pltpu.VMEM((hb, t, LANES + nselp), BF16))
    lam_init = None
    if diff:
        a_lambda, subln_gain, lam_init = diff_params
        in_specs += [
            pl.BlockSpec((4, HEAD_DIM), lambda g, p, qi, kj, bi, fl: (0, 0)),
            pl.BlockSpec((1, dv), lambda g, p, qi, kj, bi, fl: (0, 0)),
        ]
        args += [a_lambda.astype(F32), subln_gain.reshape(1, dv).astype(F32)]
    out_w = dv if diff else hb * dv
    grid_spec = pltpu.PrefetchScalarGridSpec(
        num_scalar_prefetch=4,
        grid=(n_groups, n_pairs),
        in_specs=in_specs,
        out_specs=pl.BlockSpec((t, out_w), lambda g, p, qi, kj, bi, fl: (qi[p], g)),
        scratch_shapes=scratch,
    )
    rc = max(8, min(t, (16 * 8 * LANES) // tk))
    return pl.pallas_call(
        functools.partial(_flash_kernel, hb=hb, nv=nv, use_sel=use_sel, rc=rc, diff_lam_init=lam_init),
        out_shape=jax.ShapeDtypeStruct((s, n_groups * out_w), BF16),
        grid_spec=grid_spec,
        name=name,
        compiler_params=_cparams(("parallel", "arbitrary")),
    )(qi, kj, bi, fl, *args)


def _dilated_kernel(q_ref, kp_ref, kc_ref, vp_ref, vc_ref, b_ref, o_ref, lse_ref, s_sc, p_sc, m_sc):
    span = q_ref.shape[1]
    rc = min(64, span)
    no_prev = jnp.where(pl.program_id(1) == 0, NEG, 0.0)
    col = lax.broadcasted_iota(jnp.int32, (1, 2 * span), 1)
    prev_mask = jnp.where(col < span, no_prev, 0.0)
    ones = jnp.ones((2 * span, LANES), BF16)
    for h in range(B_HEADS):
        keys = jnp.concatenate([kp_ref[h], kc_ref[h]], axis=0)
        s_sc[h] = _dot_nt(q_ref[h], keys)
        for r0 in range(0, span, rc):
            rows = slice(r0, r0 + rc)
            s = s_sc[h, rows, :] + (b_ref[h, 0, rows, :] + prev_mask)
            m = jnp.max(s, axis=-1, keepdims=True)
            m_sc[h, rows, :] = jnp.broadcast_to(m, (rc, LANES))
            p_sc[h, rows, :] = jnp.exp(s - m).astype(BF16)
        vals = jnp.concatenate([jnp.concatenate([vp_ref[h], vc_ref[h]], axis=0), ones], axis=-1)
        pv = _dot(p_sc[h], vals)
        l = pv[:, HEAD_DIM:]
        sl = slice(h * HEAD_DIM, (h + 1) * HEAD_DIM)
        o_ref[:, sl] = (pv[:, 0:HEAD_DIM] / l).astype(o_ref.dtype)
        lse_ref[:, sl] = m_sc[h] + jnp.log(l)


def dilated_group(x, bias, dilation, span, q_blk, k_blk, v_blk):
    nh, l, _ = x.shape
    s = l * dilation
    nb = l // span
    w = B_HEADS * HEAD_DIM
    hspec = lambda blk, prev: pl.BlockSpec(
        (B_HEADS, span, LANES),
        (lambda r, n: (blk, jnp.maximum(n - 1, 0), r)) if prev else (lambda r, n: (blk, n, r)))
    o, lse = pl.pallas_call(
        _dilated_kernel,
        out_shape=[jax.ShapeDtypeStruct((l, dilation * w), BF16), jax.ShapeDtypeStruct((l, dilation * w), F32)],
        grid=(dilation, nb),
        in_specs=[hspec(q_blk, False), hspec(k_blk, True), hspec(k_blk, False),
                  hspec(v_blk, True), hspec(v_blk, False),
                  pl.BlockSpec((B_HEADS, 1, span, 2 * span), lambda r, n: (0, 0, 0, 0))],
        out_specs=[pl.BlockSpec((span, w), lambda r, n: (n, r))] * 2,
        scratch_shapes=[pltpu.VMEM((B_HEADS, span, 2 * span), F32),
                        pltpu.VMEM((B_HEADS, span, 2 * span), BF16),
                        pltpu.VMEM((B_HEADS, span, LANES), F32)],
        name="dilated_group",
        compiler_params=_cparams(("parallel", "parallel")),
    )(x, x, x, x, x, bias)
    return o.reshape(s, w), lse.reshape(s, w)


def _mix3_kernel(o1, o2, o3, l1, l2, l3, out_ref):
    a, b, c = l1[...], l2[...], l3[...]
    m = jnp.maximum(jnp.maximum(a, b), c)
    ea, eb, ec = jnp.exp(a - m), jnp.exp(b - m), jnp.exp(c - m)
    num = ea * o1[...].astype(F32) + eb * o2[...].astype(F32) + ec * o3[...].astype(F32)
    out_ref[...] = (num / (ea + eb + ec)).astype(out_ref.dtype)


def mix_dilated(os, lses):
    s, w = os[0].shape
    ts = min(256, s)
    spec = pl.BlockSpec((ts, w), lambda i: (i, 0))
    return pl.pallas_call(
        _mix3_kernel,
        out_shape=jax.ShapeDtypeStruct((s, w), BF16),
        grid=(s // ts,),
        in_specs=[spec] * 6,
        out_specs=spec,
        name="mix_dilated",
        compiler_params=_cparams(("parallel",)),
    )(*os, *lses)


def _compress_kernel(x_ref, pe_ref, w1_ref, w2_ref, gain_ref, o_ref):
    kv = pl.program_id(0)
    x = x_ref[0, 0].astype(F32)
    nc = x.shape[0]
    a = _dot((x + pe_ref[0, 0]).astype(BF16), w1_ref[0, 0])
    b = _dot((x + pe_ref[0, 1]).astype(BF16), w1_ref[0, 1])
    hid = jax.nn.gelu(a + pltpu.roll(b, nc - 1, 0))
    c = _dot(hid.astype(BF16), w2_ref[0])
    ms = jnp.mean(c * c, axis=-1, keepdims=True)
    normed = (c * lax.rsqrt(ms + EPS)) * gain_ref[...]
    o_ref[0, 0] = jnp.where(kv == 0, normed, c).astype(o_ref.dtype)


def compress_kv(xc, pe, w1, w2, gain):
    _, g, nc, cw = xc.shape
    hid = w1.shape[-1]
    return pl.pallas_call(
        _compress_kernel,
        out_shape=jax.ShapeDtypeStruct((2, g, nc, HEAD_DIM), BF16),
        grid=(2, g),
        in_specs=[pl.BlockSpec((1, 1, nc, cw), lambda a, b: (a, b, 0, 0)),
                  pl.BlockSpec((1, 2, 1, cw), lambda a, b: (a, 0, 0, 0)),
                  pl.BlockSpec((1, 2, cw, hid), lambda a, b: (a, 0, 0, 0)),
                  pl.BlockSpec((1, hid, HEAD_DIM), lambda a, b: (a, 0, 0)),
                  pl.BlockSpec((1, HEAD_DIM), lambda a, b: (0, 0))],
        out_specs=pl.BlockSpec((1, 1, nc, HEAD_DIM), lambda a, b: (a, b, 0, 0)),
        name="compress_kv",
        compiler_params=_cparams(("parallel", "parallel")),
    )(xc, pe, w1, w2, gain.reshape(1, HEAD_DIM).astype(F32))


def _cmp_kernel(q_ref, k_ref, v_ref, *rest, nkt, n_top):
    bias_refs = rest[:nkt]
    ov_ref, oc_ref, sel_ref, imp_sc, s_sc, p_sc, impn_sc = rest[nkt:]
    qi = pl.program_id(1)
    tq = q_ref.shape[1]
    tile_pos = LANES * CMP_STRIDE
    n_live = jnp.minimum(lax.shift_right_logical(qi * tq + (tq - CMP_LEN), int(math.log2(tile_pos))) + 1, nkt)

    def attend(nl):
        w = nl * LANES
        rc = max(8, min(tq, 1 << int(math.log2((16 * 8 * LANES) // w))))
        k = k_ref[0, 0, 0:w, :]
        v = v_ref[0, 0, 0:w, :]
        for h in range(C_HPG):
            s_sc[h, :, 0:w] = _dot_nt(q_ref[h], k)
            for r0 in range(0, tq, rc):
                rows = slice(r0, r0 + rc)
                if nl == 1:
                    bias = bias_refs[0][h, 0, rows, :]
                else:
                    bias = jnp.concatenate([b[h, 0, rows, :] for b in bias_refs[:nl]], axis=-1)
                s = s_sc[h, rows, 0:w] + bias
                m = jnp.max(s, axis=-1, keepdims=True)
                e = jnp.exp(s - m)
                pc = e * jnp.where(m > 0.5 * NEG, 1.0 / jnp.sum(e, axis=-1, keepdims=True), 0.0)
                p_sc[h, rows, 0:w] = pc.astype(BF16)
                if h == 0:
                    imp_sc[rows, 0:w] = pc
                else:
                    imp_sc[rows, 0:w] += pc
            oc_ref[:, h * HEAD_DIM:(h + 1) * HEAD_DIM] = _dot(p_sc[h, :, 0:w], v).astype(oc_ref.dtype)
        imp = imp_sc[:, 0:w]
        hi = imp.astype(BF16)
        lo = (imp - hi.astype(F32)).astype(BF16)
        ov = ov_ref[0:w, :]
        impn_sc[...] = _dot(hi, ov) + _dot(lo, ov)

    for nl in range(1, nkt + 1):
        pl.when(n_live == nl)(functools.partial(attend, nl))
    score_in = impn_sc[...].T
    shape = score_in.shape
    t = qi * tq + lax.broadcasted_iota(jnp.int32, shape, 1)
    n = lax.broadcasted_iota(jnp.int32, shape, 0)
    cur = lax.shift_right_logical(t, int(math.log2(SEL_LEN)))
    forced = (n == 0) | (n == cur) | (n == cur - 1)
    valid = n <= cur
    score = jnp.where(valid, jnp.where(forced, FORCE, score_in), NEG)
    nf = n.astype(F32)
    sel = jnp.zeros(shape, F32)
    for _ in range(n_top):
        m = jnp.max(score, axis=0, keepdims=True)
        first = jnp.min(jnp.where(score == m, nf, 1e9), axis=0, keepdims=True)
        pick = nf == first
        sel = jnp.where(pick, 1.0, sel)
        score = jnp.where(pick, -jnp.inf, score)
    sel_ref[0] = jnp.where(valid, sel, 0.0).T.astype(sel_ref.dtype)


def cmp_attention(q_hm, kvcmp, bias, overlap, tq, n_top):
    s = q_hm.shape[1]
    ncp = kvcmp.shape[2]
    nkt = ncp // LANES
    nselp = overlap.shape[1]
    nd = bias.shape[1]
    per_tile = (LANES * CMP_STRIDE) // tq

    def bias_spec(kt):
        return pl.BlockSpec(
            (C_HPG, 1, tq, LANES),
            lambda g, i: (g, jnp.clip(i - per_tile * kt, -1, nd - 2) + 1, 0, 0))

    w = C_HEADS * HEAD_DIM
    return pl.pallas_call(
        functools.partial(_cmp_kernel, nkt=nkt, n_top=n_top),
        out_shape=[jax.ShapeDtypeStruct((s, w), BF16),
                   jax.ShapeDtypeStruct((C_GROUPS, s, nselp), BF16)],
        grid=(C_GROUPS, s // tq),
        in_specs=[pl.BlockSpec((C_HPG, tq, LANES), lambda g, i: (g, i, 0)),
                  pl.BlockSpec((1, 1, ncp, LANES), lambda g, i: (0, g, 0, 0)),
                  pl.BlockSpec((1, 1, ncp, LANES), lambda g, i: (1, g, 0, 0))]
                 + [bias_spec(kt) for kt in range(nkt)]
                 + [pl.BlockSpec((ncp, nselp), lambda g, i: (0, 0))],
        out_specs=[pl.BlockSpec((tq, w // C_GROUPS), lambda g, i: (i, g)),
                   pl.BlockSpec((1, tq, nselp), lambda g, i: (g, i, 0))],
        scratch_shapes=[pltpu.VMEM((tq, ncp), F32), pltpu.VMEM((C_HPG, tq, ncp), F32),
                        pltpu.VMEM((C_HPG, tq, ncp), BF16), pltpu.VMEM((tq, nselp), F32)],
        name="cmp_attention",
        compiler_params=_cparams(("parallel", "parallel")),
    )(q_hm, kvcmp, kvcmp, *([bias] * nkt), overlap)


def _gate_kernel(oc_ref, os_ref, ow_ref, gt_ref, o_ref):
    gt = jax.nn.sigmoid(gt_ref[0].astype(F32))
    for h in range(C_HEADS):
        sl = slice(h * HEAD_DIM, (h + 1) * HEAD_DIM)
        g0 = gt[:, h:h + 1]
        g1 = gt[:, C_HEADS + h:C_HEADS + h + 1]
        g2 = gt[:, 2 * C_HEADS + h:2 * C_HEADS + h + 1]
        mixed = (g0 * oc_ref[:, sl].astype(F32) + g1 * os_ref[:, sl].astype(F32)
                 + g2 * ow_ref[:, sl].astype(F32))
        o_ref[:, sl] = mixed.astype(o_ref.dtype)


def gate_merge(oc, osel, ow, proj_hm, gate_blk):
    s, w = oc.shape
    ts = min(256, s)
    spec = pl.BlockSpec((ts, w), lambda i: (i, 0))
    return pl.pallas_call(
        _gate_kernel,
        out_shape=jax.ShapeDtypeStruct((s, w), BF16),
        grid=(s // ts,),
        in_specs=[spec, spec, spec, pl.BlockSpec((1, ts, LANES), lambda i: (gate_blk, i, 0))],
        out_specs=spec,
        name="gate_merge",
        compiler_params=_cparams(("parallel",)),
    )(oc, osel, ow, proj_hm)


def _tile_gain(g, reps, scale=1.0):
    return jnp.tile(g.astype(F32) * scale, reps)


def even_mixer(x, h, w_in, a_qk_gain, a_lambda, a_subln_gain, b_qk_gain, w_out, rel_table, lam_init):
    s = h.shape[0]
    scale = HEAD_DIM ** -0.5
    na = 2 * A_HEADS
    ones = lambda n: jnp.ones((n * HEAD_DIM,), F32)
    zeros = lambda n: jnp.zeros((n * HEAD_DIM,), F32)
    d = h.shape[1]
    wa = na * HEAD_DIM
    table_t = rel_table.astype(F32).T

    def pair_cols(w):
        return w.reshape(d, 2, A_HEADS, HEAD_DIM).transpose(0, 2, 1, 3).reshape(d, wa)

    w_a = jnp.concatenate([pair_cols(w_in[:, 0:wa]), pair_cols(w_in[:, wa:2 * wa]), w_in[:, 2 * wa:3 * wa]],
                          axis=1).astype(BF16)
    gain_a = jnp.concatenate([_tile_gain(a_qk_gain[0], na, scale), _tile_gain(a_qk_gain[1], na), ones(na)])
    flag_a = jnp.concatenate([ones(2 * na), zeros(na)])
    proj_a = proj_headmajor(h, w_a, gain_a, flag_a, tn=512)
    t = min(1024, s)
    bias_a = causal_bias_tiles(table_t[:A_HEADS], t, n_bias_tiles(t, t, s // t))
    ao = flash_attention(
        proj_a, proj_a, proj_a, bias_a, name="diff_attention", t=t, n_groups=A_HEADS, hb=2, nv=2,
        q_map=lambda g, qb: (g, qb, 0),
        k_map=lambda g, kb: (A_HEADS + g, kb, 0),
        v_map=lambda g, kb: (2 * A_HEADS + g, kb, 0),
        b_map=lambda g, bt: (g, bt, 0, 0),
        diff_params=(a_lambda, a_subln_gain, lam_init))

    gain_b = jnp.concatenate([_tile_gain(b_qk_gain[0], B_HEADS, scale), _tile_gain(b_qk_gain[1], B_HEADS),
                              ones(B_HEADS)])
    flag_b = jnp.concatenate([ones(2 * B_HEADS), zeros(B_HEADS)])
    dilations = sorted({dil for _, dil in B_CONFIGS if dil > 1})
    projs = proj_headmajor(h, w_in, gain_b, flag_b, tn=512, col0=3 * wa, dilations=dilations)
    proj_by_dilation = dict(zip([1] + dilations, projs if dilations else [projs]))
    hb0 = 0
    os, lses = [], []
    for window, dilation in B_CONFIGS:
        proj = proj_by_dilation[dilation]
        span = window // dilation
        j = 2 * span - np.arange(3 * span)
        g_b = _bias_by_distance(table_t[A_HEADS:A_HEADS + B_HEADS], j * dilation, (j >= 0) & (j <= span))
        bias_b = toeplitz_tiles(g_b[:, None, :], span, 2 * span, 1, span)
        o, lse = dilated_group(proj, bias_b, dilation, span, hb0, hb0 + 1, hb0 + 2)
        os.append(o)
        lses.append(lse)
    bo = mix_dilated(os, lses)
    return matmul2_residual(ao, bo, w_out, x, tm=1024, tn=512)


def nsa_mixer(x, h, w_in, c_qk_gain, c_cmp_pe, c_cmp_w1, c_cmp_w2, w_out, rel_table):
    s, d = h.shape
    scale = HEAD_DIM ** -0.5
    g = C_GROUPS
    n_in = w_in.shape[1]
    n_pad = -(-n_in // 640) * 640
    ones = lambda n: jnp.ones((n,), F32)
    zeros = lambda n: jnp.zeros((n,), F32)
    kvw = g * HEAD_DIM
    gain = jnp.concatenate([_tile_gain(c_qk_gain[0], C_HEADS, scale), ones(2 * kvw),
                            _tile_gain(c_qk_gain[2], g), ones(kvw), _tile_gain(c_qk_gain[3], g), ones(kvw),
                            ones(n_pad - C_HEADS * HEAD_DIM - 6 * kvw)])
    flag = jnp.concatenate([ones(C_HEADS * HEAD_DIM), zeros(2 * kvw), ones(kvw), zeros(kvw), ones(kvw),
                            zeros(kvw), zeros(n_pad - C_HEADS * HEAD_DIM - 6 * kvw)])
    w_pad = jnp.pad(w_in.astype(BF16), ((0, 0), (0, n_pad - n_in)))
    proj = proj_headmajor(h, w_pad, gain, flag, tn=640)
    kv0 = C_HEADS
    table_t = rel_table.astype(F32).T[:C_HEADS]

    nc = s // CMP_STRIDE
    cw = CMP_STRIDE * HEAD_DIM
    xc = proj[kv0:kv0 + 2 * g].reshape(2, g, nc, cw)
    pe = c_cmp_pe.astype(F32).reshape(2, 2, 1, cw)
    w1 = c_cmp_w1.astype(BF16).reshape(2, 2, cw, c_cmp_w1.shape[-1])
    kvcmp = compress_kv(xc, pe, w1, c_cmp_w2.astype(BF16), c_qk_gain[1])
    if nc % LANES:
        kvcmp = jnp.pad(kvcmp, ((0, 0), (0, 0), (0, LANES - nc % LANES), (0, 0)))
    ncp = kvcmp.shape[2]

    tq = min(256, s)
    n_cmp = (s - CMP_LEN) // CMP_STRIDE + 1
    n_sel = s // SEL_LEN
    nselp = -(-n_sel // LANES) * LANES
    n_top = min(SEL_TOPK, n_sel)
    off = CMP_STRIDE * (LANES - 1)
    far_tile = -(-(FAR + off + CMP_LEN - 1) // tq)
    nd = far_tile + 2
    wx = -(-(tq + off) // LANES) * LANES
    dist_c = np.arange(tq * (nd - 1) + wx) - (tq + off + CMP_LEN - 1)
    vec_c = _bias_by_distance(table_t, dist_c, dist_c >= 0)
    g_c = jnp.stack([vec_c[:, tq * d:tq * d + wx] for d in range(nd)], axis=1)
    bias_c = toeplitz_tiles(g_c, LANES, tq, CMP_STRIDE, off, transpose=True)
    c_start = np.arange(ncp) * CMP_STRIDE
    s_start = np.arange(nselp) * SEL_LEN
    ov = ((c_start[:, None] < s_start[None, :] + SEL_LEN) & (c_start[:, None] + CMP_LEN > s_start[None, :])
          & (np.arange(ncp)[:, None] < n_cmp) & (np.arange(nselp)[None, :] < n_sel))
    oc, sel = cmp_attention(proj, kvcmp, bias_c, jnp.asarray(ov.astype(np.float32), BF16), tq, n_top)

    t = tq
    kv_mult = 2 if s % (2 * t) == 0 else 1
    tk = kv_mult * t
    bias_s = causal_bias_tiles(table_t, t, n_bias_tiles(t, tk, s // t), kv_mult=kv_mult)
    key_blk = np.arange(s) // SEL_LEN
    key_neg = np.where(key_blk[:, None] == np.arange(nselp)[None, :], NEG, 0.0).reshape(s // tk, tk, nselp)
    osel = flash_attention(
        proj, proj, proj, bias_s, name="selected_attention", t=t, n_groups=g, hb=C_HPG, nv=1,
        q_map=lambda gg, qb: (gg, qb, 0),
        k_map=lambda gg, kb: (kv0 + 2 * g + gg, kb, 0),
        v_map=lambda gg, kb: (kv0 + 3 * g + gg, kb, 0),
        b_map=lambda gg, bt: (gg, bt, 0, 0),
        sel=sel, key_neg=jnp.asarray(key_neg.astype(np.float32), BF16))

    look = -(-(C_WINDOW - 1) // t)
    ndw = min(look + 1, s // t)
    bias_w = causal_bias_tiles(table_t, t, ndw, window=C_WINDOW)
    ow = flash_attention(
        proj, proj, proj, bias_w, name="window_attention", t=t, n_groups=g, hb=C_HPG, nv=1, lookback=look,
        q_map=lambda gg, qb: (gg, qb, 0),
        k_map=lambda gg, kb: (kv0 + 4 * g + gg, kb, 0),
        v_map=lambda gg, kb: (kv0 + 5 * g + gg, kb, 0),
        b_map=lambda gg, bt: (gg, bt, 0, 0))

    o = gate_merge(oc, osel, ow, proj, kv0 + 6 * g)
    return matmul_residual(o, w_out, x, tm=1024, tn=512)


def conv_ffn(x, h, w_gate, w_up, layer, conv_w, conv_b, w_down):
    act = ffn_gate_up(h, w_gate, w_up, layer, conv_w, conv_b)
    return matmul_residual(act, w_down, x, tm=1024, tn=256)


def kernel(x, rel_table, ev_norm, ev_w_in, a_qk_gain, a_lambda, a_subln_gain, b_qk_gain, ev_w_out,
           od_norm, od_w_in, c_qk_gain, c_cmp_pe, c_cmp_w1, c_cmp_w2, od_w_out,
           ffn_norm, ffn_w_gate, ffn_w_up, ffn_conv_w, ffn_conv_b, ffn_w_down):
    b, s, d = x.shape
    depth = ffn_norm.shape[0]
    outs = []
    for bi in range(b):
        y = x[bi].astype(F32)
        for i in range(depth):
            if i % 2 == 0:
                e = i // 2
                lam_init = 0.8 - 0.6 * math.exp(-0.3 * i)
                y = even_mixer(y, rmsnorm(y, ev_norm[e]), ev_w_in[e], a_qk_gain[e], a_lambda[e],
                               a_subln_gain[e], b_qk_gain[e], ev_w_out[e], rel_table, lam_init)
            else:
                o = i // 2
                y = nsa_mixer(y, rmsnorm(y, od_norm[o]), od_w_in[o], c_qk_gain[o], c_cmp_pe[o],
                              c_cmp_w1[o], c_cmp_w2[o], od_w_out[o], rel_table)
            y = conv_ffn(y, rmsnorm(y, ffn_norm[i]), ffn_w_gate, ffn_w_up, i, ffn_conv_w[i],
                         ffn_conv_b[i], ffn_w_down[i])
        outs.append(y)
    return jnp.stack(outs).astype(x.dtype)
```

```python
import functools
import math

import numpy as np
import jax
import jax.numpy as jnp
from jax import lax
from jax.experimental import pallas as pl
from jax.experimental.pallas import tpu as pltpu

HEAD_DIM = 128
A_HEADS = 8
B_HEADS = 16
B_CONFIGS = ((128, 1), (512, 4), (2048, 16))
C_HEADS = 32
C_GROUPS = 2
C_HPG = C_HEADS // C_GROUPS
CMP_LEN = 32
CMP_STRIDE = 16
SEL_LEN = 64
SEL_TOPK = 16
C_WINDOW = 512
REL_BUCKETS = 32
REL_MAX_DIST = 2048
EPS = 1e-6
NEG = -1e30
FORCE = 1e9

LANES = 128
VMEM_LIMIT = 56 * 1024 * 1024

F32 = jnp.float32
BF16 = jnp.bfloat16


def _cparams(sem):
    return pltpu.CompilerParams(dimension_semantics=sem, vmem_limit_bytes=VMEM_LIMIT)


def _dot(a, b):
    return jnp.dot(a, b, preferred_element_type=F32)


def _dot_nt(a, b):
    return lax.dot_general(a, b, (((1,), (1,)), ((), ())), preferred_element_type=F32)


def _bucket_np(dist):
    n = np.maximum(dist, 0).astype(np.int32)
    max_exact = REL_BUCKETS // 2
    nf = np.maximum(n, 1).astype(np.float32)
    ratio = np.log(nf / np.float32(max_exact)) / np.float32(math.log(REL_MAX_DIST / max_exact))
    large = np.minimum(max_exact + (ratio * np.float32(REL_BUCKETS - max_exact)).astype(np.int32),
                       REL_BUCKETS - 1)
    return np.where(n < max_exact, n, large).astype(np.int32)


def _far_distance():
    b = _bucket_np(np.arange(4 * REL_MAX_DIST))
    return int(np.max(np.nonzero(b != REL_BUCKETS - 1)[0])) + 1


FAR = _far_distance()


def _bias_by_distance(table_t, dist, valid):
    b = jnp.take(table_t, jnp.asarray(_bucket_np(dist)), axis=1)
    return jnp.where(jnp.asarray(valid), b, NEG).astype(F32)


def _toeplitz_kernel(g_ref, o_ref, *, rows, width, shift, stride, transpose):
    for b in range(g_ref.shape[1]):
        _toeplitz_tile(g_ref[0, b], o_ref.at[0, b], rows, width, shift, stride, transpose)


def _toeplitz_tile(g, o_ref, rows, width, shift, stride, transpose):
    wx = g.shape[1]
    offset = (wx - shift) % wx
    if stride == 1 and not transpose and rows % LANES == 0 and width % LANES == 0 and offset % LANES == 0:
        nr, ncb = rows // LANES, width // LANES
        subs = {}
        for db in range(-(nr - 1), ncb):
            lo = LANES * db + offset - LANES
            gw = jnp.broadcast_to(g[:, lo:lo + 2 * LANES], (LANES, 2 * LANES))
            subs[db] = pltpu.roll(gw, LANES, 1, stride=1, stride_axis=0)[:, :LANES]
        for bi in range(nr):
            for bj in range(ncb):
                o_ref[bi * LANES:(bi + 1) * LANES, bj * LANES:(bj + 1) * LANES] = subs[bj - bi]
        return
    x = jnp.broadcast_to(g, (rows, wx))
    y = pltpu.roll(x, shift, 1, stride=stride, stride_axis=0)[:, :width]
    o_ref[...] = y.T if transpose else y


def toeplitz_tiles(g, rows, width, stride, offset, transpose=False):
    h, nd, wx = g.shape
    assert wx % LANES == 0 and 0 <= offset - stride * (rows - 1) and width - 1 + offset < wx
    shape = (width, rows) if transpose else (rows, width)
    tb = max(b for b in range(1, nd + 1) if nd % b == 0 and b * rows * width * 4 <= max(4 << 20, rows * width * 4))
    return pl.pallas_call(
        functools.partial(_toeplitz_kernel, rows=rows, width=width, shift=(wx - offset) % wx, stride=stride,
                          transpose=transpose),
        out_shape=jax.ShapeDtypeStruct((h, nd) + shape, F32),
        grid=(h, nd // tb),
        in_specs=[pl.BlockSpec((1, tb, 1, wx), lambda a, b: (a, b, 0, 0))],
        out_specs=pl.BlockSpec((1, tb) + shape, lambda a, b: (a, b, 0, 0)),
        name="toeplitz_tiles",
        compiler_params=_cparams(("parallel", "parallel")),
    )(g.reshape(h, nd, 1, wx))


def causal_bias_tiles(table_t, t, n_tiles, window=None, kv_mult=1):
    tk = kv_mult * t
    span = n_tiles * t + tk
    dist = (n_tiles * t) - np.arange(span)
    valid = dist >= 0 if window is None else (dist >= 0) & (dist < window)
    vrev = _bias_by_distance(table_t, dist, valid)
    g = jnp.stack([vrev[:, (n_tiles - d - 1) * t:(n_tiles - d) * t + tk] for d in range(n_tiles)], axis=1)
    return toeplitz_tiles(g, t, tk, 1, t)


def n_bias_tiles(t, tk, nq):
    return min(-(-(FAR + tk - 1) // t) + 1, nq)


def _rmsnorm_kernel(x_ref, g_ref, o_ref):
    x = x_ref[...]
    ms = jnp.mean(x * x, axis=-1, keepdims=True)
    o_ref[...] = ((x * lax.rsqrt(ms + EPS)) * g_ref[...]).astype(o_ref.dtype)


def rmsnorm(x, gain):
    s, d = x.shape
    tr = min(256, s)
    return pl.pallas_call(
        _rmsnorm_kernel,
        out_shape=jax.ShapeDtypeStruct((s, d), BF16),
        grid=(s // tr,),
        in_specs=[pl.BlockSpec((tr, d), lambda i: (i, 0)), pl.BlockSpec((1, d), lambda i: (0, 0))],
        out_specs=pl.BlockSpec((tr, d), lambda i: (i, 0)),
        name="rmsnorm",
        compiler_params=_cparams(("parallel",)),
    )(x, gain.reshape(1, d).astype(F32))


ROW_SUB = 256


def col_blocks(w, tn):
    return w.astype(BF16)[None]


def _proj_kernel(x_ref, w_ref, gain_ref, flag_ref, *refs, nblk, ts, dilations):
    perm_refs = refs[:len(dilations)]
    o_ref = refs[len(dilations)]
    od_refs = refs[len(dilations) + 1:]
    w = w_ref[0].astype(BF16)
    for r0 in range(0, x_ref.shape[0], ts):
        acc = _dot(x_ref[r0:r0 + ts, :], w)
        ys = []
        for c in range(nblk):
            sl = slice(c * LANES, (c + 1) * LANES)
            blk = acc[:, sl]
            ms = jnp.mean(blk * blk, axis=-1, keepdims=True)
            r = jnp.where(flag_ref[:, sl] > 0, lax.rsqrt(ms + EPS), 1.0)
            y = ((blk * r) * gain_ref[:, sl]).astype(o_ref.dtype)
            o_ref[c, r0:r0 + ts, :] = y
            ys.append(y)
        for d, perm_ref, od_ref in zip(dilations, perm_refs, od_refs):
            yp = _dot(perm_ref[...], jnp.concatenate(ys, axis=-1)).astype(o_ref.dtype)
            m = ts // d
            for c in range(nblk):
                for res in range(d):
                    od_ref[c, r0 // d:r0 // d + m, res * LANES:(res + 1) * LANES] = (
                        yp[res * m:(res + 1) * m, c * LANES:(c + 1) * LANES])


def proj_headmajor(x, w, gain, flag, tn, col0=0, dilations=()):
    s, k = x.shape
    n = gain.shape[0]
    tm = min(2048 if w.dtype == BF16 else 1024, s)
    ts = min(ROW_SUB, tm)
    nblk = tn // LANES
    cb0 = col0 // tn
    perms = []
    for d in dilations:
        dst = np.arange(ts)
        src = (dst % (ts // d)) * d + dst // (ts // d)
        perms.append(jnp.asarray((src[:, None] == np.arange(ts)[None, :]).astype(np.float32), BF16))
    out_shape = [jax.ShapeDtypeStruct((n // LANES, s, LANES), BF16)]
    out_specs = [pl.BlockSpec((nblk, tm, LANES), lambda i, j: (j, i, 0))]
    for d in dilations:
        out_shape.append(jax.ShapeDtypeStruct((n // LANES, s // d, d * LANES), BF16))
        out_specs.append(pl.BlockSpec((nblk, tm // d, d * LANES), lambda i, j: (j, i, 0)))
    outs = pl.pallas_call(
        functools.partial(_proj_kernel, nblk=nblk, ts=ts, dilations=tuple(dilations)),
        out_shape=out_shape,
        grid=(s // tm, n // tn),
        in_specs=[pl.BlockSpec((tm, k), lambda i, j: (i, 0)),
                  pl.BlockSpec((1, k, tn), lambda i, j: (0, 0, cb0 + j)),
                  pl.BlockSpec((1, tn), lambda i, j: (0, j)),
                  pl.BlockSpec((1, tn), lambda i, j: (0, j))]
                 + [pl.BlockSpec((ts, ts), lambda i, j: (0, 0))] * len(dilations),
        out_specs=out_specs,
        name="proj_headmajor",
        compiler_params=_cparams(("parallel", "parallel")),
    )(x, w[None], gain.reshape(1, n).astype(F32), flag.reshape(1, n).astype(F32), *perms)
    return outs if dilations else outs[0]


def _mm_res_kernel(x_ref, w_ref, r_ref, o_ref):
    o_ref[...] = r_ref[...] + _dot(x_ref[...], w_ref[0])


def matmul_residual(x, w, res, tm, tn):
    s, k = x.shape
    n = w.shape[1]
    tm = min(tm, s)
    tn = min(tn, n)
    return pl.pallas_call(
        _mm_res_kernel,
        out_shape=jax.ShapeDtypeStruct((s, n), F32),
        grid=(s // tm, n // tn),
        in_specs=[pl.BlockSpec((tm, k), lambda i, j: (i, 0), pipeline_mode=pl.Buffered(1)),
                  pl.BlockSpec((1, k, tn), lambda i, j: (0, 0, j)),
                  pl.BlockSpec((tm, tn), lambda i, j: (i, j))],
        out_specs=pl.BlockSpec((tm, tn), lambda i, j: (i, j)),
        name="matmul_residual",
        compiler_params=_cparams(("parallel", "arbitrary")),
    )(x, col_blocks(w, tn), res)


def _mm2_res_kernel(x1_ref, x2_ref, w_ref, r_ref, o_ref, *, k1):
    acc = _dot(x1_ref[...], w_ref[0, 0:k1, :]) + _dot(x2_ref[...], w_ref[0, k1:, :])
    o_ref[...] = r_ref[...] + acc


def matmul2_residual(x1, x2, w, res, tm, tn):
    s, k1 = x1.shape
    k2 = x2.shape[1]
    n = w.shape[1]
    tm = min(tm, s)
    tn = min(tn, n)
    return pl.pallas_call(
        functools.partial(_mm2_res_kernel, k1=k1),
        out_shape=jax.ShapeDtypeStruct((s, n), F32),
        grid=(s // tm, n // tn),
        in_specs=[pl.BlockSpec((tm, k1), lambda i, j: (i, 0)),
                  pl.BlockSpec((tm, k2), lambda i, j: (i, 0)),
                  pl.BlockSpec((1, k1 + k2, tn), lambda i, j: (0, 0, j)),
                  pl.BlockSpec((tm, tn), lambda i, j: (i, j))],
        out_specs=pl.BlockSpec((tm, tn), lambda i, j: (i, j)),
        name="matmul2_residual",
        compiler_params=_cparams(("parallel", "arbitrary")),
    )(x1, x2, col_blocks(w, tn), res)


def _ffn1_kernel(x_ref, wg_ref, wu_ref, cw_ref, cb_ref, o_ref, carry_ref, *, ts):
    i = pl.program_id(0)
    j = pl.program_id(1)

    @pl.when(i == 0)
    def _():
        carry_ref[j] = jnp.zeros(carry_ref.shape[1:], F32)

    prev = carry_ref[j]
    wg = wg_ref[0].astype(BF16)
    wu = wu_ref[0].astype(BF16)
    cw = cw_ref[...]
    cb = cb_ref[...]
    row = lax.broadcasted_iota(jnp.int32, (ts, wg.shape[1]), 0)
    for r0 in range(0, x_ref.shape[0], ts):
        x = x_ref[r0:r0 + ts, :]
        g = _dot(x, wg)
        u = _dot(x, wu)
        p7 = prev[7:8, :]
        p6 = prev[6:7, :]
        g1 = jnp.where(row == 0, p7, pltpu.roll(g, 1, 0))
        g2 = jnp.where(row == 0, p6, jnp.where(row == 1, p7, pltpu.roll(g, 2, 0)))
        gc = g2 * cw[0:1, :] + g1 * cw[1:2, :] + g * cw[2:3, :] + cb
        o_ref[r0:r0 + ts, :] = ((gc * jax.nn.sigmoid(gc)) * u).astype(o_ref.dtype)
        prev = g[ts - 8:ts, :]
    carry_ref[j] = prev


def ffn_gate_up(h, wg, wu, layer, conv_w, conv_b):
    s, d = h.shape
    f = wg.shape[2]
    tm = min(2048, s)
    tn = 256
    nj = f // tn
    return pl.pallas_call(
        functools.partial(_ffn1_kernel, ts=min(ROW_SUB, tm)),
        out_shape=jax.ShapeDtypeStruct((s, f), BF16),
        grid=(s // tm, nj),
        in_specs=[pl.BlockSpec((tm, d), lambda i, j: (i, 0), pipeline_mode=pl.Buffered(1)),
                  pl.BlockSpec((1, d, tn), lambda i, j: (layer, 0, j)),
                  pl.BlockSpec((1, d, tn), lambda i, j: (layer, 0, j)),
                  pl.BlockSpec((3, tn), lambda i, j: (0, j)),
                  pl.BlockSpec((1, tn), lambda i, j: (0, j))],
        out_specs=pl.BlockSpec((tm, tn), lambda i, j: (i, j)),
        scratch_shapes=[pltpu.VMEM((nj, 8, tn), F32)],
        name="ffn_gate_up",
        compiler_params=_cparams(("arbitrary", "arbitrary")),
    )(h, wg, wu, conv_w.astype(F32), conv_b.reshape(1, f).astype(F32))


def _flash_kernel(qi_ref, kj_ref, bi_ref, fl_ref, *refs, hb, nv, use_sel, rc, diff_lam_init):
    q_ref, k_ref, v_ref, b_ref = refs[:4]
    refs = refs[4:]
    if use_sel:
        sel_ref, en_ref = refs[:2]
        refs = refs[2:]
    if diff_lam_init is not None:
        lam_ref, gain_ref = refs[:2]
        refs = refs[2:]
    o_ref, m_sc, acc_sc, s_sc, p_sc, al_sc = refs[:6]
    if use_sel:
        qa_sc = refs[6]
    per_head_k = k_ref.shape[0] == hb and hb > 1
    per_head_bias = b_ref.shape[0] == hb
    p = pl.program_id(1)
    flags = fl_ref[p]
    dv = nv * LANES
    t = q_ref.shape[1]
    tk = k_ref.shape[1]

    @pl.when((flags & 1) != 0)
    def _():
        m_sc[...] = jnp.full(m_sc.shape, -jnp.inf, F32)
        acc_sc[...] = jnp.zeros(acc_sc.shape, F32)
        if use_sel:
            notsel = (1.0 - sel_ref[0].astype(F32)).astype(BF16)
            for h in range(hb):
                qa_sc[h, :, 0:LANES] = q_ref[h]
                qa_sc[h, :, LANES:] = notsel

    v_aug = jnp.concatenate([v_ref[c] for c in range(nv)] + [jnp.ones((tk, LANES), BF16)], axis=-1)

    for h in range(hb):
        k = k_ref[h if per_head_k else 0]
        if use_sel:
            k = jnp.concatenate([k, en_ref[0]], axis=-1)
        qh = qa_sc[h] if use_sel else q_ref[h]
        s_sc[h] = _dot_nt(qh, k)
        hbias = h if per_head_bias else 0
        for r0 in range(0, t, rc):
            rows = slice(r0, r0 + rc)
            s = s_sc[h, rows, :] + b_ref[hbias, 0, rows, :]
            m_old = m_sc[h, rows, :]
            m_new = jnp.maximum(m_old, jnp.max(s, axis=-1, keepdims=True))
            al_sc[h, rows, :] = jnp.exp(m_old - m_new)
            m_sc[h, rows, :] = m_new
            p_sc[h, rows, :] = jnp.exp(s - jnp.tile(m_new, (1, tk // LANES))).astype(BF16)
        acc_sc[h] = jnp.tile(al_sc[h], (1, nv + 1)) * acc_sc[h] + _dot(p_sc[h], v_aug)

    def normalised(h):
        acc = acc_sc[h]
        return acc[:, 0:dv] / jnp.tile(acc[:, dv:], (1, nv))

    @pl.when((flags & 2) != 0)
    def _():
        if diff_lam_init is None:
            for h in range(hb):
                o_ref[:, h * dv:(h + 1) * dv] = normalised(h).astype(o_ref.dtype)
        else:
            lf = lam_ref[...]
            s1 = jnp.sum(lf[0:1, :] * lf[1:2, :], axis=-1, keepdims=True)
            s2 = jnp.sum(lf[2:3, :] * lf[3:4, :], axis=-1, keepdims=True)
            lam = jnp.exp(s1) - jnp.exp(s2) + diff_lam_init
            o = normalised(0) - lam * normalised(1)
            ms = jnp.mean(o * o, axis=-1, keepdims=True)
            y = ((o * lax.rsqrt(ms + EPS)) * gain_ref[...]) * (1.0 - diff_lam_init)
            o_ref[...] = y.astype(o_ref.dtype)


def _pairs(nq, lookback, max_bias, kv_mult):
    assert lookback is None or kv_mult == 1
    qi, kj, bi, fl = [], [], [], []
    for q in range(nq):
        lo = 0 if lookback is None else max(0, q - lookback)
        hi = q // kv_mult
        for k in range(lo, hi + 1):
            qi.append(q)
            kj.append(k)
            bi.append(min(q - kv_mult * k, max_bias))
            fl.append((1 if k == lo else 0) | (2 if k == hi else 0))
    return [jnp.asarray(np.asarray(a, np.int32)) for a in (qi, kj, bi, fl)]


def flash_attention(q, k, v, bias, *, name, t, n_groups, hb, nv, q_map, k_map, v_map, b_map,
                    lookback=None, sel=None, key_neg=None, diff_params=None):
    s = q.shape[1]
    nq = s // t
    tk = bias.shape[3]
    qi, kj, bi, fl = _pairs(nq, lookback, bias.shape[1] - 1, tk // t)
    n_pairs = int(qi.shape[0])
    dv = nv * LANES
    use_sel = sel is not None
    diff = diff_params is not None
    in_specs = [
        pl.BlockSpec((hb, t, LANES), lambda g, p, qi, kj, bi, fl: q_map(g, qi[p])),
        pl.BlockSpec((hb if diff else 1, tk, LANES), lambda g, p, qi, kj, bi, fl: k_map(g, kj[p])),
        pl.BlockSpec((nv, tk, LANES), lambda g, p, qi, kj, bi, fl: v_map(g, kj[p])),
        pl.BlockSpec((1 if diff else hb, 1, t, tk), lambda g, p, qi, kj, bi, fl: b_map(g, bi[p])),
    ]
    args = [q, k, v, bias]
    scratch = [pltpu.VMEM((hb, t, LANES), F32), pltpu.VMEM((hb, t, dv + LANES), F32),
               pltpu.VMEM((hb, t, tk), F32), pltpu.VMEM((hb, t, tk), BF16), pltpu.VMEM((hb, t, LANES), F32)]
    if use_sel:
        nselp = sel.shape[2]
        in_specs += [
            pl.BlockSpec((1, t, nselp), lambda g, p, qi, kj, bi, fl: (g, qi[p], 0)),
            pl.BlockSpec((1, tk, nselp), lambda g, p, qi, kj, bi, fl: (kj[p], 0, 0)),
        ]
        args += [sel, key_neg]
        scratch.append(pltpu.VMEM((hb, t, LANES + nselp), BF16))
    lam_init = None
    if diff:
        a_lambda, subln_gain, lam_init = diff_params
        in_specs += [
            pl.BlockSpec((4, HEAD_DIM), lambda g, p, qi, kj, bi, fl: (0, 0)),
            pl.BlockSpec((1, dv), lambda g, p, qi, kj, bi, fl: (0, 0)),
        ]
        args += [a_lambda.astype(F32), subln_gain.reshape(1, dv).astype(F32)]
    out_w = dv if diff else hb * dv
    grid_spec = pltpu.PrefetchScalarGridSpec(
        num_scalar_prefetch=4,
        grid=(n_groups, n_pairs),
        in_specs=in_specs,
        out_specs=pl.BlockSpec((t, out_w), lambda g, p, qi, kj, bi, fl: (qi[p], g)),
        scratch_shapes=scratch,
    )
    rc = max(8, min(t, (16 * 8 * LANES) // tk))
    return pl.pallas_call(
        functools.partial(_flash_kernel, hb=hb, nv=nv, use_sel=use_sel, rc=rc, diff_lam_init=lam_init),
        out_shape=jax.ShapeDtypeStruct((s, n_groups * out_w), BF16),
        grid_spec=grid_spec,
        name=name,
        compiler_params=_cparams(("parallel", "arbitrary")),
    )(qi, kj, bi, fl, *args)


def _dilated_kernel(q_ref, kp_ref, kc_ref, vp_ref, vc_ref, b_ref, o_ref, lse_ref, s_sc, p_sc, m_sc):
    span = q_ref.shape[1]
    rc = min(64, span)
    no_prev = jnp.where(pl.program_id(1) == 0, NEG, 0.0)
    col = lax.broadcasted_iota(jnp.int32, (1, 2 * span), 1)
    prev_mask = jnp.where(col < span, no_prev, 0.0)
    ones = jnp.ones((2 * span, LANES), BF16)
    for h in range(B_HEADS):
        keys = jnp.concatenate([kp_ref[h], kc_ref[h]], axis=0)
        s_sc[h] = _dot_nt(q_ref[h], keys)
        for r0 in range(0, span, rc):
            rows = slice(r0, r0 + rc)
            s = s_sc[h, rows, :] + (b_ref[h, 0, rows, :] + prev_mask)
            m = jnp.max(s, axis=-1, keepdims=True)
            m_sc[h, rows, :] = jnp.broadcast_to(m, (rc, LANES))
            p_sc[h, rows, :] = jnp.exp(s - m).astype(BF16)
        vals = jnp.concatenate([jnp.concatenate([vp_ref[h], vc_ref[h]], axis=0), ones], axis=-1)
        pv = _dot(p_sc[h], vals)
        l = pv[:, HEAD_DIM:]
        sl = slice(h * HEAD_DIM, (h + 1) * HEAD_DIM)
        o_ref[:, sl] = (pv[:, 0:HEAD_DIM] / l).astype(o_ref.dtype)
        lse_ref[:, sl] = m_sc[h] + jnp.log(l)


def dilated_group(x, bias, dilation, span, q_blk, k_blk, v_blk):
    nh, l, _ = x.shape
    s = l * dilation
    nb = l // span
    w = B_HEADS * HEAD_DIM
    hspec = lambda blk, prev: pl.BlockSpec(
        (B_HEADS, span, LANES),
        (lambda r, n: (blk, jnp.maximum(n - 1, 0), r)) if prev else (lambda r, n: (blk, n, r)))
    o, lse = pl.pallas_call(
        _dilated_kernel,
        out_shape=[jax.ShapeDtypeStruct((l, dilation * w), BF16), jax.ShapeDtypeStruct((l, dilation * w), F32)],
        grid=(dilation, nb),
        in_specs=[hspec(q_blk, False), hspec(k_blk, True), hspec(k_blk, False),
                  hspec(v_blk, True), hspec(v_blk, False),
                  pl.BlockSpec((B_HEADS, 1, span, 2 * span), lambda r, n: (0, 0, 0, 0))],
        out_specs=[pl.BlockSpec((span, w), lambda r, n: (n, r))] * 2,
        scratch_shapes=[pltpu.VMEM((B_HEADS, span, 2 * span), F32),
                        pltpu.VMEM((B_HEADS, span, 2 * span), BF16),
                        pltpu.VMEM((B_HEADS, span, LANES), F32)],
        name="dilated_group",
        compiler_params=_cparams(("parallel", "parallel")),
    )(x, x, x, x, x, bias)
    return o, lse


def _mix_kernel(*refs, dilations):
    n = len(dilations)
    o_refs, l_refs = refs[:n], refs[n:2 * n]
    perm_refs = dict(zip([d for d in dilations if d > 1], refs[2 * n:-1]))
    out_ref = refs[-1]
    w = out_ref.shape[1]

    def token_major(ref, d, exact_f32):
        blk = ref[...]
        if d == 1:
            return blk.astype(F32)
        stacked = jnp.concatenate([blk[:, r * w:(r + 1) * w] for r in range(d)], axis=0)
        perm_t = perm_refs[d][...]
        if not exact_f32:
            return _dot(perm_t, stacked)
        hi = stacked.astype(BF16)
        r1 = stacked - hi.astype(F32)
        mid = r1.astype(BF16)
        lo = (r1 - mid.astype(F32)).astype(BF16)
        return (_dot(perm_t, hi) + _dot(perm_t, mid)) + _dot(perm_t, lo)

    lses = [token_major(l_refs[g], d, True) for g, d in enumerate(dilations)]
    m = functools.reduce(jnp.maximum, lses)
    es = [jnp.exp(l - m) for l in lses]
    num = sum(e * token_major(o_refs[g], d, False) for g, (e, d) in enumerate(zip(es, dilations)))
    out_ref[...] = (num / sum(es)).astype(out_ref.dtype)


def mix_dilated(os, lses, dilations):
    w = os[0].shape[1] // dilations[0]
    s = os[0].shape[0] * dilations[0]
    ts = min(256, s)
    perms = []
    for d in dilations:
        if d > 1:
            tok = np.arange(ts)
            stacked_row = (tok % d) * (ts // d) + tok // d
            perms.append(jnp.asarray((stacked_row[:, None] == np.arange(ts)[None, :]).astype(np.float32), BF16))
    specs = [pl.BlockSpec((ts // d, d * w), lambda i: (i, 0)) for d in dilations]
    return pl.pallas_call(
        functools.partial(_mix_kernel, dilations=tuple(dilations)),
        out_shape=jax.ShapeDtypeStruct((s, w), BF16),
        grid=(s // ts,),
        in_specs=specs + specs + [pl.BlockSpec((ts, ts), lambda i: (0, 0))] * len(perms),
        out_specs=pl.BlockSpec((ts, w), lambda i: (i, 0)),
        name="mix_dilated",
        compiler_params=_cparams(("parallel",)),
    )(*os, *lses, *perms)


def _compress_kernel(x_ref, pe_ref, w1_ref, w2_ref, gain_ref, o_ref):
    kv = pl.program_id(0)
    x = x_ref[0, 0].astype(F32)
    nc = x.shape[0]
    a = _dot((x + pe_ref[0, 0]).astype(BF16), w1_ref[0, 0])
    b = _dot((x + pe_ref[0, 1]).astype(BF16), w1_ref[0, 1])
    hid = jax.nn.gelu(a + pltpu.roll(b, nc - 1, 0))
    c = _dot(hid.astype(BF16), w2_ref[0])
    ms = jnp.mean(c * c, axis=-1, keepdims=True)
    normed = (c * lax.rsqrt(ms + EPS)) * gain_ref[...]
    o_ref[0, 0] = jnp.where(kv == 0, normed, c).astype(o_ref.dtype)


def compress_kv(xc, pe, w1, w2, gain):
    _, g, nc, cw = xc.shape
    hid = w1.shape[-1]
    return pl.pallas_call(
        _compress_kernel,
        out_shape=jax.ShapeDtypeStruct((2, g, nc, HEAD_DIM), BF16),
        grid=(2, g),
        in_specs=[pl.BlockSpec((1, 1, nc, cw), lambda a, b: (a, b, 0, 0)),
                  pl.BlockSpec((1, 2, 1, cw), lambda a, b: (a, 0, 0, 0)),
                  pl.BlockSpec((1, 2, cw, hid), lambda a, b: (a, 0, 0, 0)),
                  pl.BlockSpec((1, hid, HEAD_DIM), lambda a, b: (a, 0, 0)),
                  pl.BlockSpec((1, HEAD_DIM), lambda a, b: (0, 0))],
        out_specs=pl.BlockSpec((1, 1, nc, HEAD_DIM), lambda a, b: (a, b, 0, 0)),
        name="compress_kv",
        compiler_params=_cparams(("parallel", "parallel")),
    )(xc, pe, w1, w2, gain.reshape(1, HEAD_DIM).astype(F32))


def _cmp_kernel(q_ref, k_ref, v_ref, *rest, nkt, n_top):
    bias_refs = rest[:nkt]
    ov_ref, oc_ref, sel_ref, imp_sc, s_sc, p_sc, impn_sc = rest[nkt:]
    qi = pl.program_id(1)
    tq = q_ref.shape[1]
    tile_pos = LANES * CMP_STRIDE
    n_live = jnp.minimum(lax.shift_right_logical(qi * tq + (tq - CMP_LEN), int(math.log2(tile_pos))) + 1, nkt)

    def attend(nl):
        w = nl * LANES
        rc = max(8, min(tq, 1 << int(math.log2((16 * 8 * LANES) // w))))
        k = k_ref[0, 0, 0:w, :]
        v = v_ref[0, 0, 0:w, :]
        for h in range(C_HPG):
            s_sc[h, :, 0:w] = _dot_nt(q_ref[h], k)
            for r0 in range(0, tq, rc):
                rows = slice(r0, r0 + rc)
                if nl == 1:
                    bias = bias_refs[0][h, 0, rows, :]
                else:
                    bias = jnp.concatenate([b[h, 0, rows, :] for b in bias_refs[:nl]], axis=-1)
                s = s_sc[h, rows, 0:w] + bias
                m = jnp.max(s, axis=-1, keepdims=True)
                e = jnp.exp(s - m)
                pc = e * jnp.where(m > 0.5 * NEG, 1.0 / jnp.sum(e, axis=-1, keepdims=True), 0.0)
                p_sc[h, rows, 0:w] = pc.astype(BF16)
                if h == 0:
                    imp_sc[rows, 0:w] = pc
                else:
                    imp_sc[rows, 0:w] += pc
            oc_ref[:, h * HEAD_DIM:(h + 1) * HEAD_DIM] = _dot(p_sc[h, :, 0:w], v).astype(oc_ref.dtype)
        imp = imp_sc[:, 0:w]
        hi = imp.astype(BF16)
        lo = (imp - hi.astype(F32)).astype(BF16)
        ov = ov_ref[0:w, :]
        impn_sc[...] = _dot(hi, ov) + _dot(lo, ov)

    for nl in range(1, nkt + 1):
        pl.when(n_live == nl)(functools.partial(attend, nl))
    score_in = impn_sc[...].T
    shape = score_in.shape
    t = qi * tq + lax.broadcasted_iota(jnp.int32, shape, 1)
    n = lax.broadcasted_iota(jnp.int32, shape, 0)
    cur = lax.shift_right_logical(t, int(math.log2(SEL_LEN)))
    forced = (n == 0) | (n == cur) | (n == cur - 1)
    valid = n <= cur
    score = jnp.where(valid, jnp.where(forced, FORCE, score_in), NEG)
    nf = n.astype(F32)
    sel = jnp.zeros(shape, F32)
    for _ in range(n_top):
        m = jnp.max(score, axis=0, keepdims=True)
        first = jnp.min(jnp.where(score == m, nf, 1e9), axis=0, keepdims=True)
        pick = nf == first
        sel = jnp.where(pick, 1.0, sel)
        score = jnp.where(pick, -jnp.inf, score)
    sel_ref[0] = jnp.where(valid, sel, 0.0).T.astype(sel_ref.dtype)


def cmp_attention(q_hm, kvcmp, bias, overlap, tq, n_top):
    s = q_hm.shape[1]
    ncp = kvcmp.shape[2]
    nkt = ncp // LANES
    nselp = overlap.shape[1]
    nd = bias.shape[1]
    per_tile = (LANES * CMP_STRIDE) // tq

    def bias_spec(kt):
        return pl.BlockSpec(
            (C_HPG, 1, tq, LANES),
            lambda g, i: (g, jnp.clip(i - per_tile * kt, -1, nd - 2) + 1, 0, 0))

    w = C_HEADS * HEAD_DIM
    return pl.pallas_call(
        functools.partial(_cmp_kernel, nkt=nkt, n_top=n_top),
        out_shape=[jax.ShapeDtypeStruct((s, w), BF16),
                   jax.ShapeDtypeStruct((C_GROUPS, s, nselp), BF16)],
        grid=(C_GROUPS, s // tq),
        in_specs=[pl.BlockSpec((C_HPG, tq, LANES), lambda g, i: (g, i, 0)),
                  pl.BlockSpec((1, 1, ncp, LANES), lambda g, i: (0, g, 0, 0)),
                  pl.BlockSpec((1, 1, ncp, LANES), lambda g, i: (1, g, 0, 0))]
                 + [bias_spec(kt) for kt in range(nkt)]
                 + [pl.BlockSpec((ncp, nselp), lambda g, i: (0, 0))],
        out_specs=[pl.BlockSpec((tq, w // C_GROUPS), lambda g, i: (i, g)),
                   pl.BlockSpec((1, tq, nselp), lambda g, i: (g, i, 0))],
        scratch_shapes=[pltpu.VMEM((tq, ncp), F32), pltpu.VMEM((C_HPG, tq, ncp), F32),
                        pltpu.VMEM((C_HPG, tq, ncp), BF16), pltpu.VMEM((tq, nselp), F32)],
        name="cmp_attention",
        compiler_params=_cparams(("parallel", "parallel")),
    )(q_hm, kvcmp, kvcmp, *([bias] * nkt), overlap)


def _gate_kernel(oc_ref, os_ref, ow_ref, gt_ref, o_ref):
    gt = jax.nn.sigmoid(gt_ref[0].astype(F32))
    for h in range(C_HEADS):
        sl = slice(h * HEAD_DIM, (h + 1) * HEAD_DIM)
        g0 = gt[:, h:h + 1]
        g1 = gt[:, C_HEADS + h:C_HEADS + h + 1]
        g2 = gt[:, 2 * C_HEADS + h:2 * C_HEADS + h + 1]
        mixed = (g0 * oc_ref[:, sl].astype(F32) + g1 * os_ref[:, sl].astype(F32)
                 + g2 * ow_ref[:, sl].astype(F32))
        o_ref[:, sl] = mixed.astype(o_ref.dtype)


def gate_merge(oc, osel, ow, proj_hm, gate_blk):
    s, w = oc.shape
    ts = min(256, s)
    spec = pl.BlockSpec((ts, w), lambda i: (i, 0))
    return pl.pallas_call(
        _gate_kernel,
        out_shape=jax.ShapeDtypeStruct((s, w), BF16),
        grid=(s // ts,),
        in_specs=[spec, spec, spec, pl.BlockSpec((1, ts, LANES), lambda i: (gate_blk, i, 0))],
        out_specs=spec,
        name="gate_merge",
        compiler_params=_cparams(("parallel",)),
    )(oc, osel, ow, proj_hm)


def _tile_gain(g, reps, scale=1.0):
    return jnp.tile(g.astype(F32) * scale, reps)


def even_mixer(x, h, w_in, a_qk_gain, a_lambda, a_subln_gain, b_qk_gain, w_out, rel_table, lam_init):
    s = h.shape[0]
    scale = HEAD_DIM ** -0.5
    na = 2 * A_HEADS
    ones = lambda n: jnp.ones((n * HEAD_DIM,), F32)
    zeros = lambda n: jnp.zeros((n * HEAD_DIM,), F32)
    d = h.shape[1]
    wa = na * HEAD_DIM
    table_t = rel_table.astype(F32).T

    def pair_cols(w):
        return w.reshape(d, 2, A_HEADS, HEAD_DIM).transpose(0, 2, 1, 3).reshape(d, wa)

    w_a = jnp.concatenate([pair_cols(w_in[:, 0:wa]), pair_cols(w_in[:, wa:2 * wa]), w_in[:, 2 * wa:3 * wa]],
                          axis=1).astype(BF16)
    gain_a = jnp.concatenate([_tile_gain(a_qk_gain[0], na, scale), _tile_gain(a_qk_gain[1], na), ones(na)])
    flag_a = jnp.concatenate([ones(2 * na), zeros(na)])
    proj_a = proj_headmajor(h, w_a, gain_a, flag_a, tn=512)
    t = min(1024, s)
    bias_a = causal_bias_tiles(table_t[:A_HEADS], t, n_bias_tiles(t, t, s // t))
    ao = flash_attention(
        proj_a, proj_a, proj_a, bias_a, name="diff_attention", t=t, n_groups=A_HEADS, hb=2, nv=2,
        q_map=lambda g, qb: (g, qb, 0),
        k_map=lambda g, kb: (A_HEADS + g, kb, 0),
        v_map=lambda g, kb: (2 * A_HEADS + g, kb, 0),
        b_map=lambda g, bt: (g, bt, 0, 0),
        diff_params=(a_lambda, a_subln_gain, lam_init))

    gain_b = jnp.concatenate([_tile_gain(b_qk_gain[0], B_HEADS, scale), _tile_gain(b_qk_gain[1], B_HEADS),
                              ones(B_HEADS)])
    flag_b = jnp.concatenate([ones(2 * B_HEADS), zeros(B_HEADS)])
    dilations = sorted({dil for _, dil in B_CONFIGS if dil > 1})
    projs = proj_headmajor(h, w_in, gain_b, flag_b, tn=512, col0=3 * wa, dilations=dilations)
    proj_by_dilation = dict(zip([1] + dilations, projs if dilations else [projs]))
    hb0 = 0
    os, lses = [], []
    for window, dilation in B_CONFIGS:
        proj = proj_by_dilation[dilation]
        span = window // dilation
        j = 2 * span - np.arange(3 * span)
        g_b = _bias_by_distance(table_t[A_HEADS:A_HEADS + B_HEADS], j * dilation, (j >= 0) & (j <= span))
        bias_b = toeplitz_tiles(g_b[:, None, :], span, 2 * span, 1, span)
        o, lse = dilated_group(proj, bias_b, dilation, span, hb0, hb0 + 1, hb0 + 2)
        os.append(o)
        lses.append(lse)
    bo = mix_dilated(os, lses, [dil for _, dil in B_CONFIGS])
    return matmul2_residual(ao, bo, w_out, x, tm=1024, tn=512)


def nsa_mixer(x, h, w_in, c_qk_gain, c_cmp_pe, c_cmp_w1, c_cmp_w2, w_out, rel_table):
    s, d = h.shape
    scale = HEAD_DIM ** -0.5
    g = C_GROUPS
    n_in = w_in.shape[1]
    n_pad = -(-n_in // 640) * 640
    ones = lambda n: jnp.ones((n,), F32)
    zeros = lambda n: jnp.zeros((n,), F32)
    kvw = g * HEAD_DIM
    gain = jnp.concatenate([_tile_gain(c_qk_gain[0], C_HEADS, scale), ones(2 * kvw),
                            _tile_gain(c_qk_gain[2], g), ones(kvw), _tile_gain(c_qk_gain[3], g), ones(kvw),
                            ones(n_pad - C_HEADS * HEAD_DIM - 6 * kvw)])
    flag = jnp.concatenate([ones(C_HEADS * HEAD_DIM), zeros(2 * kvw), ones(kvw), zeros(kvw), ones(kvw),
                            zeros(kvw), zeros(n_pad - C_HEADS * HEAD_DIM - 6 * kvw)])
    w_pad = jnp.pad(w_in.astype(BF16), ((0, 0), (0, n_pad - n_in)))
    proj = proj_headmajor(h, w_pad, gain, flag, tn=640)
    kv0 = C_HEADS
    table_t = rel_table.astype(F32).T[:C_HEADS]

    nc = s // CMP_STRIDE
    cw = CMP_STRIDE * HEAD_DIM
    xc = proj[kv0:kv0 + 2 * g].reshape(2, g, nc, cw)
    pe = c_cmp_pe.astype(F32).reshape(2, 2, 1, cw)
    w1 = c_cmp_w1.astype(BF16).reshape(2, 2, cw, c_cmp_w1.shape[-1])
    kvcmp = compress_kv(xc, pe, w1, c_cmp_w2.astype(BF16), c_qk_gain[1])
    if nc % LANES:
        kvcmp = jnp.pad(kvcmp, ((0, 0), (0, 0), (0, LANES - nc % LANES), (0, 0)))
    ncp = kvcmp.shape[2]

    tq = min(256, s)
    n_cmp = (s - CMP_LEN) // CMP_STRIDE + 1
    n_sel = s // SEL_LEN
    nselp = -(-n_sel // LANES) * LANES
    n_top = min(SEL_TOPK, n_sel)
    off = CMP_STRIDE * (LANES - 1)
    far_tile = -(-(FAR + off + CMP_LEN - 1) // tq)
    nd = far_tile + 2
    wx = -(-(tq + off) // LANES) * LANES
    dist_c = np.arange(tq * (nd - 1) + wx) - (tq + off + CMP_LEN - 1)
    vec_c = _bias_by_distance(table_t, dist_c, dist_c >= 0)
    g_c = jnp.stack([vec_c[:, tq * d:tq * d + wx] for d in range(nd)], axis=1)
    bias_c = toeplitz_tiles(g_c, LANES, tq, CMP_STRIDE, off, transpose=True)
    c_start = np.arange(ncp) * CMP_STRIDE
    s_start = np.arange(nselp) * SEL_LEN
    ov = ((c_start[:, None] < s_start[None, :] + SEL_LEN) & (c_start[:, None] + CMP_LEN > s_start[None, :])
          & (np.arange(ncp)[:, None] < n_cmp) & (np.arange(nselp)[None, :] < n_sel))
    oc, sel = cmp_attention(proj, kvcmp, bias_c, jnp.asarray(ov.astype(np.float32), BF16), tq, n_top)

    t = tq
    kv_mult = 2 if s % (2 * t) == 0 else 1
    tk = kv_mult * t
    bias_s = causal_bias_tiles(table_t, t, n_bias_tiles(t, tk, s // t), kv_mult=kv_mult)
    key_blk = np.arange(s) // SEL_LEN
    key_neg = np.where(key_blk[:, None] == np.arange(nselp)[None, :], NEG, 0.0).reshape(s // tk, tk, nselp)
    osel = flash_attention(
        proj, proj, proj, bias_s, name="selected_attention", t=t, n_groups=g, hb=C_HPG, nv=1,
        q_map=lambda gg, qb: (gg, qb, 0),
        k_map=lambda gg, kb: (kv0 + 2 * g + gg, kb, 0),
        v_map=lambda gg, kb: (kv0 + 3 * g + gg, kb, 0),
        b_map=lambda gg, bt: (gg, bt, 0, 0),
        sel=sel, key_neg=jnp.asarray(key_neg.astype(np.float32), BF16))

    look = -(-(C_WINDOW - 1) // t)
    ndw = min(look + 1, s // t)
    bias_w = causal_bias_tiles(table_t, t, ndw, window=C_WINDOW)
    ow = flash_attention(
        proj, proj, proj, bias_w, name="window_attention", t=t, n_groups=g, hb=C_HPG, nv=1, lookback=look,
        q_map=lambda gg, qb: (gg, qb, 0),
        k_map=lambda gg, kb: (kv0 + 4 * g + gg, kb, 0),
        v_map=lambda gg, kb: (kv0 + 5 * g + gg, kb, 0),
        b_map=lambda gg, bt: (gg, bt, 0, 0))

    o = gate_merge(oc, osel, ow, proj, kv0 + 6 * g)
    return matmul_residual(o, w_out, x, tm=1024, tn=512)


def conv_ffn(x, h, w_gate, w_up, layer, conv_w, conv_b, w_down):
    act = ffn_gate_up(h, w_gate, w_up, layer, conv_w, conv_b)
    return matmul_residual(act, w_down, x, tm=1024, tn=256)


def kernel(x, rel_table, ev_norm, ev_w_in, a_qk_gain, a_lambda, a_subln_gain, b_qk_gain, ev_w_out,
           od_norm, od_w_in, c_qk_gain, c_cmp_pe, c_cmp_w1, c_cmp_w2, od_w_out,
           ffn_norm, ffn_w_gate, ffn_w_up, ffn_conv_w, ffn_conv_b, ffn_w_down):
    b, s, d = x.shape
    depth = ffn_norm.shape[0]
    outs = []
    for bi in range(b):
        y = x[bi].astype(F32)
        for i in range(depth):
            if i % 2 == 0:
                e = i // 2
                lam_init = 0.8 - 0.6 * math.exp(-0.3 * i)
                y = even_mixer(y, rmsnorm(y, ev_norm[e]), ev_w_in[e], a_qk_gain[e], a_lambda[e],
                               a_subln_gain[e], b_qk_gain[e], ev_w_out[e], rel_table, lam_init)
            else:
                o = i // 2
                y = nsa_mixer(y, rmsnorm(y, od_norm[o]), od_w_in[o], c_qk_gain[o], c_cmp_pe[o],
                              c_cmp_w1[o], c_cmp_w2[o], od_w_out[o], rel_table)
            y = conv_ffn(y, rmsnorm(y, ffn_norm[i]), ffn_w_gate, ffn_w_up, i, ffn_conv_w[i],
                         ffn_conv_b[i], ffn_w_down[i])
        outs.append(y)
    return jnp.stack(outs).astype(x.dtype)
```
